```python
import jax, jax.numpy as jnp
from jax import lax
import numpy as np

D_MODEL = 2048
BATCH = 8
SEQ = 2048
DEPTH = 1
DEC_BATCH = 128
DEC_SEQ = 4
PAST_LEN = 2048
PAGE_SIZE = 128

A_WIDTH = D_MODEL
A_GROUPS = 8
A_GROUP_DIM = A_WIDTH // A_GROUPS
CHUNK = 128
N_HEADS = 16
KV_HEADS = 4
HEAD_DIM = 128
Q_PER_KV = N_HEADS // KV_HEADS
ROPE_DIM = HEAD_DIM // 4
ROPE_THETA = 500000.0
CMP_BLOCK = 32
SLC_BLOCK = 64
CMP_PER_SLC = SLC_BLOCK // CMP_BLOCK
N_SELECT = 8
WINDOW = 512
Q_BLOCK = 128
SCALE = HEAD_DIM ** -0.5
N_GROUPS = 4
EXPERTS_PER_GROUP = 8
N_EXPERTS = N_GROUPS * EXPERTS_PER_GROUP
TOP_K_IN_GROUP = 2
EXPERT_DIM = 256
EPS = 1e-6
NEG = -1e30
FORCE_BONUS = 1e4
Q_COLS = N_HEADS * HEAD_DIM
KV_COLS = KV_HEADS * HEAD_DIM
IN_COLS = 2 * A_WIDTH + Q_COLS + 6 * KV_COLS + 3 * N_HEADS + 2 * D_MODEL

kernel_name = 'hybrid_gmlp_nsa_hmoe_step'


def rmsnorm(x, g):
    x32 = x.astype(jnp.float32)
    y = x32 * lax.rsqrt(jnp.mean(x32 * x32, axis=-1, keepdims=True) + EPS)
    return (y * g.astype(jnp.float32)).astype(x.dtype)


def layernorm(x, g, b):
    x32 = x.astype(jnp.float32)
    xc = x32 - jnp.mean(x32, axis=-1, keepdims=True)
    var = jnp.mean(xc * xc, axis=-1, keepdims=True)
    return (xc * lax.rsqrt(var + EPS) * g.astype(jnp.float32) + b.astype(jnp.float32)).astype(x.dtype)


def rope(x, pos):
    half = ROPE_DIM // 2
    inv = jnp.power(jnp.float32(ROPE_THETA), -jnp.arange(0, ROPE_DIM, 2, dtype=jnp.float32) / ROPE_DIM)
    ang = pos.astype(jnp.float32)[:, None] * inv[None, :]
    cos = jnp.cos(ang)[:, None, :]
    sin = jnp.sin(ang)[:, None, :]
    xr = x[..., :ROPE_DIM].astype(jnp.float32)
    x1, x2 = xr[..., :half], xr[..., half:]
    rot = jnp.concatenate([x1 * cos - x2 * sin, x2 * cos + x1 * sin], axis=-1).astype(x.dtype)
    return jnp.concatenate([rot, x[..., ROPE_DIM:]], axis=-1)


def split_cols(z):
    sizes = (A_WIDTH, A_WIDTH, Q_COLS) + (KV_COLS,) * 6 + (3 * N_HEADS, D_MODEL, D_MODEL)
    cuts = [int(c) for c in np.cumsum(sizes)[:-1]]
    return jnp.split(z, cuts, axis=-1)


def project(h, w_in, pos):
    bt, t, _ = h.shape
    z = jnp.einsum('btd,dc->btc', h, w_in)
    a_u, a_v, q, k_c, v_c, k_s, v_s, k_w, v_w, g_br, g_a, g_b = split_cols(z)
    def heads(a, n):
        return a.reshape(bt, t, n, HEAD_DIM)
    q = rope(heads(q, N_HEADS), pos).reshape(bt, t, KV_HEADS, Q_PER_KV, HEAD_DIM)
    kv = (rope(heads(k_c, KV_HEADS), pos), heads(v_c, KV_HEADS),
          rope(heads(k_s, KV_HEADS), pos), heads(v_s, KV_HEADS),
          rope(heads(k_w, KV_HEADS), pos), heads(v_w, KV_HEADS))
    return a_u, a_v, q, kv, g_br, g_a, g_b


def chunk_gmlp(a_u, a_v, ln_g, ln_b, w_s, b_s):
    bt, t, _ = a_u.shape
    ln = min(t, CHUNK)
    u = jax.nn.gelu(a_u)
    v = layernorm(jax.nn.gelu(a_v), ln_g, ln_b)
    vc = v.reshape(bt, t // ln, ln, A_GROUPS, A_GROUP_DIM)
    causal = jnp.tril(jnp.ones((ln, ln), dtype=bool))
    w = jnp.where(causal, w_s[:, :ln, :ln], 0.0).astype(v.dtype)
    bias = b_s[:, :ln].T[None, None, :, :, None].astype(v.dtype)
    s = jnp.einsum('gij,bcjgd->bcigd', w, vc) + bias
    return u * s.reshape(bt, t, A_WIDTH), v


def compress(rows, pe, w1, w2):
    bt, tc = rows.shape[:2]
    r = rows.reshape(bt, tc // CMP_BLOCK, CMP_BLOCK, KV_HEADS, HEAD_DIM) + pe[None, None, :, None, :]
    hid = jax.nn.gelu(jnp.einsum('bnjhd,jde->bnhe', r, w1))
    return jnp.einsum('bnhe,ef->bnhf', hid, w2)


def cmp_attend(q, q_pos, kc, vc):
    nc = kc.shape[1]
    s = jnp.einsum('bqhgd,bnhd->bhgqn', q, kc).astype(jnp.float32) * SCALE
    valid = ((jnp.arange(nc) + 1) * CMP_BLOCK - 1)[None, :] <= q_pos[:, None]
    p = jax.nn.softmax(jnp.where(valid, s, NEG), axis=-1) * valid
    o = jnp.einsum('bhgqn,bnhd->bqhgd', p.astype(vc.dtype), vc)
    return o, p.sum(axis=2)


def select_blocks(imp, q_pos, n_blocks):
    bt, kv, tq, nc = imp.shape
    imp = jnp.pad(imp, ((0, 0), (0, 0), (0, 0), (0, n_blocks * CMP_PER_SLC - nc)))
    imp = imp.reshape(bt, kv, tq, n_blocks, CMP_PER_SLC).sum(-1)
    j = jnp.arange(n_blocks)[None, :]
    forced = (j == 0) | (j == (q_pos // SLC_BLOCK)[:, None])
    future = j * SLC_BLOCK > q_pos[:, None]
    score = jnp.where(future, -1.0, imp + FORCE_BONUS * forced)
    _, idx = lax.top_k(score, min(N_SELECT, n_blocks))
    return idx


def gathered_attend(q, q_pos, kg, vg, kpos):
    s = jnp.einsum('bqhgd,bhqld->bhgql', q, kg).astype(jnp.float32) * SCALE
    valid = (kpos <= q_pos[None, None, :, None])[:, :, None]
    p = jax.nn.softmax(jnp.where(valid, s, NEG), axis=-1)
    return jnp.einsum('bhgql,bhqld->bqhgd', p.astype(vg.dtype), vg)


def window_attend(q, q_pos, k, v, k_pos):
    s = jnp.einsum('bqhgd,blhd->bhgql', q, k).astype(jnp.float32) * SCALE
    d = q_pos[:, None] - k_pos[None, :]
    valid = (d >= 0) & (d < WINDOW) & (k_pos[None, :] >= 0)
    p = jax.nn.softmax(jnp.where(valid, s, NEG), axis=-1)
    return jnp.einsum('bhgql,blhd->bqhgd', p.astype(v.dtype), v)


def nsa_prompt(q, kv, cmp_k, cmp_v):
    k_c, v_c, k_s, v_s, k_w, v_w = kv
    b, s = q.shape[:2]
    kc = compress(k_c, *cmp_k)
    vc = compress(v_c, *cmp_v)
    nsb = s // SLC_BLOCK
    kr = k_s.reshape(b, nsb, SLC_BLOCK, KV_HEADS, HEAD_DIM)
    vr = v_s.reshape(b, nsb, SLC_BLOCK, KV_HEADS, HEAD_DIM)
    zpad = jnp.zeros((b, WINDOW, KV_HEADS, HEAD_DIM), k_w.dtype)
    kwp = jnp.concatenate([zpad, k_w], axis=1)
    vwp = jnp.concatenate([zpad, v_w], axis=1)
    bi = jnp.arange(b)[:, None, None, None]
    hi = jnp.arange(KV_HEADS)[None, :, None, None]
    offs = jnp.arange(SLC_BLOCK)
    nqb = s // Q_BLOCK
    qb = jnp.moveaxis(q.reshape(b, nqb, Q_BLOCK, KV_HEADS, Q_PER_KV, HEAD_DIM), 1, 0)

    def block(args):
        q_blk, n = args
        start = n * Q_BLOCK
        q_pos = start + jnp.arange(Q_BLOCK)
        o_c, imp = cmp_attend(q_blk, q_pos, kc, vc)
        idx = select_blocks(imp, q_pos, nsb)
        kg = kr[bi, idx, :, hi, :].reshape(b, KV_HEADS, Q_BLOCK, -1, HEAD_DIM)
        vg = vr[bi, idx, :, hi, :].reshape(b, KV_HEADS, Q_BLOCK, -1, HEAD_DIM)
        kpos = (idx[..., None] * SLC_BLOCK + offs).reshape(b, KV_HEADS, Q_BLOCK, -1)
        o_s = gathered_attend(q_blk, q_pos, kg, vg, kpos)
        kwin = lax.dynamic_slice_in_dim(kwp, start, WINDOW + Q_BLOCK, axis=1)
        vwin = lax.dynamic_slice_in_dim(vwp, start, WINDOW + Q_BLOCK, axis=1)
        k_pos = start - WINDOW + jnp.arange(WINDOW + Q_BLOCK)
        o_w = window_attend(q_blk, q_pos, kwin, vwin, k_pos)
        return o_c, o_s, o_w

    oc, osl, ow = lax.map(block, (qb, jnp.arange(nqb)))
    def unblock(o):
        return jnp.moveaxis(o, 0, 1).reshape(b, s, KV_HEADS, Q_PER_KV, HEAD_DIM)
    return unblock(oc), unblock(osl), unblock(ow)


def nsa_sample(q, q_pos, kv, pool_k_cmp, pool_v_cmp, pool_k_slc, pool_v_slc, layer, page_table,
               kbuf, vbuf, cmp_k, cmp_v):
    k_c, v_c, k_s, v_s, k_w, v_w = kv
    db, t = q.shape[:2]
    past = page_table.shape[1] * PAGE_SIZE
    kp_rows = pool_k_cmp[layer, page_table].reshape(db, past, KV_HEADS, HEAD_DIM)
    vp_rows = pool_v_cmp[layer, page_table].reshape(db, past, KV_HEADS, HEAD_DIM)
    n_new_c = (t // CMP_BLOCK) * CMP_BLOCK
    kc = jnp.concatenate([compress(kp_rows, *cmp_k), compress(k_c[:, :n_new_c], *cmp_k)], axis=1)
    vc = jnp.concatenate([compress(vp_rows, *cmp_v), compress(v_c[:, :n_new_c], *cmp_v)], axis=1)
    o_c, imp = cmp_attend(q, q_pos, kc, vc)
    nb_past = past // SLC_BLOCK
    nb_new = -(-t // SLC_BLOCK)
    idx = select_blocks(imp, q_pos, nb_past + nb_new)
    sub = PAGE_SIZE // SLC_BLOCK
    bi = jnp.arange(db)[:, None, None, None]
    hi = jnp.arange(KV_HEADS)[None, :, None, None]
    offs = jnp.arange(SLC_BLOCK)
    is_past = (idx < nb_past)[..., None, None]
    pi = jnp.minimum(idx, nb_past - 1)
    phys = page_table[bi, pi // sub][..., None]
    rows = ((pi % sub) * SLC_BLOCK)[..., None] + offs
    ni = jnp.clip(idx - nb_past, 0, nb_new - 1)
    pad = ((0, 0), (0, nb_new * SLC_BLOCK - t), (0, 0), (0, 0))
    knb = jnp.pad(k_s, pad).reshape(db, nb_new, SLC_BLOCK, KV_HEADS, HEAD_DIM)
    vnb = jnp.pad(v_s, pad).reshape(db, nb_new, SLC_BLOCK, KV_HEADS, HEAD_DIM)

    def gather(pool, newb):
        from_pool = pool[layer, phys, rows, hi[..., None], :]
        from_new = newb[bi, ni, :, hi, :]
        return jnp.where(is_past, from_pool, from_new).reshape(db, KV_HEADS, t, -1, HEAD_DIM)

    kg = gather(pool_k_slc, knb)
    vg = gather(pool_v_slc, vnb)
    kpos = (idx[..., None] * SLC_BLOCK + offs).reshape(db, KV_HEADS, t, -1)
    o_s = gathered_attend(q, q_pos, kg, vg, kpos)
    wb = kbuf.shape[1]
    kwin = jnp.concatenate([kbuf, k_w], axis=1)
    vwin = jnp.concatenate([vbuf, v_w], axis=1)
    k_pos = past - wb + jnp.arange(wb + t)
    o_w = window_attend(q, q_pos, kwin, vwin, k_pos)
    return o_c, o_s, o_w, kwin[:, -wb:], vwin[:, -wb:]


def merge(a_out, o_c, o_s, o_w, g_br, g_a, g_b, w_pa, w_pb, w_o):
    bt, t = a_out.shape[:2]
    gb = jax.nn.sigmoid(g_br.astype(jnp.float32)).reshape(bt, t, KV_HEADS, Q_PER_KV, 3).astype(o_c.dtype)
    o = o_c * gb[..., 0:1] + o_s * gb[..., 1:2] + o_w * gb[..., 2:3]
    y_b = jnp.einsum('btc,cd->btd', o.reshape(bt, t, Q_COLS), w_pb)
    y_a = jnp.einsum('btc,cd->btd', a_out, w_pa)
    m = jax.nn.sigmoid(g_a) * y_a + jax.nn.sigmoid(g_b) * y_b
    return jnp.einsum('btd,de->bte', m, w_o)


def hier_moe(h, w_rg, b_rg, w_re, b_re, w_gate, w_up, w_down):
    x = h.reshape(-1, D_MODEL)
    n = x.shape[0]
    pg = jax.nn.softmax(jnp.einsum('nd,dg->ng', x, w_rg).astype(jnp.float32) + b_rg.astype(jnp.float32), axis=-1)
    pg_top, g_idx = lax.top_k(pg, 1)
    le = (jnp.einsum('nd,de->ne', x, w_re).astype(jnp.float32) + b_re.astype(jnp.float32))
    le = le.reshape(n, N_GROUPS, EXPERTS_PER_GROUP)[jnp.arange(n), g_idx[:, 0]]
    pe = jax.nn.softmax(le, axis=-1)
    pe_top, e_idx = lax.top_k(pe, TOP_K_IN_GROUP)
    w = pg_top * pe_top / jnp.sum(pe_top, axis=-1, keepdims=True)
    expert = g_idx * EXPERTS_PER_GROUP + e_idx
    gate = jnp.sum(jax.nn.one_hot(expert, N_EXPERTS, dtype=jnp.float32) * w[..., None], axis=1)
    hid = jax.nn.silu(jnp.einsum('nd,edf->nef', x, w_gate)) * jnp.einsum('nd,edf->nef', x, w_up)
    y = jnp.einsum('nef,efd->nd', hid * gate[..., None].astype(hid.dtype), w_down)
    return y.reshape(h.shape)


def setup_inputs(seed: int = 0) -> dict:
    key = jax.random.key(seed)
    ks = jax.random.split(key, 40)
    f32 = jnp.float32
    n_pages = PAST_LEN // PAGE_SIZE
    n_used = DEC_BATCH * n_pages
    n_phys = n_used + n_used // 4
    win_buf = min(WINDOW, PAST_LEN)
    def nrm(k, shape, scale):
        return jax.random.normal(k, shape, f32) * scale
    pool = (DEPTH, n_phys, PAGE_SIZE, KV_HEADS, HEAD_DIM)
    wbuf = (DEPTH, DEC_BATCH, win_buf, KV_HEADS, HEAD_DIM)
    page_table = jax.random.permutation(ks[8], n_phys)[:n_used].reshape(DEC_BATCH, n_pages).astype(jnp.int32)
    return {
        'x_prompt': nrm(ks[0], (BATCH, SEQ, D_MODEL), 1.0),
        'x_sample': nrm(ks[1], (DEC_BATCH, DEC_SEQ, D_MODEL), 1.0),
        'cache_k_cmp': nrm(ks[2], pool, 1.0),
        'cache_v_cmp': nrm(ks[3], pool, 1.0),
        'cache_k_slc': nrm(ks[4], pool, 1.0),
        'cache_v_slc': nrm(ks[5], pool, 1.0),
        'state_k_win': nrm(ks[6], wbuf, 1.0),
        'state_v_win': nrm(ks[7], wbuf, 1.0),
        'page_table': page_table,
        'g_mix': 1.0 + nrm(ks[9], (DEPTH, D_MODEL), 0.1),
        'w_in': nrm(ks[10], (DEPTH, D_MODEL, IN_COLS), D_MODEL ** -0.5),
        'a_ln_g': 1.0 + nrm(ks[11], (DEPTH, A_WIDTH), 0.1),
        'a_ln_b': nrm(ks[12], (DEPTH, A_WIDTH), 0.1),
        'a_w_s': nrm(ks[13], (DEPTH, A_GROUPS, CHUNK, CHUNK), CHUNK ** -0.5),
        'a_b_s': 1.0 + nrm(ks[14], (DEPTH, A_GROUPS, CHUNK), 0.1),
        'cmp_pe_k': nrm(ks[15], (DEPTH, CMP_BLOCK, HEAD_DIM), 0.1),
        'cmp_w1_k': nrm(ks[16], (DEPTH, CMP_BLOCK, HEAD_DIM, HEAD_DIM), (CMP_BLOCK * HEAD_DIM) ** -0.5),
        'cmp_w2_k': nrm(ks[17], (DEPTH, HEAD_DIM, HEAD_DIM), HEAD_DIM ** -0.5),
        'cmp_pe_v': nrm(ks[18], (DEPTH, CMP_BLOCK, HEAD_DIM), 0.1),
        'cmp_w1_v': nrm(ks[19], (DEPTH, CMP_BLOCK, HEAD_DIM, HEAD_DIM), (CMP_BLOCK * HEAD_DIM) ** -0.5),
        'cmp_w2_v': nrm(ks[20], (DEPTH, HEAD_DIM, HEAD_DIM), HEAD_DIM ** -0.5),
        'w_pa': nrm(ks[21], (DEPTH, A_WIDTH, D_MODEL), A_WIDTH ** -0.5),
        'w_pb': nrm(ks[22], (DEPTH, Q_COLS, D_MODEL), Q_COLS ** -0.5),
        'w_o': nrm(ks[23], (DEPTH, D_MODEL, D_MODEL), D_MODEL ** -0.5),
        'g_ffn': 1.0 + nrm(ks[24], (DEPTH, D_MODEL), 0.1),
        'w_rg': nrm(ks[25], (DEPTH, D_MODEL, N_GROUPS), D_MODEL ** -0.5),
        'b_rg': nrm(ks[26], (DEPTH, N_GROUPS), 0.01),
        'w_re': nrm(ks[27], (DEPTH, D_MODEL, N_EXPERTS), D_MODEL ** -0.5),
        'b_re': nrm(ks[28], (DEPTH, N_EXPERTS), 0.01),
        'w_gate': nrm(ks[29], (DEPTH, N_EXPERTS, D_MODEL, EXPERT_DIM), D_MODEL ** -0.5),
        'w_up': nrm(ks[30], (DEPTH, N_EXPERTS, D_MODEL, EXPERT_DIM), D_MODEL ** -0.5),
        'w_down': nrm(ks[31], (DEPTH, N_EXPERTS, EXPERT_DIM, D_MODEL), EXPERT_DIM ** -0.5),
        'g_final': 1.0 + nrm(ks[32], (D_MODEL,), 0.1),
    }


def reference(x_prompt, x_sample, cache_k_cmp, cache_v_cmp, cache_k_slc, cache_v_slc,
              state_k_win, state_v_win, page_table, g_mix, w_in, a_ln_g, a_ln_b, a_w_s, a_b_s,
              cmp_pe_k, cmp_w1_k, cmp_w2_k, cmp_pe_v, cmp_w1_v, cmp_w2_v,
              w_pa, w_pb, w_o, g_ffn, w_rg, b_rg, w_re, b_re, w_gate, w_up, w_down, g_final):
    s = x_prompt.shape[1]
    t = x_sample.shape[1]
    past = page_table.shape[1] * PAGE_SIZE
    pos_p = jnp.arange(s)
    pos_s = past + jnp.arange(t)
    wl = min(WINDOW, s)
    xp, xs = x_prompt, x_sample
    kcp, vcp, ksp, vsp, kwp, vwp = [], [], [], [], [], []
    kcs, vcs, kss, vss, kws, vws, ach = [], [], [], [], [], [], []
    for l in range(DEPTH):
        cmp_k = (cmp_pe_k[l], cmp_w1_k[l], cmp_w2_k[l])
        cmp_v = (cmp_pe_v[l], cmp_w1_v[l], cmp_w2_v[l])
        moe_w = (w_rg[l], b_rg[l], w_re[l], b_re[l], w_gate[l], w_up[l], w_down[l])
        out_w = (w_pa[l], w_pb[l], w_o[l])
        gm_w = (a_ln_g[l], a_ln_b[l], a_w_s[l], a_b_s[l])
        h = rmsnorm(xp, g_mix[l])
        a_u, a_v, q, kv, g_br, g_a, g_b = project(h, w_in[l], pos_p)
        a_out, _ = chunk_gmlp(a_u, a_v, *gm_w)
        o_c, o_s, o_w = nsa_prompt(q, kv, cmp_k, cmp_v)
        xp = xp + merge(a_out, o_c, o_s, o_w, g_br, g_a, g_b, *out_w)
        xp = xp + hier_moe(rmsnorm(xp, g_ffn[l]), *moe_w)
        kcp.append(kv[0]); vcp.append(kv[1]); ksp.append(kv[2]); vsp.append(kv[3])
        kwp.append(kv[4][:, -wl:]); vwp.append(kv[5][:, -wl:])
        h = rmsnorm(xs, g_mix[l])
        a_u, a_v, q, kv, g_br, g_a, g_b = project(h, w_in[l], pos_s)
        a_out, v_new = chunk_gmlp(a_u, a_v, *gm_w)
        o_c, o_s, o_w, kbuf, vbuf = nsa_sample(q, pos_s, kv, cache_k_cmp, cache_v_cmp, cache_k_slc,
                                               cache_v_slc, l, page_table, state_k_win[l],
                                               state_v_win[l], cmp_k, cmp_v)
        xs = xs + merge(a_out, o_c, o_s, o_w, g_br, g_a, g_b, *out_w)
        xs = xs + hier_moe(rmsnorm(xs, g_ffn[l]), *moe_w)
        kcs.append(kv[0]); vcs.append(kv[1]); kss.append(kv[2]); vss.append(kv[3])
        kws.append(kbuf); vws.append(vbuf); ach.append(v_new)
    y_prompt = rmsnorm(xp, g_final)
    y_sample = rmsnorm(xs, g_final)
    return (y_prompt, y_sample,
            jnp.stack(kcp), jnp.stack(vcp), jnp.stack(ksp), jnp.stack(vsp), jnp.stack(kwp), jnp.stack(vwp),
            jnp.stack(kcs), jnp.stack(vcs), jnp.stack(kss), jnp.stack(vss), jnp.stack(kws), jnp.stack(vws),
            jnp.stack(ach))
```

```python
import functools

import jax
import jax.numpy as jnp
import numpy as np
from jax import lax
from jax.experimental import pallas as pl
from jax.experimental.pallas import tpu as pltpu

F32 = jnp.float32
MM = jnp.bfloat16

D_MODEL = 2048
A_WIDTH = 2048
A_GROUPS = 8
A_GROUP_DIM = A_WIDTH // A_GROUPS
CHUNK = 128
N_HEADS = 16
KV_HEADS = 4
HEAD_DIM = 128
Q_PER_KV = N_HEADS // KV_HEADS
ROPE_DIM = HEAD_DIM // 4
ROPE_THETA = 500000.0
CMP_BLOCK = 32
SLC_BLOCK = 64
_SLC_SHIFT = 6
N_SELECT = 8
WINDOW = 512
Q_BLOCK = 128
PAGE_SIZE = 128
SCALE = HEAD_DIM ** -0.5
N_GROUPS = 4
EXPERTS_PER_GROUP = 8
N_EXPERTS = N_GROUPS * EXPERTS_PER_GROUP
EXPERT_DIM = 256
EPS = 1e-6
NEG = -1e30
FORCE_BONUS = 1e4
Q_COLS = N_HEADS * HEAD_DIM
KV_COLS = KV_HEADS * HEAD_DIM

LANES = 128
NEW_ROWS = 16
TN = 512
VMEM_LIMIT = 56 * 1024 * 1024

_A_TILES = 3 * D_MODEL // TN
_KV_TILE0 = _A_TILES
_G_TILE0 = _A_TILES + 6
_G_TILES = 2 * D_MODEL // TN + 1
_N_TILES = _G_TILE0 + _G_TILES


def _cparams(sem):
    return pltpu.CompilerParams(dimension_semantics=sem, vmem_limit_bytes=VMEM_LIMIT)


def _rope_tile(z, cos, s1, s2):
    outs = []
    for h in range(z.shape[1] // HEAD_DIM):
        zh = z[:, h * HEAD_DIM:(h + 1) * HEAD_DIM]
        outs.append(zh * cos + pltpu.roll(zh, ROPE_DIM // 2, 1) * s1
                    + pltpu.roll(zh, HEAD_DIM - ROPE_DIM // 2, 1) * s2)
    return jnp.concatenate(outs, axis=1)


def _proj_kernel(x_ref, g_ref, w_ref, cos_ref, s1_ref, s2_ref,
                 a_ref, kc_ref, vc_ref, ks_ref, vs_ref, kw_ref, vw_ref, gt_ref, h_scr):
    j = pl.program_id(1)

    @pl.when(j == 0)
    def _():
        x = x_ref[...]
        ms = jnp.mean(x * x, axis=-1, keepdims=True)
        h_scr[...] = (x * lax.rsqrt(ms + EPS) * g_ref[...]).astype(MM)

    z = jnp.dot(h_scr[...], w_ref[...], preferred_element_type=F32)

    def rope(v):
        return _rope_tile(v, cos_ref[...], s1_ref[...], s2_ref[...])

    @pl.when(j < 2 * D_MODEL // TN)
    def _():
        a_ref[...] = z

    @pl.when((j >= 2 * D_MODEL // TN) & (j < _A_TILES))
    def _():
        a_ref[...] = rope(z)

    for t, (ref, roped) in enumerate(((kc_ref, True), (vc_ref, False), (ks_ref, True),
                                      (vs_ref, False), (kw_ref, True), (vw_ref, False))):
        @pl.when(j == _KV_TILE0 + t)
        def _(ref=ref, roped=roped):
            ref[...] = rope(z) if roped else z

    @pl.when(j >= _G_TILE0)
    def _():
        gt_ref[...] = z


def _project(x2d, g, w, tabs, tm):
    n, d = x2d.shape
    cos, s1, s2 = tabs
    tab_blocks = cos.shape[0] // tm
    row = lambda i, j: (i, 0)
    tab = lambda i, j: (i % tab_blocks, 0)
    kv_spec = pl.BlockSpec((tm, TN), row)
    kv_shape = jax.ShapeDtypeStruct((n, KV_COLS), F32)
    return pl.pallas_call(
        _proj_kernel,
        grid=(n // tm, _N_TILES),
        in_specs=[pl.BlockSpec((tm, d), row),
                  pl.BlockSpec((1, d), lambda i, j: (0, 0)),
                  pl.BlockSpec((d, TN), lambda i, j: (0, j)),
                  pl.BlockSpec((tm, LANES), tab), pl.BlockSpec((tm, LANES), tab),
                  pl.BlockSpec((tm, LANES), tab)],
        out_specs=[pl.BlockSpec((tm, TN), lambda i, j: (i, jnp.minimum(j, _A_TILES - 1)))]
                  + [kv_spec] * 6
                  + [pl.BlockSpec((tm, TN), lambda i, j: (i, jnp.clip(j - _G_TILE0, 0, _G_TILES - 1)))],
        out_shape=[jax.ShapeDtypeStruct((n, _A_TILES * TN), F32)] + [kv_shape] * 6
                  + [jax.ShapeDtypeStruct((n, _G_TILES * TN), F32)],
        scratch_shapes=[pltpu.VMEM((tm, d), MM)],
        compiler_params=_cparams(("parallel", "arbitrary")),
        name="norm_project",
    )(x2d, g, w, cos, s1, s2)


def _rope_tables(pos):
    half = ROPE_DIM // 2
    inv = jnp.power(jnp.float32(ROPE_THETA), -jnp.arange(0, ROPE_DIM, 2, dtype=F32) / ROPE_DIM)
    ang = pos.astype(F32)[:, None] * inv[None, :]
    cos, sin = jnp.cos(ang), jnp.sin(ang)
    r = pos.shape[0]
    one = jnp.ones((r, HEAD_DIM - ROPE_DIM), F32)
    zero = jnp.zeros((r, HEAD_DIM - ROPE_DIM), F32)
    zh = jnp.zeros((r, half), F32)
    return (jnp.concatenate([cos, cos, one], axis=1),
            jnp.concatenate([zh, sin, zero], axis=1),
            jnp.concatenate([-sin, zh, zero], axis=1))


def _layout_w_in(w_in):
    c0 = 2 * A_WIDTH + Q_COLS + 6 * KV_COLS
    main, g_br, g_ab = w_in[:, :c0], w_in[:, c0:c0 + 3 * N_HEADS], w_in[:, c0 + 3 * N_HEADS:]
    per_head = 3 * Q_PER_KV
    g_br = g_br.reshape(-1, KV_HEADS, per_head)
    g_br = jnp.pad(g_br, ((0, 0), (0, 0), (0, LANES - per_head))).reshape(-1, KV_HEADS * LANES)
    return jnp.concatenate([main, g_ab, g_br], axis=1).astype(MM)


def _gmlp_kernel(u_ref, v_ref, lng_ref, lnb_ref, mix_ref, bias_ref, o_ref, *maybe_vout):
    u = jax.nn.gelu(u_ref[...])
    v = jax.nn.gelu(v_ref[...])
    mu = jnp.mean(v, axis=-1, keepdims=True)
    vc = v - mu
    var = jnp.mean(vc * vc, axis=-1, keepdims=True)
    v = vc * lax.rsqrt(var + EPS) * lng_ref[...] + lnb_ref[...]
    if maybe_vout:
        maybe_vout[0][...] = v
    for g in range(A_GROUPS):
        sl = slice(g * A_GROUP_DIM, (g + 1) * A_GROUP_DIM)
        s = jnp.dot(mix_ref[g], v[:, sl].astype(MM), preferred_element_type=F32)
        b = bias_ref[g]
        s = s + jnp.concatenate([b] * (A_GROUP_DIM // LANES), axis=1)
        o_ref[:, sl] = (u[:, sl] * s).astype(o_ref.dtype)


def _gmlp(a, ln_g, ln_b, mix, bias, emit_v):
    n = a.shape[0]
    out_shape = [jax.ShapeDtypeStruct((n, A_WIDTH), MM)]
    out_specs = [pl.BlockSpec((CHUNK, A_WIDTH), lambda i: (i, 0))]
    if emit_v:
        out_shape.append(jax.ShapeDtypeStruct((n, A_WIDTH), F32))
        out_specs.append(pl.BlockSpec((CHUNK, A_WIDTH), lambda i: (i, 0)))
    const3 = lambda i: (0, 0, 0)
    return pl.pallas_call(
        _gmlp_kernel,
        grid=(n // CHUNK,),
        in_specs=[pl.BlockSpec((CHUNK, A_WIDTH), lambda i: (i, 0)),
                  pl.BlockSpec((CHUNK, A_WIDTH), lambda i: (i, 1)),
                  pl.BlockSpec((1, A_WIDTH), lambda i: (0, 0)),
                  pl.BlockSpec((1, A_WIDTH), lambda i: (0, 0)),
                  pl.BlockSpec((A_GROUPS, CHUNK, CHUNK), const3),
                  pl.BlockSpec((A_GROUPS, CHUNK, LANES), const3)],
        out_specs=out_specs,
        out_shape=out_shape,
        compiler_params=_cparams(("parallel",)),
        name="gmlp",
    )(a, a, ln_g, ln_b, mix, bias)


def _gmlp_mix(w_s, b_s, t):
    ln = min(t, CHUNK)
    causal = jnp.tril(jnp.ones((ln, ln), dtype=bool))
    w = jnp.where(causal, w_s[:, :ln, :ln], 0.0)
    reps = CHUNK // ln
    eye = jnp.eye(reps, dtype=w.dtype)
    mix = jnp.einsum('ab,gij->gaibj', eye, w).reshape(A_GROUPS, CHUNK, CHUNK)
    bias = jnp.tile(b_s[:, :ln], (1, reps))
    return mix.astype(MM), jnp.broadcast_to(bias[:, :, None], (A_GROUPS, CHUNK, LANES)).astype(F32)


def _compress_kernel(n_pages, pt_ref, *refs):
    k_pages, v_pages = refs[:n_pages], refs[n_pages:2 * n_pages]
    pek_ref, w1k_ref, w2k_ref, pev_ref, w1v_ref, w2v_ref, kc_ref, vc_ref, scr = refs[2 * n_pages:]
    blocks = n_pages * PAGE_SIZE // CMP_BLOCK
    half = blocks // 2

    def one(pages, pe_ref, w1_ref, w2_ref, out_ref):
        for p in range(n_pages):
            page = pages[p][0] + pe_ref[...]
            for h in range(KV_HEADS):
                scr[h, p * PAGE_SIZE:(p + 1) * PAGE_SIZE, :] = page[:, h * HEAD_DIM:(h + 1) * HEAD_DIM]
        acc = jnp.zeros((KV_HEADS * blocks, HEAD_DIM), F32)
        for jp in range(CMP_BLOCK // 2):
            parts = []
            for j in (2 * jp, 2 * jp + 1):
                rows = [scr[h, pl.ds(off + j, half, stride=2 * CMP_BLOCK), :]
                        for h in range(KV_HEADS) for off in (0, CMP_BLOCK)]
                parts.append(jnp.concatenate(rows, axis=0).astype(MM))
            lhs = jnp.concatenate(parts, axis=1)
            acc = acc + jnp.dot(lhs, w1_ref[jp], preferred_element_type=F32)
        hid = jax.nn.gelu(acc)
        out = jnp.dot(hid.astype(MM), w2_ref[...], preferred_element_type=F32)
        out_ref[0] = out.reshape(KV_HEADS, blocks, HEAD_DIM)

    one(k_pages, pek_ref, w1k_ref, w2k_ref, kc_ref)
    one(v_pages, pev_ref, w1v_ref, w2v_ref, vc_ref)


def _compress(pool_k, pool_v, page_table, cmp_k, cmp_v):
    n_seq, n_pages = page_table.shape
    blocks = n_pages * PAGE_SIZE // CMP_BLOCK

    def page_spec(p):
        return pl.BlockSpec((1, PAGE_SIZE, KV_COLS), lambda s, pt, p=p: (pt[s, p], 0, 0))

    def prep(c):
        pe, w1, w2 = c
        pe_t = jnp.tile(pe, (PAGE_SIZE // CMP_BLOCK, KV_HEADS)).astype(F32)
        return pe_t, w1.reshape(CMP_BLOCK // 2, 2 * HEAD_DIM, HEAD_DIM).astype(MM), w2.astype(MM)

    const2 = lambda s, pt: (0, 0)
    const3 = lambda s, pt: (0, 0, 0)
    w_specs = [pl.BlockSpec((PAGE_SIZE, KV_COLS), const2),
               pl.BlockSpec((CMP_BLOCK // 2, 2 * HEAD_DIM, HEAD_DIM), const3),
               pl.BlockSpec((HEAD_DIM, HEAD_DIM), const2)]
    out_spec = pl.BlockSpec((1, KV_HEADS, blocks, HEAD_DIM), lambda s, pt: (s, 0, 0, 0))
    out_shape = jax.ShapeDtypeStruct((n_seq, KV_HEADS, blocks, HEAD_DIM), F32)
    return pl.pallas_call(
        functools.partial(_compress_kernel, n_pages),
        grid_spec=pltpu.PrefetchScalarGridSpec(
            num_scalar_prefetch=1,
            grid=(n_seq,),
            in_specs=[page_spec(p) for p in range(n_pages)] * 2 + w_specs * 2,
            out_specs=[out_spec, out_spec],
            scratch_shapes=[pltpu.VMEM((KV_HEADS, n_pages * PAGE_SIZE, HEAD_DIM), F32)]),
        out_shape=[out_shape, out_shape],
        compiler_params=_cparams(("arbitrary",)),
        name="compress",
    )(page_table, *([pool_k] * n_pages), *([pool_v] * n_pages), *prep(cmp_k), *prep(cmp_v))


def _dot_nt(a, b):
    return lax.dot_general(a, b, (((1,), (1,)), ((), ())), preferred_element_type=F32)


def _cmp_branch(qb, qpos, kc, vc):
    half = kc.shape[0] // 2
    blk = lax.broadcasted_iota(jnp.int32, (1, half), 1)
    s_e = _dot_nt(qb, kc[:half].astype(MM)) * SCALE
    s_o = _dot_nt(qb, kc[half:].astype(MM)) * SCALE
    ok_e = ((2 * blk + 1) * CMP_BLOCK - 1) <= qpos
    ok_o = ((2 * blk + 2) * CMP_BLOCK - 1) <= qpos
    s_e = jnp.where(ok_e, s_e, NEG)
    s_o = jnp.where(ok_o, s_o, NEG)
    m = jnp.maximum(jnp.max(s_e, axis=1, keepdims=True), jnp.max(s_o, axis=1, keepdims=True))
    e_e = jnp.exp(s_e - m)
    e_o = jnp.exp(s_o - m)
    den = jnp.sum(e_e, axis=1, keepdims=True) + jnp.sum(e_o, axis=1, keepdims=True)
    p_e = jnp.where(ok_e, e_e / den, 0.0)
    p_o = jnp.where(ok_o, e_o / den, 0.0)
    o_c = (jnp.dot(p_e.astype(MM), vc[:half].astype(MM), preferred_element_type=F32)
           + jnp.dot(p_o.astype(MM), vc[half:].astype(MM), preferred_element_type=F32))
    return o_c, p_e, p_o


def _select(imp, qpos, n_blocks):
    j = lax.broadcasted_iota(jnp.int32, (1, n_blocks), 1)
    forced = (j == 0) | (j == (qpos >> _SLC_SHIFT))
    future = j * SLC_BLOCK > qpos
    score = jnp.where(future, -1.0, imp + jnp.where(forced, FORCE_BONUS, 0.0))
    rank = jnp.zeros(score.shape, jnp.int32)
    for i in range(n_blocks):
        ci = score[:, i:i + 1]
        beats = (ci > score) | ((ci == score) & (i < j))
        rank = rank + beats.astype(jnp.int32)
    return (rank < min(N_SELECT, n_blocks)).astype(F32)


def _softmax_parts(parts):
    m = functools.reduce(jnp.maximum, [jnp.max(s, axis=1, keepdims=True) for s, _ in parts])
    es = [jnp.exp(s - m) for s, _ in parts]
    den = functools.reduce(lambda a, b: a + b, [jnp.sum(e, axis=1, keepdims=True) for e in es])
    acc = None
    for e, (_, v) in zip(es, parts):
        o = jnp.dot((e / den).astype(MM), v, preferred_element_type=F32)
        acc = o if acc is None else acc + o
    return acc


def _prompt_attn_kernel(q_ref, kc_ref, vc_ref, ks_ref, vs_ref, kw_ref, vw_ref, gbr_ref, o_ref, selm_scr):
    n = pl.program_id(2)
    seq = ks_ref.shape[0]
    n_tiles = seq // Q_BLOCK
    n_sel_blocks = seq // SLC_BLOCK
    q = q_ref[...]
    qb = jnp.concatenate([q[:, g * HEAD_DIM:(g + 1) * HEAD_DIM] for g in range(Q_PER_KV)], axis=0).astype(MM)
    qpos1 = n * Q_BLOCK + lax.broadcasted_iota(jnp.int32, (Q_BLOCK, 1), 0)
    qpos = jnp.concatenate([qpos1] * Q_PER_KV, axis=0)

    o_c, p_e, p_o = _cmp_branch(qb, qpos, kc_ref[0, 0], vc_ref[0, 0])
    imp_e = functools.reduce(lambda a, b: a + b, [p_e[g * Q_BLOCK:(g + 1) * Q_BLOCK] for g in range(Q_PER_KV)])
    imp_o = functools.reduce(lambda a, b: a + b, [p_o[g * Q_BLOCK:(g + 1) * Q_BLOCK] for g in range(Q_PER_KV)])
    sel = _select(imp_e + imp_o, qpos1, n_sel_blocks).astype(MM)

    blk_of_key = lax.broadcasted_iota(jnp.int32, (n_sel_blocks, Q_BLOCK), 1) >> _SLC_SHIFT
    row_blk = lax.broadcasted_iota(jnp.int32, (n_sel_blocks, Q_BLOCK), 0)
    for kt in range(n_tiles):
        expand = (row_blk == blk_of_key + kt * (Q_BLOCK // SLC_BLOCK)).astype(MM)
        selm_scr[kt] = jnp.dot(sel, expand, preferred_element_type=F32)

    kcol = lax.broadcasted_iota(jnp.int32, (1, Q_BLOCK), 1)

    def flash(k_ref, v_ref, lo, hi, mask_fn):
        def body(kt, carry):
            m, l, acc = carry
            start = pl.multiple_of(kt * Q_BLOCK, Q_BLOCK)
            k = k_ref[pl.ds(start, Q_BLOCK), :].astype(MM)
            v = v_ref[pl.ds(start, Q_BLOCK), :].astype(MM)
            s = _dot_nt(qb, k) * SCALE
            s = jnp.where(mask_fn(kt, start + kcol), s, NEG)
            m_new = jnp.maximum(m, jnp.max(s, axis=1, keepdims=True))
            alpha = jnp.exp(m - m_new)
            p = jnp.exp(s - m_new)
            l = alpha * l + jnp.sum(p, axis=1, keepdims=True)
            acc = alpha * acc + jnp.dot(p.astype(MM), v, preferred_element_type=F32)
            return m_new, l, acc

        rows = Q_PER_KV * Q_BLOCK
        init = (jnp.full((rows, 1), NEG, F32), jnp.zeros((rows, 1), F32), jnp.zeros((rows, HEAD_DIM), F32))
        _, l, acc = lax.fori_loop(lo, hi, body, init)
        return acc / l

    def slc_mask(kt, kpos):
        sm = selm_scr[kt]
        sm = jnp.concatenate([sm] * Q_PER_KV, axis=0)
        return (sm > 0.5) & (kpos <= qpos)

    def win_mask(kt, kpos):
        d = qpos - kpos
        return (d >= 0) & (d < WINDOW)

    o_s = flash(ks_ref, vs_ref, 0, n + 1, slc_mask)
    o_w = flash(kw_ref, vw_ref, jnp.maximum(n - WINDOW // Q_BLOCK, 0), n + 1, win_mask)

    gate = jax.nn.sigmoid(gbr_ref[...])
    for g in range(Q_PER_KV):
        rs = slice(g * Q_BLOCK, (g + 1) * Q_BLOCK)
        o = (o_c[rs] * gate[:, 3 * g:3 * g + 1] + o_s[rs] * gate[:, 3 * g + 1:3 * g + 2]
             + o_w[rs] * gate[:, 3 * g + 2:3 * g + 3])
        o_ref[:, g * HEAD_DIM:(g + 1) * HEAD_DIM] = o.astype(o_ref.dtype)


def _prompt_attention(a, kc, vc, ks, vs, kw, vw, gt, batch, seq):
    n_qb = seq // Q_BLOCK
    q_col0 = 2 * A_WIDTH // (Q_PER_KV * HEAD_DIM)
    gbr_col0 = 2 * D_MODEL // LANES
    cmp_spec = pl.BlockSpec((1, 1, kc.shape[2], HEAD_DIM), lambda b, h, n: (b, h, 0, 0))
    kv_spec = pl.BlockSpec((seq, HEAD_DIM), lambda b, h, n: (b, h))
    return pl.pallas_call(
        _prompt_attn_kernel,
        grid=(batch, KV_HEADS, n_qb),
        in_specs=[pl.BlockSpec((Q_BLOCK, Q_PER_KV * HEAD_DIM), lambda b, h, n: (b * n_qb + n, q_col0 + h)),
                  cmp_spec, cmp_spec, kv_spec, kv_spec, kv_spec, kv_spec,
                  pl.BlockSpec((Q_BLOCK, LANES), lambda b, h, n: (b * n_qb + n, gbr_col0 + h))],
        out_specs=pl.BlockSpec((Q_BLOCK, Q_PER_KV * HEAD_DIM), lambda b, h, n: (b * n_qb + n, h)),
        out_shape=jax.ShapeDtypeStruct((batch * seq, Q_COLS), MM),
        scratch_shapes=[pltpu.VMEM((n_qb, Q_BLOCK, Q_BLOCK), F32)],
        compiler_params=_cparams(("parallel", "parallel", "arbitrary")),
        name="prompt_attention",
    )(a, kc, vc, ks, vs, kw, vw, gt)


def _sample_attn_kernel(n_pages, pt_ref, *refs):
    ks_pages, vs_pages = refs[:n_pages], refs[n_pages:2 * n_pages]
    (q_ref, kc_ref, vc_ref, kbuf_ref, vbuf_ref, ksn_ref, vsn_ref, kwn_ref, vwn_ref, gbr_ref,
     o_ref, kwo_ref, vwo_ref) = refs[2 * n_pages:]
    t = q_ref.shape[1]
    past = n_pages * PAGE_SIZE
    nb_past = past // SLC_BLOCK
    wb = kbuf_ref.shape[1]
    tpos = past + lax.broadcasted_iota(jnp.int32, (t, 1), 0)
    qpos = jnp.concatenate([tpos] * Q_PER_KV, axis=0)
    new_idx = lax.broadcasted_iota(jnp.int32, (1, NEW_ROWS), 1)
    new_pos = past + new_idx
    new_ok = (new_pos <= qpos) & (new_idx < t)
    q = q_ref[0]
    gate = jax.nn.sigmoid(gbr_ref[0])

    def new_rows(ref, hs):
        r = ref[0][:, hs]
        return jnp.concatenate([r, jnp.zeros((NEW_ROWS - t, HEAD_DIM), F32)], axis=0).astype(MM)

    kwo_ref[0, :wb - t, :] = kbuf_ref[0, t:, :]
    kwo_ref[0, wb - t:, :] = kwn_ref[0]
    vwo_ref[0, :wb - t, :] = vbuf_ref[0, t:, :]
    vwo_ref[0, wb - t:, :] = vwn_ref[0]

    buf_pos = past - wb + lax.broadcasted_iota(jnp.int32, (1, wb), 1)
    d_buf = qpos - buf_pos
    buf_ok = (d_buf >= 0) & (d_buf < WINDOW) & (buf_pos >= 0)
    d_new = qpos - new_pos
    win_new_ok = (d_new >= 0) & (d_new < WINDOW) & (new_idx < t)

    for h in range(KV_HEADS):
        hs = slice(h * HEAD_DIM, (h + 1) * HEAD_DIM)
        qb = jnp.concatenate([q[:, (h * Q_PER_KV + g) * HEAD_DIM:(h * Q_PER_KV + g + 1) * HEAD_DIM]
                              for g in range(Q_PER_KV)], axis=0).astype(MM)
        o_c, p_e, p_o = _cmp_branch(qb, qpos, kc_ref[0, h], vc_ref[0, h])
        imp_e = functools.reduce(lambda a, b: a + b, [p_e[g * t:(g + 1) * t] for g in range(Q_PER_KV)])
        imp_o = functools.reduce(lambda a, b: a + b, [p_o[g * t:(g + 1) * t] for g in range(Q_PER_KV)])
        imp = jnp.concatenate([imp_e + imp_o, jnp.zeros((t, 1), F32)], axis=1)
        sel = _select(imp, tpos, nb_past + 1)
        sel = jnp.concatenate([sel] * Q_PER_KV, axis=0)

        ksn = new_rows(ksn_ref, hs)
        vsn = new_rows(vsn_ref, hs)
        parts = []
        for p in range(n_pages):
            k = ks_pages[p][0][:, hs].astype(MM)
            v = vs_pages[p][0][:, hs].astype(MM)
            s = _dot_nt(qb, k) * SCALE
            per_page = PAGE_SIZE // SLC_BLOCK
            lane_blk = lax.broadcasted_iota(jnp.int32, (1, PAGE_SIZE), 1) >> _SLC_SHIFT
            ok = functools.reduce(
                lambda a, b: a | b,
                [(lane_blk == c) & (sel[:, p * per_page + c:p * per_page + c + 1] > 0.5)
                 for c in range(per_page)])
            parts.append((jnp.where(ok, s, NEG), v))
        s_new = _dot_nt(qb, ksn) * SCALE
        parts.append((jnp.where(new_ok & (sel[:, nb_past:nb_past + 1] > 0.5), s_new, NEG), vsn))
        o_s = _softmax_parts(parts)

        kb = kbuf_ref[0][:, hs].astype(MM)
        vb = vbuf_ref[0][:, hs].astype(MM)
        kwn = new_rows(kwn_ref, hs)
        vwn = new_rows(vwn_ref, hs)
        s_buf = jnp.where(buf_ok, _dot_nt(qb, kb) * SCALE, NEG)
        s_nw = jnp.where(win_new_ok, _dot_nt(qb, kwn) * SCALE, NEG)
        o_w = _softmax_parts([(s_buf, vb), (s_nw, vwn)])

        for g in range(Q_PER_KV):
            rs = slice(g * t, (g + 1) * t)
            c0 = h * LANES + 3 * g
            o = (o_c[rs] * gate[:, c0:c0 + 1] + o_s[rs] * gate[:, c0 + 1:c0 + 2]
                 + o_w[rs] * gate[:, c0 + 2:c0 + 3])
            col = (h * Q_PER_KV + g) * HEAD_DIM
            o_ref[0, :, col:col + HEAD_DIM] = o.astype(o_ref.dtype)


def _sample_attention(a3, kc, vc, pool_ks, pool_vs, page_table, kbuf, vbuf, ksn, vsn, kwn, vwn, gt3):
    db, t, _ = a3.shape
    n_pages = page_table.shape[1]
    wb = kbuf.shape[1]

    def page_spec(p):
        return pl.BlockSpec((1, PAGE_SIZE, KV_COLS), lambda s, pt, p=p: (pt[s, p], 0, 0))

    row3 = lambda s, pt: (s, 0, 0)
    cmp_spec = pl.BlockSpec((1, KV_HEADS, kc.shape[2], HEAD_DIM), lambda s, pt: (s, 0, 0, 0))
    buf_spec = pl.BlockSpec((1, wb, KV_COLS), row3)
    new_spec = pl.BlockSpec((1, t, KV_COLS), row3)
    return pl.pallas_call(
        functools.partial(_sample_attn_kernel, n_pages),
        grid_spec=pltpu.PrefetchScalarGridSpec(
            num_scalar_prefetch=1,
            grid=(db,),
            in_specs=[page_spec(p) for p in range(n_pages)] * 2
                     + [pl.BlockSpec((1, t, Q_COLS), lambda s, pt: (s, 0, 2 * A_WIDTH // Q_COLS)),
                        cmp_spec, cmp_spec, buf_spec, buf_spec, new_spec, new_spec, new_spec, new_spec,
                        pl.BlockSpec((1, t, KV_HEADS * LANES),
                                     lambda s, pt: (s, 0, 2 * D_MODEL // (KV_HEADS * LANES)))],
            out_specs=[pl.BlockSpec((1, t, Q_COLS), row3), buf_spec, buf_spec]),
        out_shape=[jax.ShapeDtypeStruct((db, t, Q_COLS), MM),
                   jax.ShapeDtypeStruct(kbuf.shape, F32), jax.ShapeDtypeStruct(vbuf.shape, F32)],
        compiler_params=_cparams(("arbitrary",)),
        name="sample_attention",
    )(page_table, *([pool_ks] * n_pages), *([pool_vs] * n_pages),
      a3, kc, vc, kbuf, vbuf, ksn, vsn, kwn, vwn, gt3)


def _mix_kernel(a_ref, o_ref, wpa_ref, wpb_ref, ga_ref, gb_ref, m_ref):
    y_a = jnp.dot(a_ref[...], wpa_ref[...], preferred_element_type=F32)
    y_b = jnp.dot(o_ref[...], wpb_ref[...], preferred_element_type=F32)
    m = jax.nn.sigmoid(ga_ref[...]) * y_a + jax.nn.sigmoid(gb_ref[...]) * y_b
    m_ref[...] = m.astype(m_ref.dtype)


def _mix(a_out, o, w_pa, w_pb, gt, tm):
    n = a_out.shape[0]
    gb0 = D_MODEL // TN
    return pl.pallas_call(
        _mix_kernel,
        grid=(n // tm, D_MODEL // TN),
        in_specs=[pl.BlockSpec((tm, A_WIDTH), lambda i, j: (i, 0)),
                  pl.BlockSpec((tm, Q_COLS), lambda i, j: (i, 0)),
                  pl.BlockSpec((A_WIDTH, TN), lambda i, j: (0, j)),
                  pl.BlockSpec((Q_COLS, TN), lambda i, j: (0, j)),
                  pl.BlockSpec((tm, TN), lambda i, j: (i, j)),
                  pl.BlockSpec((tm, TN), lambda i, j: (i, gb0 + j))],
        out_specs=pl.BlockSpec((tm, TN), lambda i, j: (i, j)),
        out_shape=jax.ShapeDtypeStruct((n, D_MODEL), MM),
        compiler_params=_cparams(("parallel", "arbitrary")),
        name="merge_gate",
    )(a_out, o, w_pa, w_pb, gt, gt)


def _out_proj_kernel(m_ref, w_ref, x_ref, o_ref):
    o_ref[...] = x_ref[...] + jnp.dot(m_ref[...], w_ref[...], preferred_element_type=F32)


def _out_proj(m, w_o, x2d, tm):
    n = m.shape[0]
    return pl.pallas_call(
        _out_proj_kernel,
        grid=(n // tm, D_MODEL // TN),
        in_specs=[pl.BlockSpec((tm, D_MODEL), lambda i, j: (i, 0)),
                  pl.BlockSpec((D_MODEL, TN), lambda i, j: (0, j)),
                  pl.BlockSpec((tm, TN), lambda i, j: (i, j))],
        out_specs=pl.BlockSpec((tm, TN), lambda i, j: (i, j)),
        out_shape=jax.ShapeDtypeStruct((n, D_MODEL), F32),
        compiler_params=_cparams(("parallel", "arbitrary")),
        name="out_proj",
    )(m, w_o, x2d)


def _route(logits):
    lane = lax.broadcasted_iota(jnp.int32, logits.shape, 1)
    big = jnp.int32(LANES)
    is_g = lane < N_GROUPS
    gl = jnp.where(is_g, logits, -jnp.inf)
    gm = jnp.max(gl, axis=1, keepdims=True)
    ge = jnp.exp(gl - gm)
    pg = ge / jnp.sum(ge, axis=1, keepdims=True)
    pg_top = jnp.max(pg, axis=1, keepdims=True)
    g_idx = jnp.min(jnp.where(is_g & (pg == pg_top), lane, big), axis=1, keepdims=True)
    in_grp = (lane >= N_GROUPS) & (((lane - N_GROUPS) >> 3) == g_idx)
    el = jnp.where(in_grp, logits, -jnp.inf)
    em = jnp.max(el, axis=1, keepdims=True)
    ee = jnp.exp(el - em)
    pe = ee / jnp.sum(ee, axis=1, keepdims=True)
    p1 = jnp.max(pe, axis=1, keepdims=True)
    i1 = jnp.min(jnp.where(in_grp & (pe == p1), lane, big), axis=1, keepdims=True)
    rest = in_grp & (lane != i1)
    p2 = jnp.max(jnp.where(rest, pe, -1.0), axis=1, keepdims=True)
    i2 = jnp.min(jnp.where(rest & (pe == p2), lane, big), axis=1, keepdims=True)
    tot = p1 + p2
    return jnp.where(lane == i1, pg_top * p1 / tot, 0.0) + jnp.where(lane == i2, pg_top * p2 / tot, 0.0)


def _moe_kernel(final_norm, x_ref, gf_ref, wr_hi_ref, wr_lo_ref, br_ref, wg_ref, wu_ref, wd_ref, gfin_ref,
                y_ref, h_scr, gate_scr, acc_scr):
    e = pl.program_id(1)

    @pl.when(e == 0)
    def _():
        x = x_ref[...]
        ms = jnp.mean(x * x, axis=-1, keepdims=True)
        h = x * lax.rsqrt(ms + EPS) * gf_ref[...]
        hi = h.astype(MM)
        lo = (h - hi.astype(F32)).astype(MM)
        logits = (jnp.dot(hi, wr_hi_ref[...], preferred_element_type=F32)
                  + (jnp.dot(hi, wr_lo_ref[...], preferred_element_type=F32)
                     + jnp.dot(lo, wr_hi_ref[...], preferred_element_type=F32))) + br_ref[...]
        h_scr[...] = hi
        gate_scr[...] = _route(logits)
        acc_scr[...] = jnp.zeros_like(acc_scr)

    h = h_scr[...]
    lane = lax.broadcasted_iota(jnp.int32, gate_scr.shape, 1)
    gcol = jnp.sum(jnp.where(lane == e + N_GROUPS, gate_scr[...], 0.0), axis=1, keepdims=True)
    hid = (jax.nn.silu(jnp.dot(h, wg_ref[0], preferred_element_type=F32))
           * jnp.dot(h, wu_ref[0], preferred_element_type=F32))
    acc_scr[...] += jnp.dot((hid * gcol).astype(MM), wd_ref[0], preferred_element_type=F32)

    @pl.when(e == pl.num_programs(1) - 1)
    def _():
        x2 = x_ref[...] + acc_scr[...]
        if final_norm:
            ms = jnp.mean(x2 * x2, axis=-1, keepdims=True)
            x2 = x2 * lax.rsqrt(ms + EPS) * gfin_ref[...]
        y_ref[...] = x2


def _moe(x1, g_ffn, wr_hi, wr_lo, b_r, w_gate, w_up, w_down, g_final, tm, final_norm):
    n = x1.shape[0]
    row = lambda i, e: (i, 0)
    const = lambda i, e: (0, 0)
    return pl.pallas_call(
        functools.partial(_moe_kernel, final_norm),
        grid=(n // tm, N_EXPERTS),
        in_specs=[pl.BlockSpec((tm, D_MODEL), row),
                  pl.BlockSpec((1, D_MODEL), const),
                  pl.BlockSpec((D_MODEL, LANES), const),
                  pl.BlockSpec((D_MODEL, LANES), const),
                  pl.BlockSpec((1, LANES), const),
                  pl.BlockSpec((1, D_MODEL, EXPERT_DIM), lambda i, e: (e, 0, 0)),
                  pl.BlockSpec((1, D_MODEL, EXPERT_DIM), lambda i, e: (e, 0, 0)),
                  pl.BlockSpec((1, EXPERT_DIM, D_MODEL), lambda i, e: (e, 0, 0)),
                  pl.BlockSpec((1, D_MODEL), const)],
        out_specs=pl.BlockSpec((tm, D_MODEL), row),
        out_shape=jax.ShapeDtypeStruct((n, D_MODEL), F32),
        scratch_shapes=[pltpu.VMEM((tm, D_MODEL), MM), pltpu.VMEM((tm, LANES), F32),
                        pltpu.VMEM((tm, D_MODEL), F32)],
        compiler_params=_cparams(("parallel", "arbitrary")),
        name="moe_final_norm",
    )(x1, g_ffn, wr_hi, wr_lo, b_r, w_gate, w_up, w_down, g_final)


def _router_weights(w_rg, b_rg, w_re, b_re):
    w = jnp.concatenate([w_rg, w_re], axis=1)
    w = jnp.pad(w, ((0, 0), (0, LANES - w.shape[1])))
    b = jnp.pad(jnp.concatenate([b_rg, b_re]), (0, LANES - N_GROUPS - N_EXPERTS))[None, :]
    hi = w.astype(MM)
    lo = (w - hi.astype(F32)).astype(MM)
    return hi, lo, b.astype(F32)


def _row_tile(n):
    return 512 if n % 512 == 0 else n


def kernel(x_prompt, x_sample, cache_k_cmp, cache_v_cmp, cache_k_slc, cache_v_slc, state_k_win, state_v_win, page_table, g_mix, w_in, a_ln_g, a_ln_b, a_w_s, a_b_s, cmp_pe_k, cmp_w1_k, cmp_w2_k, cmp_pe_v, cmp_w1_v, cmp_w2_v, w_pa, w_pb, w_o, g_ffn, w_rg, b_rg, w_re, b_re, w_gate, w_up, w_down, g_final):
    depth = g_mix.shape[0]
    b, s, d = x_prompt.shape
    db, t, _ = x_sample.shape
    n_pages = page_table.shape[1]
    past = n_pages * PAGE_SIZE
    wl = min(WINDOW, s)
    assert s % Q_BLOCK == 0 and CHUNK % t == 0 and (db * t) % CHUNK == 0 and t < CMP_BLOCK

    tabs_p = _rope_tables(jnp.arange(s))
    tabs_s = _rope_tables(past + (jnp.arange(db * t) % t))
    prompt_pages = jnp.arange(b * s // PAGE_SIZE, dtype=jnp.int32).reshape(b, s // PAGE_SIZE)
    g_final2 = g_final[None, :]

    xp = x_prompt.reshape(b * s, d)
    xs = x_sample.reshape(db * t, d)
    outs_p = [[] for _ in range(6)]
    outs_s = [[] for _ in range(7)]
    for l in range(depth):
        w_l = _layout_w_in(w_in[l])
        g_l = g_mix[l][None, :]
        ln_g, ln_b = a_ln_g[l][None, :], a_ln_b[l][None, :]
        cmp_k = (cmp_pe_k[l], cmp_w1_k[l], cmp_w2_k[l])
        cmp_v = (cmp_pe_v[l], cmp_w1_v[l], cmp_w2_v[l])
        wpa, wpb, wo = w_pa[l].astype(MM), w_pb[l].astype(MM), w_o[l].astype(MM)
        wr_hi, wr_lo, b_r = _router_weights(w_rg[l], b_rg[l], w_re[l], b_re[l])
        wg, wu, wd = w_gate[l].astype(MM), w_up[l].astype(MM), w_down[l].astype(MM)
        last = l == depth - 1

        def tail(x2d, a_out, o, gt):
            tm = _row_tile(x2d.shape[0])
            m = _mix(a_out, o, wpa, wpb, gt, tm)
            x1 = _out_proj(m, wo, x2d, tm)
            return _moe(x1, g_ffn[l][None, :], wr_hi, wr_lo, b_r, wg, wu, wd, g_final2, tm, last)

        a, kc_r, vc_r, ks_r, vs_r, kw_r, vw_r, gt = _project(xp, g_l, w_l, tabs_p, _row_tile(b * s))
        mix_p, bias_p = _gmlp_mix(a_w_s[l], a_b_s[l], s)
        (a_out,) = _gmlp(a, ln_g, ln_b, mix_p, bias_p, False)
        pages = lambda r: r.reshape(b * s // PAGE_SIZE, PAGE_SIZE, KV_COLS)
        kc, vc = _compress(pages(kc_r), pages(vc_r), prompt_pages, cmp_k, cmp_v)
        o = _prompt_attention(a, kc, vc, ks_r, vs_r, kw_r, vw_r, gt, b, s)
        xp = tail(xp, a_out, o, gt)
        heads = lambda r: r.reshape(b, s, KV_HEADS, HEAD_DIM)
        for lst, r in zip(outs_p, (kc_r, vc_r, ks_r, vs_r)):
            lst.append(heads(r))
        outs_p[4].append(heads(kw_r)[:, -wl:])
        outs_p[5].append(heads(vw_r)[:, -wl:])

        a, kc_r, vc_r, ks_r, vs_r, kw_r, vw_r, gt = _project(xs, g_l, w_l, tabs_s, _row_tile(db * t))
        mix_s, bias_s = _gmlp_mix(a_w_s[l], a_b_s[l], t)
        a_out, v_new = _gmlp(a, ln_g, ln_b, mix_s, bias_s, True)
        pool = lambda c: c[l].reshape(c.shape[1], PAGE_SIZE, KV_COLS)
        kc, vc = _compress(pool(cache_k_cmp), pool(cache_v_cmp), page_table, cmp_k, cmp_v)
        r3 = lambda r: r.reshape(db, t, r.shape[-1])
        wbuf = lambda st: st[l].reshape(db, st.shape[2], KV_COLS)
        o, kwin, vwin = _sample_attention(
            r3(a), kc, vc, pool(cache_k_slc), pool(cache_v_slc), page_table,
            wbuf(state_k_win), wbuf(state_v_win), r3(ks_r), r3(vs_r), r3(kw_r), r3(vw_r), r3(gt))
        xs = tail(xs, a_out, o.reshape(db * t, Q_COLS), gt)
        heads = lambda r: r.reshape(db, -1, KV_HEADS, HEAD_DIM)
        for lst, r in zip(outs_s, (kc_r, vc_r, ks_r, vs_r, kwin, vwin)):
            lst.append(heads(r))
        outs_s[6].append(v_new.reshape(db, t, A_WIDTH))

    y_prompt = xp.reshape(b, s, d)
    y_sample = xs.reshape(db, t, d)
    return (y_prompt, y_sample, *[jnp.stack(o) for o in outs_p], *[jnp.stack(o) for o in outs_s])
```

```python
import functools
import math

import jax
import jax.numpy as jnp
from jax import lax
from jax.experimental import pallas as pl
from jax.experimental.pallas import tpu as pltpu

F32 = jnp.float32
MM = jnp.bfloat16

D_MODEL = 2048
A_WIDTH = 2048
A_GROUPS = 8
A_GROUP_DIM = A_WIDTH // A_GROUPS
CHUNK = 128
N_HEADS = 16
KV_HEADS = 4
HEAD_DIM = 128
Q_PER_KV = N_HEADS // KV_HEADS
ROPE_DIM = HEAD_DIM // 4
ROPE_THETA = 500000.0
CMP_BLOCK = 32
SLC_BLOCK = 64
_SLC_SHIFT = 6
N_SELECT = 8
WINDOW = 512
Q_BLOCK = 128
PAGE_SIZE = 128
SCALE = HEAD_DIM ** -0.5
LOG2E = math.log2(math.e)
N_GROUPS = 4
EXPERTS_PER_GROUP = 8
N_EXPERTS = N_GROUPS * EXPERTS_PER_GROUP
EXPERT_DIM = 256
EPS = 1e-6
NEG = -1e30
FORCE_BONUS = 1e4
Q_COLS = N_HEADS * HEAD_DIM
KV_COLS = KV_HEADS * HEAD_DIM

LANES = 128
NEW_ROWS = 16
TN = 512
SEL_CHUNK = 512
VMEM_LIMIT = 56 * 1024 * 1024

_A_TILES = 3 * D_MODEL // TN
_G_TILES = 2 * D_MODEL // TN + 1
_KV_TILES = 6


def _cparams(sem):
    return pltpu.CompilerParams(dimension_semantics=sem, vmem_limit_bytes=VMEM_LIMIT)


def _rope_tile(z, cos, s1, s2):
    outs = []
    for h in range(z.shape[1] // HEAD_DIM):
        zh = z[:, h * HEAD_DIM:(h + 1) * HEAD_DIM]
        outs.append(zh * cos + pltpu.roll(zh, ROPE_DIM // 2, 1) * s1
                    + pltpu.roll(zh, HEAD_DIM - ROPE_DIM // 2, 1) * s2)
    return jnp.concatenate(outs, axis=1)


def _proj_a_kernel(x_ref, g_ref, w_ref, cos_ref, s1_ref, s2_ref, a_ref, gt_ref, h_ref):
    j = pl.program_id(1)

    @pl.when(j == 0)
    def _():
        x = x_ref[...]
        ms = jnp.mean(x * x, axis=-1, keepdims=True)
        h_ref[...] = (x * lax.rsqrt(ms + EPS) * g_ref[...]).astype(MM)

    z = jnp.dot(h_ref[...], w_ref[...], preferred_element_type=F32)

    @pl.when(j < 2 * D_MODEL // TN)
    def _():
        a_ref[...] = z

    @pl.when((j >= 2 * D_MODEL // TN) & (j < _A_TILES))
    def _():
        a_ref[...] = _rope_tile(z, cos_ref[...], s1_ref[...], s2_ref[...])

    @pl.when(j >= _A_TILES)
    def _():
        gt_ref[...] = z


def _project_a(x2d, g, w, tabs, tm):
    n, d = x2d.shape
    cos, s1, s2 = tabs
    tab_blocks = cos.shape[0] // tm
    row = lambda i, j: (i, 0)
    tab = lambda i, j: (i % tab_blocks, 0)
    return pl.pallas_call(
        _proj_a_kernel,
        grid=(n // tm, _A_TILES + _G_TILES),
        in_specs=[pl.BlockSpec((tm, d), row),
                  pl.BlockSpec((1, d), lambda i, j: (0, 0)),
                  pl.BlockSpec((d, TN), lambda i, j: (0, j)),
                  pl.BlockSpec((tm, LANES), tab), pl.BlockSpec((tm, LANES), tab),
                  pl.BlockSpec((tm, LANES), tab)],
        out_specs=[pl.BlockSpec((tm, TN), lambda i, j: (i, jnp.minimum(j, _A_TILES - 1))),
                   pl.BlockSpec((tm, TN), lambda i, j: (i, jnp.maximum(j - _A_TILES, 0))),
                   pl.BlockSpec((tm, d), row)],
        out_shape=[jax.ShapeDtypeStruct((n, _A_TILES * TN), F32),
                   jax.ShapeDtypeStruct((n, _G_TILES * TN), F32),
                   jax.ShapeDtypeStruct((n, d), MM)],
        compiler_params=_cparams(("parallel", "arbitrary")),
        name="norm_project",
    )(x2d, g, w, cos, s1, s2)


def _proj_kv_kernel(h_ref, w_ref, cos_ref, s1_ref, s2_ref,
                    kc_ref, vc_ref, ks_ref, vs_ref, kw_ref, vw_ref, ksb_ref, vsb_ref, kwb_ref, vwb_ref):
    j = pl.program_id(1)
    z = jnp.dot(h_ref[...], w_ref[...], preferred_element_type=F32)
    outs = ((kc_ref, None, True), (vc_ref, None, False), (ks_ref, ksb_ref, True),
            (vs_ref, vsb_ref, False), (kw_ref, kwb_ref, True), (vw_ref, vwb_ref, False))
    for t, (ref, packed_ref, roped) in enumerate(outs):
        @pl.when(j == t)
        def _(ref=ref, packed_ref=packed_ref, roped=roped):
            v = _rope_tile(z, cos_ref[...], s1_ref[...], s2_ref[...]) if roped else z
            for h in range(KV_HEADS):
                ref[:, h, :] = v[:, h * HEAD_DIM:(h + 1) * HEAD_DIM]
            if packed_ref is not None:
                packed_ref[...] = v.astype(MM)


def _project_kv(h, w, tabs, tm):
    n, d = h.shape
    cos, s1, s2 = tabs
    tab_blocks = cos.shape[0] // tm
    tab = lambda i, j: (i % tab_blocks, 0)
    return pl.pallas_call(
        _proj_kv_kernel,
        grid=(n // tm, _KV_TILES),
        in_specs=[pl.BlockSpec((tm, d), lambda i, j: (i, 0)),
                  pl.BlockSpec((d, TN), lambda i, j: (0, j)),
                  pl.BlockSpec((tm, LANES), tab), pl.BlockSpec((tm, LANES), tab),
                  pl.BlockSpec((tm, LANES), tab)],
        out_specs=[pl.BlockSpec((tm, KV_HEADS, HEAD_DIM), lambda i, j: (i, 0, 0))] * 6
                  + [pl.BlockSpec((tm, KV_COLS), lambda i, j: (i, 0))] * 4,
        out_shape=[jax.ShapeDtypeStruct((n, KV_HEADS, HEAD_DIM), F32)] * 6
                  + [jax.ShapeDtypeStruct((n, KV_COLS), MM)] * 4,
        compiler_params=_cparams(("parallel", "arbitrary")),
        name="kv_project",
    )(h, w, cos, s1, s2)


def _rope_tables(pos):
    half = ROPE_DIM // 2
    inv = jnp.power(jnp.float32(ROPE_THETA), -jnp.arange(0, ROPE_DIM, 2, dtype=F32) / ROPE_DIM)
    ang = pos.astype(F32)[:, None] * inv[None, :]
    cos, sin = jnp.cos(ang), jnp.sin(ang)
    r = pos.shape[0]
    one = jnp.ones((r, HEAD_DIM - ROPE_DIM), F32)
    zero = jnp.zeros((r, HEAD_DIM - ROPE_DIM), F32)
    zh = jnp.zeros((r, half), F32)
    return (jnp.concatenate([cos, cos, one], axis=1),
            jnp.concatenate([zh, sin, zero], axis=1),
            jnp.concatenate([-sin, zh, zero], axis=1))


def _layout_w_in(w_in):
    c_a = 2 * A_WIDTH + Q_COLS
    c_kv = c_a + 6 * KV_COLS
    g_br, g_ab = w_in[:, c_kv:c_kv + 3 * N_HEADS], w_in[:, c_kv + 3 * N_HEADS:]
    per_head = 3 * Q_PER_KV
    g_br = g_br.reshape(-1, KV_HEADS, per_head)
    g_br = jnp.pad(g_br, ((0, 0), (0, 0), (0, LANES - per_head))).reshape(-1, KV_HEADS * LANES)
    w_a = jnp.concatenate([w_in[:, :c_a], g_ab, g_br], axis=1).astype(MM)
    return w_a, w_in[:, c_a:c_kv].astype(MM)


def _gmlp_kernel(u_ref, v_ref, lng_ref, lnb_ref, mix_ref, bias_ref, o_ref, *maybe_vout):
    u = jax.nn.gelu(u_ref[...])
    v = jax.nn.gelu(v_ref[...])
    mu = jnp.mean(v, axis=-1, keepdims=True)
    vc = v - mu
    var = jnp.mean(vc * vc, axis=-1, keepdims=True)
    v = vc * lax.rsqrt(var + EPS) * lng_ref[...] + lnb_ref[...]
    if maybe_vout:
        maybe_vout[0][...] = v
    for g in range(A_GROUPS):
        sl = slice(g * A_GROUP_DIM, (g + 1) * A_GROUP_DIM)
        s = jnp.dot(mix_ref[g], v[:, sl].astype(MM), preferred_element_type=F32)
        b = bias_ref[g]
        s = s + jnp.concatenate([b] * (A_GROUP_DIM // LANES), axis=1)
        o_ref[:, sl] = (u[:, sl] * s).astype(o_ref.dtype)


def _gmlp(a, ln_g, ln_b, mix, bias, emit_v):
    n = a.shape[0]
    out_shape = [jax.ShapeDtypeStruct((n, A_WIDTH), MM)]
    out_specs = [pl.BlockSpec((CHUNK, A_WIDTH), lambda i: (i, 0))]
    if emit_v:
        out_shape.append(jax.ShapeDtypeStruct((n, A_WIDTH), F32))
        out_specs.append(pl.BlockSpec((CHUNK, A_WIDTH), lambda i: (i, 0)))
    const3 = lambda i: (0, 0, 0)
    return pl.pallas_call(
        _gmlp_kernel,
        grid=(n // CHUNK,),
        in_specs=[pl.BlockSpec((CHUNK, A_WIDTH), lambda i: (i, 0)),
                  pl.BlockSpec((CHUNK, A_WIDTH), lambda i: (i, 1)),
                  pl.BlockSpec((1, A_WIDTH), lambda i: (0, 0)),
                  pl.BlockSpec((1, A_WIDTH), lambda i: (0, 0)),
                  pl.BlockSpec((A_GROUPS, CHUNK, CHUNK), const3),
                  pl.BlockSpec((A_GROUPS, CHUNK, LANES), const3)],
        out_specs=out_specs,
        out_shape=out_shape,
        compiler_params=_cparams(("parallel",)),
        name="gmlp",
    )(a, a, ln_g, ln_b, mix, bias)


def _gmlp_mix(w_s, b_s, t):
    ln = min(t, CHUNK)
    causal = jnp.tril(jnp.ones((ln, ln), dtype=bool))
    w = jnp.where(causal, w_s[:, :ln, :ln], 0.0)
    reps = CHUNK // ln
    eye = jnp.eye(reps, dtype=w.dtype)
    mix = jnp.einsum('ab,gij->gaibj', eye, w).reshape(A_GROUPS, CHUNK, CHUNK)
    bias = jnp.tile(b_s[:, :ln], (1, reps))
    return mix.astype(MM), jnp.broadcast_to(bias[:, :, None], (A_GROUPS, CHUNK, LANES)).astype(F32)


def _compress_kernel(n_pages, pt_ref, *refs):
    k_pages, v_pages = refs[:n_pages], refs[n_pages:2 * n_pages]
    pek_ref, w1k_ref, w2k_ref, pev_ref, w1v_ref, w2v_ref, kc_ref, vc_ref, scr = refs[2 * n_pages:]
    blocks = n_pages * PAGE_SIZE // CMP_BLOCK
    half = blocks // 2

    def one(pages, pe_ref, w1_ref, w2_ref, out_ref):
        for p in range(n_pages):
            for h in range(KV_HEADS):
                scr[h, p * PAGE_SIZE:(p + 1) * PAGE_SIZE, :] = pages[p][0, 0, :, h, :] + pe_ref[...]
        acc = jnp.zeros((KV_HEADS * blocks, HEAD_DIM), F32)
        for jp in range(CMP_BLOCK // 2):
            parts = []
            for j in (2 * jp, 2 * jp + 1):
                rows = [scr[h, pl.ds(off + j, half, stride=2 * CMP_BLOCK), :]
                        for h in range(KV_HEADS) for off in (0, CMP_BLOCK)]
                parts.append(jnp.concatenate(rows, axis=0).astype(MM))
            lhs = jnp.concatenate(parts, axis=1)
            acc = acc + jnp.dot(lhs, w1_ref[jp], preferred_element_type=F32)
        hid = jax.nn.gelu(acc)
        out = jnp.dot(hid.astype(MM), w2_ref[...], preferred_element_type=F32)
        out_ref[0] = out.reshape(KV_HEADS, blocks, HEAD_DIM)

    one(k_pages, pek_ref, w1k_ref, w2k_ref, kc_ref)
    one(v_pages, pev_ref, w1v_ref, w2v_ref, vc_ref)


def _compress(pool_k, pool_v, layer, page_table, cmp_k, cmp_v):
    n_seq, n_pages = page_table.shape
    blocks = n_pages * PAGE_SIZE // CMP_BLOCK

    def page_spec(p):
        return pl.BlockSpec((1, 1, PAGE_SIZE, KV_HEADS, HEAD_DIM), lambda s, pt, p=p: (layer, pt[s, p], 0, 0, 0))

    def prep(c):
        pe, w1, w2 = c
        pe_t = jnp.tile(pe, (PAGE_SIZE // CMP_BLOCK, 1)).astype(F32)
        return pe_t, w1.reshape(CMP_BLOCK // 2, 2 * HEAD_DIM, HEAD_DIM).astype(MM), w2.astype(MM)

    const2 = lambda s, pt: (0, 0)
    const3 = lambda s, pt: (0, 0, 0)
    w_specs = [pl.BlockSpec((PAGE_SIZE, HEAD_DIM), const2),
               pl.BlockSpec((CMP_BLOCK // 2, 2 * HEAD_DIM, HEAD_DIM), const3),
               pl.BlockSpec((HEAD_DIM, HEAD_DIM), const2)]
    out_spec = pl.BlockSpec((1, KV_HEADS, blocks, HEAD_DIM), lambda s, pt: (s, 0, 0, 0))
    out_shape = jax.ShapeDtypeStruct((n_seq, KV_HEADS, blocks, HEAD_DIM), F32)
    return pl.pallas_call(
        functools.partial(_compress_kernel, n_pages),
        grid_spec=pltpu.PrefetchScalarGridSpec(
            num_scalar_prefetch=1,
            grid=(n_seq,),
            in_specs=[page_spec(p) for p in range(n_pages)] * 2 + w_specs * 2,
            out_specs=[out_spec, out_spec],
            scratch_shapes=[pltpu.VMEM((KV_HEADS, n_pages * PAGE_SIZE, HEAD_DIM), F32)]),
        out_shape=[out_shape, out_shape],
        compiler_params=_cparams(("arbitrary",)),
        name="compress",
    )(page_table, *([pool_k] * n_pages), *([pool_v] * n_pages), *prep(cmp_k), *prep(cmp_v))


def _select_rows(imp, qpos, n_blocks):
    j = lax.broadcasted_iota(jnp.int32, (n_blocks, 1), 0)
    forced = (j == 0) | (j == (qpos >> _SLC_SHIFT))
    future = j * SLC_BLOCK > qpos
    score = jnp.where(future, -1.0, imp + jnp.where(forced, FORCE_BONUS, 0.0))
    rank = jnp.zeros(score.shape, jnp.int32)
    for i in range(n_blocks):
        ri = score[i:i + 1, :]
        beats = (ri > score) | ((ri == score) & (i < j))
        rank = rank + beats.astype(jnp.int32)
    return rank < min(N_SELECT, n_blocks)


def _prompt_attn_kernel(q_ref, kc_ref, vc_ref, ks_ref, vs_ref, kw_ref, vw_ref, gbr_ref, o_ref,
                        vst_scr, vwt_scr, bias_scr, acc_scr):
    n = pl.program_id(2)
    seq = ks_ref.shape[0]
    n_tiles = seq // Q_BLOCK
    n_sel_blocks = seq // SLC_BLOCK
    n_cmp = kc_ref.shape[2]
    lane_groups = [slice(g * Q_BLOCK, (g + 1) * Q_BLOCK) for g in range(Q_PER_KV)]

    tiles_per_chunk = SEL_CHUNK // Q_BLOCK

    @pl.when(n == 0)
    def _():
        for kt in range(n_tiles):
            rows = slice(kt * Q_BLOCK, (kt + 1) * Q_BLOCK)
            c, j = divmod(kt, tiles_per_chunk)
            vst_scr[c, :, j * Q_BLOCK:(j + 1) * Q_BLOCK] = vs_ref[rows, :].astype(F32).T.astype(MM)
            vwt_scr[kt] = vw_ref[rows, :].astype(F32).T.astype(MM)

    q = q_ref[...] * (SCALE * LOG2E)
    q_t = jnp.concatenate([q[:, g * HEAD_DIM:(g + 1) * HEAD_DIM].T for g in range(Q_PER_KV)],
                          axis=1).astype(MM)
    qpos = n * Q_BLOCK + lax.broadcasted_iota(jnp.int32, (1, Q_BLOCK), 1)

    s_c = jnp.dot(kc_ref[0, 0].astype(MM), q_t, preferred_element_type=F32)
    r = lax.broadcasted_iota(jnp.int32, (n_cmp, 1), 0)
    blk = jnp.where(r < n_cmp // 2, 2 * r, 2 * (r - n_cmp // 2) + 1)
    ok = ((blk + 1) * CMP_BLOCK - 1) <= qpos
    imp = jnp.zeros((n_cmp, Q_BLOCK), F32)
    probs = []
    for sl in lane_groups:
        s = jnp.where(ok, s_c[:, sl], NEG)
        e = jnp.exp2(s - jnp.max(s, axis=0, keepdims=True))
        p = jnp.where(ok, e / jnp.sum(e, axis=0, keepdims=True), 0.0)
        imp = imp + p
        probs.append(p.astype(MM))
    oc_t = jnp.dot(vc_ref[0, 0].T.astype(MM), jnp.concatenate(probs, axis=1), preferred_element_type=F32)

    sel = _select_rows(imp[:n_cmp // 2] + imp[n_cmp // 2:], qpos, n_sel_blocks)
    sel_bias = jnp.where(sel, 0.0, NEG)
    blocks_per_chunk = SEL_CHUNK // SLC_BLOCK
    key_row = lax.broadcasted_iota(jnp.int32, (SEL_CHUNK, 1), 0)
    n_chunks = lax.div(n, tiles_per_chunk) + 1
    for c in range(seq // SEL_CHUNK):
        @pl.when(c < n_chunks)
        def _(c=c):
            tile = jnp.concatenate(
                [jnp.broadcast_to(sel_bias[c * blocks_per_chunk + i:c * blocks_per_chunk + i + 1, :],
                                  (SLC_BLOCK, Q_BLOCK)) for i in range(blocks_per_chunk)], axis=0)
            bias_scr[c] = jnp.where(c * SEL_CHUNK + key_row <= qpos, tile, NEG)

    acc_scr[...] = jnp.zeros_like(acc_scr)

    def body(c, carry):
        ms, ls = carry
        start = pl.multiple_of(c * SEL_CHUNK, SEL_CHUNK)
        s_t = jnp.dot(ks_ref[pl.ds(start, SEL_CHUNK), :], q_t, preferred_element_type=F32)
        bias = bias_scr[c]
        new_ms, new_ls, ps, alphas = [], [], [], []
        for g, sl in enumerate(lane_groups):
            s = s_t[:, sl] + bias
            m_new = jnp.maximum(ms[g], jnp.max(s, axis=0, keepdims=True))
            alpha = jnp.exp2(ms[g] - m_new)
            p = jnp.exp2(s - m_new)
            new_ls.append(alpha * ls[g] + jnp.sum(p, axis=0, keepdims=True))
            new_ms.append(m_new)
            alphas.append(alpha)
            ps.append(p.astype(MM))
        pv = jnp.dot(vst_scr[c], jnp.concatenate(ps, axis=1), preferred_element_type=F32)
        for g, sl in enumerate(lane_groups):
            acc_scr[:, sl] = alphas[g] * acc_scr[:, sl] + pv[:, sl]
        return tuple(new_ms), tuple(new_ls)

    init = (tuple(jnp.full((1, Q_BLOCK), NEG, F32) for _ in lane_groups),
            tuple(jnp.zeros((1, Q_BLOCK), F32) for _ in lane_groups))
    _, ls_s = lax.fori_loop(0, n_chunks, body, init)

    win_keys = WINDOW + Q_BLOCK
    w_tile0 = jnp.maximum(n - WINDOW // Q_BLOCK, 0)
    w_start = pl.multiple_of(w_tile0 * Q_BLOCK, Q_BLOCK)
    s_w = jnp.dot(kw_ref[pl.ds(w_start, win_keys), :], q_t, preferred_element_type=F32)
    dist = qpos - (w_start + lax.broadcasted_iota(jnp.int32, (win_keys, 1), 0))
    w_bias = jnp.where((dist >= 0) & (dist < WINDOW), 0.0, NEG)
    ls_w, ps = [], []
    for sl in lane_groups:
        s = s_w[:, sl] + w_bias
        p = jnp.exp2(s - jnp.max(s, axis=0, keepdims=True))
        ls_w.append(jnp.sum(p, axis=0, keepdims=True))
        ps.append(p.astype(MM))
    p_w = jnp.concatenate(ps, axis=1)
    ow_t = functools.reduce(
        lambda a, b: a + b,
        [jnp.dot(vwt_scr[w_tile0 + j], p_w[j * Q_BLOCK:(j + 1) * Q_BLOCK, :], preferred_element_type=F32)
         for j in range(win_keys // Q_BLOCK)])

    gate_t = jax.nn.sigmoid(gbr_ref[...]).T
    for g, sl in enumerate(lane_groups):
        o_t = (oc_t[:, sl] * gate_t[3 * g:3 * g + 1, :]
               + (acc_scr[:, sl] / ls_s[g]) * gate_t[3 * g + 1:3 * g + 2, :]
               + (ow_t[:, sl] / ls_w[g]) * gate_t[3 * g + 2:3 * g + 3, :])
        o_ref[:, g * HEAD_DIM:(g + 1) * HEAD_DIM] = o_t.T.astype(o_ref.dtype)


def _prompt_attention(a, kc, vc, ks, vs, kw, vw, gt, batch, seq):
    n_qb = seq // Q_BLOCK
    q_col0 = 2 * A_WIDTH // (Q_PER_KV * HEAD_DIM)
    gbr_col0 = 2 * D_MODEL // LANES
    cmp_spec = pl.BlockSpec((1, 1, kc.shape[2], HEAD_DIM), lambda b, h, n: (b, h, 0, 0))
    kv_spec = pl.BlockSpec((seq, HEAD_DIM), lambda b, h, n: (b, h))
    assert seq % SEL_CHUNK == 0 and seq >= WINDOW + Q_BLOCK
    scratch = [pltpu.VMEM((seq // SEL_CHUNK, HEAD_DIM, SEL_CHUNK), MM),
               pltpu.VMEM((n_qb, HEAD_DIM, Q_BLOCK), MM),
               pltpu.VMEM((seq // SEL_CHUNK, SEL_CHUNK, Q_BLOCK), F32),
               pltpu.VMEM((HEAD_DIM, Q_PER_KV * Q_BLOCK), F32)]
    return pl.pallas_call(
        _prompt_attn_kernel,
        grid=(batch, KV_HEADS, n_qb),
        in_specs=[pl.BlockSpec((Q_BLOCK, Q_PER_KV * HEAD_DIM), lambda b, h, n: (b * n_qb + n, q_col0 + h)),
                  cmp_spec, cmp_spec, kv_spec, kv_spec, kv_spec, kv_spec,
                  pl.BlockSpec((Q_BLOCK, LANES), lambda b, h, n: (b * n_qb + n, gbr_col0 + h))],
        out_specs=pl.BlockSpec((Q_BLOCK, Q_PER_KV * HEAD_DIM), lambda b, h, n: (b * n_qb + n, h)),
        out_shape=jax.ShapeDtypeStruct((batch * seq, Q_COLS), MM),
        scratch_shapes=scratch,
        compiler_params=_cparams(("parallel", "parallel", "arbitrary")),
        name="prompt_attention",
    )(a, kc, vc, ks, vs, kw, vw, gt)


def _dot_nt(a, b):
    return lax.dot_general(a, b, (((1,), (1,)), ((), ())), preferred_element_type=F32)


def _cmp_branch(qb, qpos, kc, vc):
    half = kc.shape[0] // 2
    blk = lax.broadcasted_iota(jnp.int32, (1, half), 1)
    s_e = _dot_nt(qb, kc[:half].astype(MM)) * SCALE
    s_o = _dot_nt(qb, kc[half:].astype(MM)) * SCALE
    ok_e = ((2 * blk + 1) * CMP_BLOCK - 1) <= qpos
    ok_o = ((2 * blk + 2) * CMP_BLOCK - 1) <= qpos
    s_e = jnp.where(ok_e, s_e, NEG)
    s_o = jnp.where(ok_o, s_o, NEG)
    m = jnp.maximum(jnp.max(s_e, axis=1, keepdims=True), jnp.max(s_o, axis=1, keepdims=True))
    e_e = jnp.exp(s_e - m)
    e_o = jnp.exp(s_o - m)
    den = jnp.sum(e_e, axis=1, keepdims=True) + jnp.sum(e_o, axis=1, keepdims=True)
    p_e = jnp.where(ok_e, e_e / den, 0.0)
    p_o = jnp.where(ok_o, e_o / den, 0.0)
    o_c = (jnp.dot(p_e.astype(MM), vc[:half].astype(MM), preferred_element_type=F32)
           + jnp.dot(p_o.astype(MM), vc[half:].astype(MM), preferred_element_type=F32))
    return o_c, p_e, p_o


def _select_cols(imp, qpos, n_blocks):
    j = lax.broadcasted_iota(jnp.int32, (1, n_blocks), 1)
    forced = (j == 0) | (j == (qpos >> _SLC_SHIFT))
    future = j * SLC_BLOCK > qpos
    score = jnp.where(future, -1.0, imp + jnp.where(forced, FORCE_BONUS, 0.0))
    rank = jnp.zeros(score.shape, jnp.int32)
    for i in range(n_blocks):
        ci = score[:, i:i + 1]
        beats = (ci > score) | ((ci == score) & (i < j))
        rank = rank + beats.astype(jnp.int32)
    return (rank < min(N_SELECT, n_blocks)).astype(F32)


def _softmax_parts(parts):
    m = functools.reduce(jnp.maximum, [jnp.max(s, axis=1, keepdims=True) for s, _ in parts])
    es = [jnp.exp(s - m) for s, _ in parts]
    den = functools.reduce(lambda a, b: a + b, [jnp.sum(e, axis=1, keepdims=True) for e in es])
    acc = None
    for e, (_, v) in zip(es, parts):
        o = jnp.dot((e / den).astype(MM), v, preferred_element_type=F32)
        acc = o if acc is None else acc + o
    return acc


def _sample_attn_kernel(n_pages, pt_ref, *refs):
    ks_pages, vs_pages = refs[:n_pages], refs[n_pages:2 * n_pages]
    (q_ref, kc_ref, vc_ref, kbuf_ref, vbuf_ref, ksn_ref, vsn_ref, kwn_ref, vwn_ref, kwr_ref, vwr_ref, gbr_ref,
     o_ref, kwo_ref, vwo_ref) = refs[2 * n_pages:]
    t = q_ref.shape[1]
    past = n_pages * PAGE_SIZE
    nb_past = past // SLC_BLOCK
    wb = kbuf_ref.shape[2]
    tpos = past + lax.broadcasted_iota(jnp.int32, (t, 1), 0)
    qpos = jnp.concatenate([tpos] * Q_PER_KV, axis=0)
    new_idx = lax.broadcasted_iota(jnp.int32, (1, NEW_ROWS), 1)
    new_pos = past + new_idx
    new_ok = (new_pos <= qpos) & (new_idx < t)
    q = q_ref[0]
    gate = jax.nn.sigmoid(gbr_ref[0])

    def new_rows(ref, hs):
        r = ref[0][:, hs].astype(F32)
        return jnp.concatenate([r, jnp.zeros((NEW_ROWS - t, HEAD_DIM), F32)], axis=0).astype(MM)

    kwo_ref[0, 0, :wb - t] = kbuf_ref[0, 0, t:]
    kwo_ref[0, 0, wb - t:] = kwr_ref[0]
    vwo_ref[0, 0, :wb - t] = vbuf_ref[0, 0, t:]
    vwo_ref[0, 0, wb - t:] = vwr_ref[0]

    buf_pos = past - wb + lax.broadcasted_iota(jnp.int32, (1, wb), 1)
    d_buf = qpos - buf_pos
    buf_ok = (d_buf >= 0) & (d_buf < WINDOW) & (buf_pos >= 0)
    d_new = qpos - new_pos
    win_new_ok = (d_new >= 0) & (d_new < WINDOW) & (new_idx < t)

    for h in range(KV_HEADS):
        hs = slice(h * HEAD_DIM, (h + 1) * HEAD_DIM)
        qb = jnp.concatenate([q[:, (h * Q_PER_KV + g) * HEAD_DIM:(h * Q_PER_KV + g + 1) * HEAD_DIM]
                              for g in range(Q_PER_KV)], axis=0).astype(MM)
        o_c, p_e, p_o = _cmp_branch(qb, qpos, kc_ref[0, h], vc_ref[0, h])
        imp_e = functools.reduce(lambda a, b: a + b, [p_e[g * t:(g + 1) * t] for g in range(Q_PER_KV)])
        imp_o = functools.reduce(lambda a, b: a + b, [p_o[g * t:(g + 1) * t] for g in range(Q_PER_KV)])
        imp = jnp.concatenate([imp_e + imp_o, jnp.zeros((t, 1), F32)], axis=1)
        sel = _select_cols(imp, tpos, nb_past + 1)
        sel = jnp.concatenate([sel] * Q_PER_KV, axis=0)

        ksn = new_rows(ksn_ref, hs)
        vsn = new_rows(vsn_ref, hs)
        parts = []
        for p in range(n_pages):
            k = ks_pages[p][0, 0, :, h, :].astype(MM)
            v = vs_pages[p][0, 0, :, h, :].astype(MM)
            s = _dot_nt(qb, k) * SCALE
            per_page = PAGE_SIZE // SLC_BLOCK
            lane_blk = lax.broadcasted_iota(jnp.int32, (1, PAGE_SIZE), 1) >> _SLC_SHIFT
            ok = functools.reduce(
                lambda a, b: a | b,
                [(lane_blk == c) & (sel[:, p * per_page + c:p * per_page + c + 1] > 0.5)
                 for c in range(per_page)])
            parts.append((jnp.where(ok, s, NEG), v))
        s_new = _dot_nt(qb, ksn) * SCALE
        parts.append((jnp.where(new_ok & (sel[:, nb_past:nb_past + 1] > 0.5), s_new, NEG), vsn))
        o_s = _softmax_parts(parts)

        kb = kbuf_ref[0, 0, :, h, :].astype(MM)
        vb = vbuf_ref[0, 0, :, h, :].astype(MM)
        kwn = new_rows(kwn_ref, hs)
        vwn = new_rows(vwn_ref, hs)
        s_buf = jnp.where(buf_ok, _dot_nt(qb, kb) * SCALE, NEG)
        s_nw = jnp.where(win_new_ok, _dot_nt(qb, kwn) * SCALE, NEG)
        o_w = _softmax_parts([(s_buf, vb), (s_nw, vwn)])

        for g in range(Q_PER_KV):
            rs = slice(g * t, (g + 1) * t)
            c0 = h * LANES + 3 * g
            o = (o_c[rs] * gate[:, c0:c0 + 1] + o_s[rs] * gate[:, c0 + 1:c0 + 2]
                 + o_w[rs] * gate[:, c0 + 2:c0 + 3])
            col = (h * Q_PER_KV + g) * HEAD_DIM
            o_ref[0, :, col:col + HEAD_DIM] = o.astype(o_ref.dtype)


def _sample_attention(a3, kc, vc, pool_ks, pool_vs, layer, page_table, kbuf, vbuf,
                      ksn, vsn, kwn, vwn, kw_rows, vw_rows, gt3):
    db, t, _ = a3.shape
    n_pages = page_table.shape[1]
    wb = kbuf.shape[2]

    def page_spec(p):
        return pl.BlockSpec((1, 1, PAGE_SIZE, KV_HEADS, HEAD_DIM), lambda s, pt, p=p: (layer, pt[s, p], 0, 0, 0))

    row3 = lambda s, pt: (s, 0, 0)
    cmp_spec = pl.BlockSpec((1, KV_HEADS, kc.shape[2], HEAD_DIM), lambda s, pt: (s, 0, 0, 0))
    buf_spec = pl.BlockSpec((1, 1, wb, KV_HEADS, HEAD_DIM), lambda s, pt: (layer, s, 0, 0, 0))
    out_buf_spec = pl.BlockSpec((1, 1, wb, KV_HEADS, HEAD_DIM), lambda s, pt: (0, s, 0, 0, 0))
    new_spec = pl.BlockSpec((1, t, KV_COLS), row3)
    rows_spec = pl.BlockSpec((1, t, KV_HEADS, HEAD_DIM), lambda s, pt: (s, 0, 0, 0))
    buf_shape = jax.ShapeDtypeStruct((1, db, wb, KV_HEADS, HEAD_DIM), F32)
    return pl.pallas_call(
        functools.partial(_sample_attn_kernel, n_pages),
        grid_spec=pltpu.PrefetchScalarGridSpec(
            num_scalar_prefetch=1,
            grid=(db,),
            in_specs=[page_spec(p) for p in range(n_pages)] * 2
                     + [pl.BlockSpec((1, t, Q_COLS), lambda s, pt: (s, 0, 2 * A_WIDTH // Q_COLS)),
                        cmp_spec, cmp_spec, buf_spec, buf_spec, new_spec, new_spec, new_spec, new_spec,
                        rows_spec, rows_spec,
                        pl.BlockSpec((1, t, KV_HEADS * LANES),
                                     lambda s, pt: (s, 0, 2 * D_MODEL // (KV_HEADS * LANES)))],
            out_specs=[pl.BlockSpec((1, t, Q_COLS), row3), out_buf_spec, out_buf_spec]),
        out_shape=[jax.ShapeDtypeStruct((db, t, Q_COLS), MM), buf_shape, buf_shape],
        compiler_params=_cparams(("arbitrary",)),
        name="sample_attention",
    )(page_table, *([pool_ks] * n_pages), *([pool_vs] * n_pages),
      a3, kc, vc, kbuf, vbuf, ksn, vsn, kwn, vwn, kw_rows, vw_rows, gt3)


def _mix_kernel(a_ref, o_ref, wpa_ref, wpb_ref, ga_ref, gb_ref, m_ref):
    y_a = jnp.dot(a_ref[...], wpa_ref[...], preferred_element_type=F32)
    y_b = jnp.dot(o_ref[...], wpb_ref[...], preferred_element_type=F32)
    m = jax.nn.sigmoid(ga_ref[...]) * y_a + jax.nn.sigmoid(gb_ref[...]) * y_b
    m_ref[...] = m.astype(m_ref.dtype)


def _mix(a_out, o, w_pa, w_pb, gt, tm):
    n = a_out.shape[0]
    gb0 = D_MODEL // TN
    return pl.pallas_call(
        _mix_kernel,
        grid=(n // tm, D_MODEL // TN),
        in_specs=[pl.BlockSpec((tm, A_WIDTH), lambda i, j: (i, 0)),
                  pl.BlockSpec((tm, Q_COLS), lambda i, j: (i, 0)),
                  pl.BlockSpec((A_WIDTH, TN), lambda i, j: (0, j)),
                  pl.BlockSpec((Q_COLS, TN), lambda i, j: (0, j)),
                  pl.BlockSpec((tm, TN), lambda i, j: (i, j)),
                  pl.BlockSpec((tm, TN), lambda i, j: (i, gb0 + j))],
        out_specs=pl.BlockSpec((tm, TN), lambda i, j: (i, j)),
        out_shape=jax.ShapeDtypeStruct((n, D_MODEL), MM),
        compiler_params=_cparams(("parallel", "arbitrary")),
        name="merge_gate",
    )(a_out, o, w_pa, w_pb, gt, gt)


def _out_proj_kernel(m_ref, w_ref, x_ref, o_ref):
    o_ref[...] = x_ref[...] + jnp.dot(m_ref[...], w_ref[...], preferred_element_type=F32)


def _out_proj(m, w_o, x2d, tm):
    n = m.shape[0]
    return pl.pallas_call(
        _out_proj_kernel,
        grid=(n // tm, D_MODEL // TN),
        in_specs=[pl.BlockSpec((tm, D_MODEL), lambda i, j: (i, 0)),
                  pl.BlockSpec((D_MODEL, TN), lambda i, j: (0, j)),
                  pl.BlockSpec((tm, TN), lambda i, j: (i, j))],
        out_specs=pl.BlockSpec((tm, TN), lambda i, j: (i, j)),
        out_shape=jax.ShapeDtypeStruct((n, D_MODEL), F32),
        compiler_params=_cparams(("parallel", "arbitrary")),
        name="out_proj",
    )(m, w_o, x2d)


def _route(logits):
    lane = lax.broadcasted_iota(jnp.int32, logits.shape, 1)
    big = jnp.int32(LANES)
    is_g = lane < N_GROUPS
    gl = jnp.where(is_g, logits, -jnp.inf)
    gm = jnp.max(gl, axis=1, keepdims=True)
    ge = jnp.exp(gl - gm)
    pg = ge / jnp.sum(ge, axis=1, keepdims=True)
    pg_top = jnp.max(pg, axis=1, keepdims=True)
    g_idx = jnp.min(jnp.where(is_g & (pg == pg_top), lane, big), axis=1, keepdims=True)
    in_grp = (lane >= N_GROUPS) & (((lane - N_GROUPS) >> 3) == g_idx)
    el = jnp.where(in_grp, logits, -jnp.inf)
    em = jnp.max(el, axis=1, keepdims=True)
    ee = jnp.exp(el - em)
    pe = ee / jnp.sum(ee, axis=1, keepdims=True)
    p1 = jnp.max(pe, axis=1, keepdims=True)
    i1 = jnp.min(jnp.where(in_grp & (pe == p1), lane, big), axis=1, keepdims=True)
    rest = in_grp & (lane != i1)
    p2 = jnp.max(jnp.where(rest, pe, -1.0), axis=1, keepdims=True)
    i2 = jnp.min(jnp.where(rest & (pe == p2), lane, big), axis=1, keepdims=True)
    tot = p1 + p2
    return jnp.where(lane == i1, pg_top * p1 / tot, 0.0) + jnp.where(lane == i2, pg_top * p2 / tot, 0.0)


def _moe_kernel(final_norm, x_ref, gf_ref, wr_hi_ref, wr_lo_ref, br_ref, wg_ref, wu_ref, wd_ref, gfin_ref,
                y_ref, h_scr, gate_scr, acc_scr):
    e = pl.program_id(1)

    @pl.when(e == 0)
    def _():
        x = x_ref[...]
        ms = jnp.mean(x * x, axis=-1, keepdims=True)
        h = x * lax.rsqrt(ms + EPS) * gf_ref[...]
        hi = h.astype(MM)
        lo = (h - hi.astype(F32)).astype(MM)
        logits = (jnp.dot(hi, wr_hi_ref[...], preferred_element_type=F32)
                  + (jnp.dot(hi, wr_lo_ref[...], preferred_element_type=F32)
                     + jnp.dot(lo, wr_hi_ref[...], preferred_element_type=F32))) + br_ref[...]
        h_scr[...] = hi
        gate_scr[...] = _route(logits)
        acc_scr[...] = jnp.zeros_like(acc_scr)

    h = h_scr[...]
    lane = lax.broadcasted_iota(jnp.int32, gate_scr.shape, 1)
    gcol = jnp.sum(jnp.where(lane == e + N_GROUPS, gate_scr[...], 0.0), axis=1, keepdims=True)
    hid = (jax.nn.silu(jnp.dot(h, wg_ref[0], preferred_element_type=F32))
           * jnp.dot(h, wu_ref[0], preferred_element_type=F32))
    acc_scr[...] += jnp.dot((hid * gcol).astype(MM), wd_ref[0], preferred_element_type=F32)

    @pl.when(e == pl.num_programs(1) - 1)
    def _():
        x2 = x_ref[...] + acc_scr[...]
        if final_norm:
            ms = jnp.mean(x2 * x2, axis=-1, keepdims=True)
            x2 = x2 * lax.rsqrt(ms + EPS) * gfin_ref[...]
        y_ref[...] = x2


def _moe(x1, g_ffn, wr_hi, wr_lo, b_r, w_gate, w_up, w_down, g_final, tm, final_norm):
    n = x1.shape[0]
    row = lambda i, e: (i, 0)
    const = lambda i, e: (0, 0)
    return pl.pallas_call(
        functools.partial(_moe_kernel, final_norm),
        grid=(n // tm, N_EXPERTS),
        in_specs=[pl.BlockSpec((tm, D_MODEL), row),
                  pl.BlockSpec((1, D_MODEL), const),
                  pl.BlockSpec((D_MODEL, LANES), const),
                  pl.BlockSpec((D_MODEL, LANES), const),
                  pl.BlockSpec((1, LANES), const),
                  pl.BlockSpec((1, D_MODEL, EXPERT_DIM), lambda i, e: (e, 0, 0)),
                  pl.BlockSpec((1, D_MODEL, EXPERT_DIM), lambda i, e: (e, 0, 0)),
                  pl.BlockSpec((1, EXPERT_DIM, D_MODEL), lambda i, e: (e, 0, 0)),
                  pl.BlockSpec((1, D_MODEL), const)],
        out_specs=pl.BlockSpec((tm, D_MODEL), row),
        out_shape=jax.ShapeDtypeStruct((n, D_MODEL), F32),
        scratch_shapes=[pltpu.VMEM((tm, D_MODEL), MM), pltpu.VMEM((tm, LANES), F32),
                        pltpu.VMEM((tm, D_MODEL), F32)],
        compiler_params=_cparams(("parallel", "arbitrary")),
        name="moe_final_norm",
    )(x1, g_ffn, wr_hi, wr_lo, b_r, w_gate, w_up, w_down, g_final)


def _router_weights(w_rg, b_rg, w_re, b_re):
    w = jnp.concatenate([w_rg, w_re], axis=1)
    w = jnp.pad(w, ((0, 0), (0, LANES - w.shape[1])))
    b = jnp.pad(jnp.concatenate([b_rg, b_re]), (0, LANES - N_GROUPS - N_EXPERTS))[None, :]
    hi = w.astype(MM)
    lo = (w - hi.astype(F32)).astype(MM)
    return hi, lo, b.astype(F32)


def _row_tile(n):
    return 512 if n % 512 == 0 else n


def kernel(x_prompt, x_sample, cache_k_cmp, cache_v_cmp, cache_k_slc, cache_v_slc, state_k_win, state_v_win, page_table, g_mix, w_in, a_ln_g, a_ln_b, a_w_s, a_b_s, cmp_pe_k, cmp_w1_k, cmp_w2_k, cmp_pe_v, cmp_w1_v, cmp_w2_v, w_pa, w_pb, w_o, g_ffn, w_rg, b_rg, w_re, b_re, w_gate, w_up, w_down, g_final):
    depth = g_mix.shape[0]
    b, s, d = x_prompt.shape
    db, t, _ = x_sample.shape
    n_pages = page_table.shape[1]
    past = n_pages * PAGE_SIZE
    wl = min(WINDOW, s)
    assert s % Q_BLOCK == 0 and CHUNK % t == 0 and (db * t) % CHUNK == 0 and t < CMP_BLOCK

    tabs_p = _rope_tables(jnp.arange(s))
    tabs_s = _rope_tables(past + (jnp.arange(db * t) % t))
    prompt_pages = jnp.arange(b * s // PAGE_SIZE, dtype=jnp.int32).reshape(b, s // PAGE_SIZE)
    g_final2 = g_final[None, :]

    xp = x_prompt.reshape(b * s, d)
    xs = x_sample.reshape(db * t, d)
    outs_p = [[] for _ in range(6)]
    outs_s = [[] for _ in range(7)]
    for l in range(depth):
        w_a, w_kv = _layout_w_in(w_in[l])
        g_l = g_mix[l][None, :]
        ln_g, ln_b = a_ln_g[l][None, :], a_ln_b[l][None, :]
        cmp_k = (cmp_pe_k[l], cmp_w1_k[l], cmp_w2_k[l])
        cmp_v = (cmp_pe_v[l], cmp_w1_v[l], cmp_w2_v[l])
        wpa, wpb, wo = w_pa[l].astype(MM), w_pb[l].astype(MM), w_o[l].astype(MM)
        wr_hi, wr_lo, b_r = _router_weights(w_rg[l], b_rg[l], w_re[l], b_re[l])
        wg, wu, wd = w_gate[l].astype(MM), w_up[l].astype(MM), w_down[l].astype(MM)
        last = l == depth - 1

        def tail(x2d, a_out, o, gt):
            tm = _row_tile(x2d.shape[0])
            m = _mix(a_out, o, wpa, wpb, gt, tm)
            x1 = _out_proj(m, wo, x2d, tm)
            return _moe(x1, g_ffn[l][None, :], wr_hi, wr_lo, b_r, wg, wu, wd, g_final2, tm, last)

        tm = _row_tile(b * s)
        a, gt, h = _project_a(xp, g_l, w_a, tabs_p, tm)
        kc_r, vc_r, ks_r, vs_r, kw_r, vw_r, ks_b, vs_b, kw_b, vw_b = _project_kv(h, w_kv, tabs_p, tm)
        mix_p, bias_p = _gmlp_mix(a_w_s[l], a_b_s[l], s)
        (a_out,) = _gmlp(a, ln_g, ln_b, mix_p, bias_p, False)
        pages = lambda r: r.reshape(1, b * s // PAGE_SIZE, PAGE_SIZE, KV_HEADS, HEAD_DIM)
        kc, vc = _compress(pages(kc_r), pages(vc_r), 0, prompt_pages, cmp_k, cmp_v)
        o = _prompt_attention(a, kc, vc, ks_b, vs_b, kw_b, vw_b, gt, b, s)
        xp = tail(xp, a_out, o, gt)
        heads = lambda r: r.reshape(b, s, KV_HEADS, HEAD_DIM)
        for lst, r in zip(outs_p, (kc_r, vc_r, ks_r, vs_r)):
            lst.append(heads(r))
        outs_p[4].append(heads(kw_r)[:, -wl:])
        outs_p[5].append(heads(vw_r)[:, -wl:])

        tm = _row_tile(db * t)
        a, gt, h = _project_a(xs, g_l, w_a, tabs_s, tm)
        kc_r, vc_r, ks_r, vs_r, kw_r, vw_r, ks_b, vs_b, kw_b, vw_b = _project_kv(h, w_kv, tabs_s, tm)
        mix_s, bias_s = _gmlp_mix(a_w_s[l], a_b_s[l], t)
        a_out, v_new = _gmlp(a, ln_g, ln_b, mix_s, bias_s, True)
        kc, vc = _compress(cache_k_cmp, cache_v_cmp, l, page_table, cmp_k, cmp_v)
        r3 = lambda r: r.reshape(db, t, r.shape[-1])
        r4 = lambda r: r.reshape(db, t, KV_HEADS, HEAD_DIM)
        o, kwin, vwin = _sample_attention(
            r3(a), kc, vc, cache_k_slc, cache_v_slc, l, page_table, state_k_win, state_v_win,
            r3(ks_b), r3(vs_b), r3(kw_b), r3(vw_b), r4(kw_r), r4(vw_r), r3(gt))
        xs = tail(xs, a_out, o.reshape(db * t, Q_COLS), gt)
        for lst, r in zip(outs_s, (kc_r, vc_r, ks_r, vs_r)):
            lst.append(r4(r))
        outs_s[4].append(kwin[0])
        outs_s[5].append(vwin[0])
        outs_s[6].append(v_new.reshape(db, t, A_WIDTH))

    y_prompt = xp.reshape(b, s, d)
    y_sample = xs.reshape(db, t, d)
    return (y_prompt, y_sample, *[jnp.stack(o) for o in outs_p], *[jnp.stack(o) for o in outs_s])
```

```python
import functools
import math

import jax
import jax.numpy as jnp
from jax import lax
from jax.experimental import pallas as pl
from jax.experimental.pallas import tpu as pltpu

F32 = jnp.float32
MM = jnp.bfloat16

D_MODEL = 2048
A_WIDTH = 2048
A_GROUPS = 8
A_GROUP_DIM = A_WIDTH // A_GROUPS
CHUNK = 128
N_HEADS = 16
KV_HEADS = 4
HEAD_DIM = 128
Q_PER_KV = N_HEADS // KV_HEADS
ROPE_DIM = HEAD_DIM // 4
ROPE_THETA = 500000.0
CMP_BLOCK = 32
SLC_BLOCK = 64
_SLC_SHIFT = 6
N_SELECT = 8
WINDOW = 512
Q_BLOCK = 128
PAGE_SIZE = 128
SCALE = HEAD_DIM ** -0.5
LOG2E = math.log2(math.e)
N_GROUPS = 4
EXPERTS_PER_GROUP = 8
N_EXPERTS = N_GROUPS * EXPERTS_PER_GROUP
EXPERT_DIM = 256
EPS = 1e-6
NEG = -1e30
FORCE_BONUS = 1e4
Q_COLS = N_HEADS * HEAD_DIM
KV_COLS = KV_HEADS * HEAD_DIM

LANES = 128
TN = 512
SEL_CHUNK = 512
CMP_PITCH = CMP_BLOCK * KV_HEADS + 8
VMEM_LIMIT = 56 * 1024 * 1024

_A_TILES = 3 * D_MODEL // TN
_G_TILES = 2 * D_MODEL // TN + 1
_KV_TILES = 6


def _cparams(sem):
    return pltpu.CompilerParams(dimension_semantics=sem, vmem_limit_bytes=VMEM_LIMIT)


def _rope_tile(z, cos, s1, s2):
    outs = []
    for h in range(z.shape[1] // HEAD_DIM):
        zh = z[:, h * HEAD_DIM:(h + 1) * HEAD_DIM]
        outs.append(zh * cos + pltpu.roll(zh, ROPE_DIM // 2, 1) * s1
                    + pltpu.roll(zh, HEAD_DIM - ROPE_DIM // 2, 1) * s2)
    return jnp.concatenate(outs, axis=1)


def _proj_a_kernel(x_ref, g_ref, w_ref, cos_ref, s1_ref, s2_ref, a_ref, gt_ref, h_ref):
    j = pl.program_id(1)

    @pl.when(j == 0)
    def _():
        x = x_ref[...]
        ms = jnp.mean(x * x, axis=-1, keepdims=True)
        h_ref[...] = (x * lax.rsqrt(ms + EPS) * g_ref[...]).astype(MM)

    z = jnp.dot(h_ref[...], w_ref[...], preferred_element_type=F32)

    @pl.when(j < 2 * D_MODEL // TN)
    def _():
        a_ref[...] = z

    @pl.when((j >= 2 * D_MODEL // TN) & (j < _A_TILES))
    def _():
        a_ref[...] = _rope_tile(z, cos_ref[...], s1_ref[...], s2_ref[...])

    @pl.when(j >= _A_TILES)
    def _():
        gt_ref[...] = z


def _project_a(x2d, g, w, tabs, tm):
    n, d = x2d.shape
    cos, s1, s2 = tabs
    tab_blocks = cos.shape[0] // tm
    row = lambda i, j: (i, 0)
    tab = lambda i, j: (i % tab_blocks, 0)
    return pl.pallas_call(
        _proj_a_kernel,
        grid=(n // tm, _A_TILES + _G_TILES),
        in_specs=[pl.BlockSpec((tm, d), row),
                  pl.BlockSpec((1, d), lambda i, j: (0, 0)),
                  pl.BlockSpec((d, TN), lambda i, j: (0, j)),
                  pl.BlockSpec((tm, LANES), tab), pl.BlockSpec((tm, LANES), tab),
                  pl.BlockSpec((tm, LANES), tab)],
        out_specs=[pl.BlockSpec((tm, TN), lambda i, j: (i, jnp.minimum(j, _A_TILES - 1))),
                   pl.BlockSpec((tm, TN), lambda i, j: (i, jnp.maximum(j - _A_TILES, 0))),
                   pl.BlockSpec((tm, d), row)],
        out_shape=[jax.ShapeDtypeStruct((n, _A_TILES * TN), F32),
                   jax.ShapeDtypeStruct((n, _G_TILES * TN), F32),
                   jax.ShapeDtypeStruct((n, d), MM)],
        compiler_params=_cparams(("parallel", "arbitrary")),
        name="norm_project",
    )(x2d, g, w, cos, s1, s2)


def _proj_kv_kernel(h_ref, w_ref, cos_ref, s1_ref, s2_ref,
                    kc_ref, vc_ref, ks_ref, vs_ref, kw_ref, vw_ref, ksb_ref, vsb_ref, kwb_ref, vwb_ref):
    j = pl.program_id(1)
    z = jnp.dot(h_ref[...], w_ref[...], preferred_element_type=F32)
    outs = ((kc_ref, None, True), (vc_ref, None, False), (ks_ref, ksb_ref, True),
            (vs_ref, vsb_ref, False), (kw_ref, kwb_ref, True), (vw_ref, vwb_ref, False))
    for t, (ref, packed_ref, roped) in enumerate(outs):
        @pl.when(j == t)
        def _(ref=ref, packed_ref=packed_ref, roped=roped):
            v = _rope_tile(z, cos_ref[...], s1_ref[...], s2_ref[...]) if roped else z
            for h in range(KV_HEADS):
                ref[pl.ds(h, z.shape[0], stride=KV_HEADS), :] = v[:, h * HEAD_DIM:(h + 1) * HEAD_DIM]
            if packed_ref is not None:
                packed_ref[...] = v.astype(MM)


def _project_kv(h, w, tabs, tm):
    n, d = h.shape
    cos, s1, s2 = tabs
    tab_blocks = cos.shape[0] // tm
    tab = lambda i, j: (i % tab_blocks, 0)
    return pl.pallas_call(
        _proj_kv_kernel,
        grid=(n // tm, _KV_TILES),
        in_specs=[pl.BlockSpec((tm, d), lambda i, j: (i, 0)),
                  pl.BlockSpec((d, TN), lambda i, j: (0, j)),
                  pl.BlockSpec((tm, LANES), tab), pl.BlockSpec((tm, LANES), tab),
                  pl.BlockSpec((tm, LANES), tab)],
        out_specs=[pl.BlockSpec((tm * KV_HEADS, HEAD_DIM), lambda i, j: (i, 0))] * 6
                  + [pl.BlockSpec((tm, KV_COLS), lambda i, j: (i, 0))] * 4,
        out_shape=[jax.ShapeDtypeStruct((n * KV_HEADS, HEAD_DIM), F32)] * 6
                  + [jax.ShapeDtypeStruct((n, KV_COLS), MM)] * 4,
        compiler_params=_cparams(("parallel", "arbitrary")),
        name="kv_project",
    )(h, w, cos, s1, s2)


def _rope_tables(pos):
    half = ROPE_DIM // 2
    inv = jnp.power(jnp.float32(ROPE_THETA), -jnp.arange(0, ROPE_DIM, 2, dtype=F32) / ROPE_DIM)
    ang = pos.astype(F32)[:, None] * inv[None, :]
    cos, sin = jnp.cos(ang), jnp.sin(ang)
    r = pos.shape[0]
    one = jnp.ones((r, HEAD_DIM - ROPE_DIM), F32)
    zero = jnp.zeros((r, HEAD_DIM - ROPE_DIM), F32)
    zh = jnp.zeros((r, half), F32)
    return (jnp.concatenate([cos, cos, one], axis=1),
            jnp.concatenate([zh, sin, zero], axis=1),
            jnp.concatenate([-sin, zh, zero], axis=1))


def _layout_w_in(w_in):
    c_a = 2 * A_WIDTH + Q_COLS
    c_kv = c_a + 6 * KV_COLS
    g_br, g_ab = w_in[:, c_kv:c_kv + 3 * N_HEADS], w_in[:, c_kv + 3 * N_HEADS:]
    per_head = 3 * Q_PER_KV
    g_br = g_br.reshape(-1, KV_HEADS, per_head)
    g_br = jnp.pad(g_br, ((0, 0), (0, 0), (0, LANES - per_head))).reshape(-1, KV_HEADS * LANES)
    w_a = jnp.concatenate([w_in[:, :c_a], g_ab, g_br], axis=1).astype(MM)
    return w_a, w_in[:, c_a:c_kv].astype(MM)


def _gmlp_kernel(u_ref, v_ref, lng_ref, lnb_ref, mix_ref, bias_ref, o_ref, *maybe_vout):
    u = jax.nn.gelu(u_ref[...])
    v = jax.nn.gelu(v_ref[...])
    mu = jnp.mean(v, axis=-1, keepdims=True)
    vc = v - mu
    var = jnp.mean(vc * vc, axis=-1, keepdims=True)
    v = vc * lax.rsqrt(var + EPS) * lng_ref[...] + lnb_ref[...]
    if maybe_vout:
        maybe_vout[0][...] = v
    for g in range(A_GROUPS):
        sl = slice(g * A_GROUP_DIM, (g + 1) * A_GROUP_DIM)
        s = jnp.dot(mix_ref[g], v[:, sl].astype(MM), preferred_element_type=F32)
        b = bias_ref[g]
        s = s + jnp.concatenate([b] * (A_GROUP_DIM // LANES), axis=1)
        o_ref[:, sl] = (u[:, sl] * s).astype(o_ref.dtype)


def _gmlp(a, ln_g, ln_b, mix, bias, emit_v):
    n = a.shape[0]
    out_shape = [jax.ShapeDtypeStruct((n, A_WIDTH), MM)]
    out_specs = [pl.BlockSpec((CHUNK, A_WIDTH), lambda i: (i, 0))]
    if emit_v:
        out_shape.append(jax.ShapeDtypeStruct((n, A_WIDTH), F32))
        out_specs.append(pl.BlockSpec((CHUNK, A_WIDTH), lambda i: (i, 0)))
    const3 = lambda i: (0, 0, 0)
    return pl.pallas_call(
        _gmlp_kernel,
        grid=(n // CHUNK,),
        in_specs=[pl.BlockSpec((CHUNK, A_WIDTH), lambda i: (i, 0)),
                  pl.BlockSpec((CHUNK, A_WIDTH), lambda i: (i, 1)),
                  pl.BlockSpec((1, A_WIDTH), lambda i: (0, 0)),
                  pl.BlockSpec((1, A_WIDTH), lambda i: (0, 0)),
                  pl.BlockSpec((A_GROUPS, CHUNK, CHUNK), const3),
                  pl.BlockSpec((A_GROUPS, CHUNK, LANES), const3)],
        out_specs=out_specs,
        out_shape=out_shape,
        compiler_params=_cparams(("parallel",)),
        name="gmlp",
    )(a, a, ln_g, ln_b, mix, bias)


def _gmlp_mix(w_s, b_s, t):
    ln = min(t, CHUNK)
    causal = jnp.tril(jnp.ones((ln, ln), dtype=bool))
    w = jnp.where(causal, w_s[:, :ln, :ln], 0.0)
    reps = CHUNK // ln
    eye = jnp.eye(reps, dtype=w.dtype)
    mix = jnp.einsum('ab,gij->gaibj', eye, w).reshape(A_GROUPS, CHUNK, CHUNK)
    bias = jnp.tile(b_s[:, :ln], (1, reps))
    return mix.astype(MM), jnp.broadcast_to(bias[:, :, None], (A_GROUPS, CHUNK, LANES)).astype(F32)


def _compress_kernel(n_pages, pt_ref, *refs):
    k_pages, v_pages = refs[:n_pages], refs[n_pages:2 * n_pages]
    pek_ref, w1k_ref, w2k_ref, pev_ref, w1v_ref, w2v_ref, kc_ref, vc_ref, scr = refs[2 * n_pages:]
    blocks = n_pages * PAGE_SIZE // CMP_BLOCK
    half = blocks // 2

    page_rows = PAGE_SIZE * KV_HEADS
    blk_rows = CMP_BLOCK * KV_HEADS

    region = half * CMP_PITCH

    def one(pages, pe_ref, w1_ref, w2_ref, out_ref):
        for p in range(n_pages):
            page = pages[p][0, 0] + pe_ref[...]
            for bl in range(PAGE_SIZE // CMP_BLOCK):
                m = p * (PAGE_SIZE // CMP_BLOCK) + bl
                off = (m % 2) * region + (m // 2) * CMP_PITCH
                scr[off:off + blk_rows, :] = page[bl * blk_rows:(bl + 1) * blk_rows, :]
        acc = jnp.zeros((KV_HEADS * blocks, HEAD_DIM), F32)
        for jp in range(CMP_BLOCK // 2):
            parts = []
            for j in (2 * jp, 2 * jp + 1):
                rows = [scr[pl.ds(parity * region + j * KV_HEADS + h, half, stride=CMP_PITCH), :]
                        for h in range(KV_HEADS) for parity in (0, 1)]
                parts.append(jnp.concatenate(rows, axis=0).astype(MM))
            lhs = jnp.concatenate(parts, axis=1)
            acc = acc + jnp.dot(lhs, w1_ref[jp], preferred_element_type=F32)
        hid = jax.nn.gelu(acc)
        out = jnp.dot(hid.astype(MM), w2_ref[...], preferred_element_type=F32)
        out_ref[0] = out.reshape(KV_HEADS, blocks, HEAD_DIM)

    one(k_pages, pek_ref, w1k_ref, w2k_ref, kc_ref)
    one(v_pages, pev_ref, w1v_ref, w2v_ref, vc_ref)


def _compress(pool_k, pool_v, layer, page_table, cmp_k, cmp_v):
    n_seq, n_pages = page_table.shape
    blocks = n_pages * PAGE_SIZE // CMP_BLOCK
    page_rows = PAGE_SIZE * KV_HEADS

    def page_spec(p):
        return pl.BlockSpec((1, 1, page_rows, HEAD_DIM), lambda s, pt, p=p: (layer, pt[s, p], 0, 0))

    def prep(c):
        pe, w1, w2 = c
        pe_t = jnp.tile(jnp.repeat(pe, KV_HEADS, axis=0), (PAGE_SIZE // CMP_BLOCK, 1)).astype(F32)
        return pe_t, w1.reshape(CMP_BLOCK // 2, 2 * HEAD_DIM, HEAD_DIM).astype(MM), w2.astype(MM)

    const2 = lambda s, pt: (0, 0)
    const3 = lambda s, pt: (0, 0, 0)
    w_specs = [pl.BlockSpec((page_rows, HEAD_DIM), const2),
               pl.BlockSpec((CMP_BLOCK // 2, 2 * HEAD_DIM, HEAD_DIM), const3),
               pl.BlockSpec((HEAD_DIM, HEAD_DIM), const2)]
    out_spec = pl.BlockSpec((1, KV_HEADS, blocks, HEAD_DIM), lambda s, pt: (s, 0, 0, 0))
    out_shape = jax.ShapeDtypeStruct((n_seq, KV_HEADS, blocks, HEAD_DIM), F32)
    return pl.pallas_call(
        functools.partial(_compress_kernel, n_pages),
        grid_spec=pltpu.PrefetchScalarGridSpec(
            num_scalar_prefetch=1,
            grid=(n_seq,),
            in_specs=[page_spec(p) for p in range(n_pages)] * 2 + w_specs * 2,
            out_specs=[out_spec, out_spec],
            scratch_shapes=[pltpu.VMEM((blocks * CMP_PITCH, HEAD_DIM), F32)]),
        out_shape=[out_shape, out_shape],
        compiler_params=_cparams(("arbitrary",)),
        name="compress",
    )(page_table, *([pool_k] * n_pages), *([pool_v] * n_pages), *prep(cmp_k), *prep(cmp_v))


def _select_rows(imp, qpos, n_blocks):
    j = lax.broadcasted_iota(jnp.int32, (n_blocks, 1), 0)
    forced = (j == 0) | (j == (qpos >> _SLC_SHIFT))
    future = j * SLC_BLOCK > qpos
    score = jnp.where(future, -1.0, imp + jnp.where(forced, FORCE_BONUS, 0.0))
    rank = jnp.zeros(score.shape, jnp.int32)
    for i in range(n_blocks):
        ri = score[i:i + 1, :]
        beats = (ri > score) | ((ri == score) & (i < j))
        rank = rank + beats.astype(jnp.int32)
    return rank < min(N_SELECT, n_blocks)


def _prompt_attn_kernel(q_ref, kc_ref, vc_ref, ks_ref, vs_ref, kw_ref, vw_ref, gbr_ref, o_ref,
                        vst_scr, vwt_scr, bias_scr, acc_scr):
    n = pl.program_id(2)
    seq = ks_ref.shape[0]
    n_tiles = seq // Q_BLOCK
    n_sel_blocks = seq // SLC_BLOCK
    n_cmp = kc_ref.shape[2]
    lane_groups = [slice(g * Q_BLOCK, (g + 1) * Q_BLOCK) for g in range(Q_PER_KV)]

    tiles_per_chunk = SEL_CHUNK // Q_BLOCK

    @pl.when(n == 0)
    def _():
        for kt in range(n_tiles):
            rows = slice(kt * Q_BLOCK, (kt + 1) * Q_BLOCK)
            c, j = divmod(kt, tiles_per_chunk)
            vst_scr[c, :, j * Q_BLOCK:(j + 1) * Q_BLOCK] = vs_ref[rows, :].astype(F32).T.astype(MM)
            vwt_scr[kt] = vw_ref[rows, :].astype(F32).T.astype(MM)

    q = q_ref[...] * (SCALE * LOG2E)
    q_t = jnp.concatenate([q[:, g * HEAD_DIM:(g + 1) * HEAD_DIM].T for g in range(Q_PER_KV)],
                          axis=1).astype(MM)
    qpos = n * Q_BLOCK + lax.broadcasted_iota(jnp.int32, (1, Q_BLOCK), 1)

    s_c = jnp.dot(kc_ref[0, 0].astype(MM), q_t, preferred_element_type=F32)
    r = lax.broadcasted_iota(jnp.int32, (n_cmp, 1), 0)
    blk = jnp.where(r < n_cmp // 2, 2 * r, 2 * (r - n_cmp // 2) + 1)
    ok = ((blk + 1) * CMP_BLOCK - 1) <= qpos
    imp = jnp.zeros((n_cmp, Q_BLOCK), F32)
    probs = []
    for sl in lane_groups:
        s = jnp.where(ok, s_c[:, sl], NEG)
        e = jnp.exp2(s - jnp.max(s, axis=0, keepdims=True))
        p = jnp.where(ok, e / jnp.sum(e, axis=0, keepdims=True), 0.0)
        imp = imp + p
        probs.append(p.astype(MM))
    oc_t = jnp.dot(vc_ref[0, 0].T.astype(MM), jnp.concatenate(probs, axis=1), preferred_element_type=F32)

    sel = _select_rows(imp[:n_cmp // 2] + imp[n_cmp // 2:], qpos, n_sel_blocks)
    sel_bias = jnp.where(sel, 0.0, NEG)
    blocks_per_chunk = SEL_CHUNK // SLC_BLOCK
    key_row = lax.broadcasted_iota(jnp.int32, (SEL_CHUNK, 1), 0)
    n_chunks = lax.div(n, tiles_per_chunk) + 1
    for c in range(seq // SEL_CHUNK):
        @pl.when(c < n_chunks)
        def _(c=c):
            tile = jnp.concatenate(
                [jnp.broadcast_to(sel_bias[c * blocks_per_chunk + i:c * blocks_per_chunk + i + 1, :],
                                  (SLC_BLOCK, Q_BLOCK)) for i in range(blocks_per_chunk)], axis=0)
            bias_scr[c] = jnp.where(c * SEL_CHUNK + key_row <= qpos, tile, NEG)

    acc_scr[...] = jnp.zeros_like(acc_scr)

    def body(c, carry):
        ms, ls = carry
        start = pl.multiple_of(c * SEL_CHUNK, SEL_CHUNK)
        s_t = jnp.dot(ks_ref[pl.ds(start, SEL_CHUNK), :], q_t, preferred_element_type=F32)
        bias = bias_scr[c]
        new_ms, new_ls, ps, alphas = [], [], [], []
        for g, sl in enumerate(lane_groups):
            s = s_t[:, sl] + bias
            m_new = jnp.maximum(ms[g], jnp.max(s, axis=0, keepdims=True))
            alpha = jnp.exp2(ms[g] - m_new)
            p = jnp.exp2(s - m_new)
            new_ls.append(alpha * ls[g] + jnp.sum(p, axis=0, keepdims=True))
            new_ms.append(m_new)
            alphas.append(alpha)
            ps.append(p.astype(MM))
        pv = jnp.dot(vst_scr[c], jnp.concatenate(ps, axis=1), preferred_element_type=F32)
        for g, sl in enumerate(lane_groups):
            acc_scr[:, sl] = alphas[g] * acc_scr[:, sl] + pv[:, sl]
        return tuple(new_ms), tuple(new_ls)

    init = (tuple(jnp.full((1, Q_BLOCK), NEG, F32) for _ in lane_groups),
            tuple(jnp.zeros((1, Q_BLOCK), F32) for _ in lane_groups))
    _, ls_s = lax.fori_loop(0, n_chunks, body, init)

    win_keys = WINDOW + Q_BLOCK
    w_tile0 = jnp.maximum(n - WINDOW // Q_BLOCK, 0)
    w_start = pl.multiple_of(w_tile0 * Q_BLOCK, Q_BLOCK)
    s_w = jnp.dot(kw_ref[pl.ds(w_start, win_keys), :], q_t, preferred_element_type=F32)
    dist = qpos - (w_start + lax.broadcasted_iota(jnp.int32, (win_keys, 1), 0))
    w_bias = jnp.where((dist >= 0) & (dist < WINDOW), 0.0, NEG)
    ls_w, ps = [], []
    for sl in lane_groups:
        s = s_w[:, sl] + w_bias
        p = jnp.exp2(s - jnp.max(s, axis=0, keepdims=True))
        ls_w.append(jnp.sum(p, axis=0, keepdims=True))
        ps.append(p.astype(MM))
    p_w = jnp.concatenate(ps, axis=1)
    ow_t = functools.reduce(
        lambda a, b: a + b,
        [jnp.dot(vwt_scr[w_tile0 + j], p_w[j * Q_BLOCK:(j + 1) * Q_BLOCK, :], preferred_element_type=F32)
         for j in range(win_keys // Q_BLOCK)])

    gate_t = jax.nn.sigmoid(gbr_ref[...]).T
    for g, sl in enumerate(lane_groups):
        o_t = (oc_t[:, sl] * gate_t[3 * g:3 * g + 1, :]
               + (acc_scr[:, sl] / ls_s[g]) * gate_t[3 * g + 1:3 * g + 2, :]
               + (ow_t[:, sl] / ls_w[g]) * gate_t[3 * g + 2:3 * g + 3, :])
        o_ref[:, g * HEAD_DIM:(g + 1) * HEAD_DIM] = o_t.T.astype(o_ref.dtype)


def _prompt_attention(a, kc, vc, ks, vs, kw, vw, gt, batch, seq):
    n_qb = seq // Q_BLOCK
    q_col0 = 2 * A_WIDTH // (Q_PER_KV * HEAD_DIM)
    gbr_col0 = 2 * D_MODEL // LANES
    cmp_spec = pl.BlockSpec((1, 1, kc.shape[2], HEAD_DIM), lambda b, h, n: (b, h, 0, 0))
    kv_spec = pl.BlockSpec((seq, HEAD_DIM), lambda b, h, n: (b, h))
    assert seq % SEL_CHUNK == 0 and seq >= WINDOW + Q_BLOCK
    scratch = [pltpu.VMEM((seq // SEL_CHUNK, HEAD_DIM, SEL_CHUNK), MM),
               pltpu.VMEM((n_qb, HEAD_DIM, Q_BLOCK), MM),
               pltpu.VMEM((seq // SEL_CHUNK, SEL_CHUNK, Q_BLOCK), F32),
               pltpu.VMEM((HEAD_DIM, Q_PER_KV * Q_BLOCK), F32)]
    return pl.pallas_call(
        _prompt_attn_kernel,
        grid=(batch, KV_HEADS, n_qb),
        in_specs=[pl.BlockSpec((Q_BLOCK, Q_PER_KV * HEAD_DIM), lambda b, h, n: (b * n_qb + n, q_col0 + h)),
                  cmp_spec, cmp_spec, kv_spec, kv_spec, kv_spec, kv_spec,
                  pl.BlockSpec((Q_BLOCK, LANES), lambda b, h, n: (b * n_qb + n, gbr_col0 + h))],
        out_specs=pl.BlockSpec((Q_BLOCK, Q_PER_KV * HEAD_DIM), lambda b, h, n: (b * n_qb + n, h)),
        out_shape=jax.ShapeDtypeStruct((batch * seq, Q_COLS), MM),
        scratch_shapes=scratch,
        compiler_params=_cparams(("parallel", "parallel", "arbitrary")),
        name="prompt_attention",
    )(a, kc, vc, ks, vs, kw, vw, gt)


def _dot_nt(a, b):
    return lax.dot_general(a, b, (((1,), (1,)), ((), ())), preferred_element_type=F32)


def _select_cols(imp, qpos, n_blocks):
    j = lax.broadcasted_iota(jnp.int32, (1, n_blocks), 1)
    forced = (j == 0) | (j == (qpos >> _SLC_SHIFT))
    future = j * SLC_BLOCK > qpos
    score = jnp.where(future, -1.0, imp + jnp.where(forced, FORCE_BONUS, 0.0))
    rank = jnp.zeros(score.shape, jnp.int32)
    for i in range(n_blocks):
        ci = score[:, i:i + 1]
        beats = (ci > score) | ((ci == score) & (i < j))
        rank = rank + beats.astype(jnp.int32)
    return (rank < min(N_SELECT, n_blocks)).astype(F32)


def _softmax_parts(parts):
    m = functools.reduce(jnp.maximum, [jnp.max(s, axis=1, keepdims=True) for s, _ in parts])
    es = [jnp.exp2(s - m) for s, _ in parts]
    den = functools.reduce(lambda a, b: a + b, [jnp.sum(e, axis=1, keepdims=True) for e in es])
    acc = None
    for e, (_, v) in zip(es, parts):
        o = jnp.dot((e / den).astype(MM), v, preferred_element_type=F32)
        acc = o if acc is None else acc + o
    return acc


def _sample_attn_kernel(n_pages, pt_ref, *refs):
    ks_pages, vs_pages = refs[:n_pages], refs[n_pages:2 * n_pages]
    (q_ref, kc_ref, vc_ref, kbuf_ref, vbuf_ref, ksn_ref, vsn_ref, kwn_ref, vwn_ref, gbr_ref,
     o_ref, kwo_ref, vwo_ref) = refs[2 * n_pages:]
    t = q_ref.shape[1]
    past = n_pages * PAGE_SIZE
    nb_past = past // SLC_BLOCK
    buf_rows = kbuf_ref.shape[2]
    wb = buf_rows // KV_HEADS
    new_rows = t * KV_HEADS
    hq = KV_HEADS * t
    n_cmp = kc_ref.shape[2]
    head_shift, t_shift, cmp_shift = (v.bit_length() - 1 for v in (KV_HEADS, t, n_cmp))
    assert (KV_HEADS, t, n_cmp) == (1 << head_shift, 1 << t_shift, 1 << cmp_shift) and LANES % KV_HEADS == 0

    q = q_ref[0] * (SCALE * LOG2E)
    qb = jnp.concatenate([q[:, (h * Q_PER_KV + g) * HEAD_DIM:(h * Q_PER_KV + g + 1) * HEAD_DIM]
                          for g in range(Q_PER_KV) for h in range(KV_HEADS)], axis=0).astype(MM)
    ri = lax.broadcasted_iota(jnp.int32, (Q_PER_KV * hq, 1), 0)
    row_h = (ri >> t_shift) & (KV_HEADS - 1)
    row_t = ri & (t - 1)
    qpos = past + row_t

    def key_cols(n):
        c = lax.broadcasted_iota(jnp.int32, (1, n), 1)
        return c & (KV_HEADS - 1), c >> head_shift

    kwo_ref[0, 0, :buf_rows - new_rows] = kbuf_ref[0, 0, new_rows:]
    kwo_ref[0, 0, buf_rows - new_rows:] = kwn_ref[0]
    vwo_ref[0, 0, :buf_rows - new_rows] = vbuf_ref[0, 0, new_rows:]
    vwo_ref[0, 0, buf_rows - new_rows:] = vwn_ref[0]

    kc = kc_ref[0].reshape(KV_HEADS * n_cmp, HEAD_DIM).astype(MM)
    vc = vc_ref[0].reshape(KV_HEADS * n_cmp, HEAD_DIM).astype(MM)
    col = lax.broadcasted_iota(jnp.int32, (1, KV_HEADS * n_cmp), 1)
    slot = col & (n_cmp - 1)
    blk = jnp.where(slot < n_cmp // 2, 2 * slot, 2 * (slot - n_cmp // 2) + 1)
    ok = ((col >> cmp_shift) == row_h) & (((blk + 1) * CMP_BLOCK - 1) <= qpos)
    s = jnp.where(ok, _dot_nt(qb, kc), NEG)
    e = jnp.exp2(s - jnp.max(s, axis=1, keepdims=True))
    p = jnp.where(ok, e / jnp.sum(e, axis=1, keepdims=True), 0.0)
    o_c = jnp.dot(p.astype(MM), vc, preferred_element_type=F32)
    p = functools.reduce(lambda a, b: a + b, [p[:, h * n_cmp:(h + 1) * n_cmp] for h in range(KV_HEADS)])
    p = functools.reduce(lambda a, b: a + b, [p[g * hq:(g + 1) * hq] for g in range(Q_PER_KV)])
    imp = jnp.concatenate([p[:, :n_cmp // 2] + p[:, n_cmp // 2:], jnp.zeros((hq, 1), F32)], axis=1)
    sel = _select_cols(imp, qpos[:hq], nb_past + 1)
    sel = jnp.concatenate([sel] * Q_PER_KV, axis=0)

    page_h, page_r = key_cols(PAGE_SIZE * KV_HEADS)
    own_head = page_h == row_h
    page_blk = page_r >> _SLC_SHIFT
    per_page = PAGE_SIZE // SLC_BLOCK
    parts = []
    for pg in range(n_pages):
        ok = own_head & functools.reduce(
            lambda a, b: a | b,
            [(page_blk == c) & (sel[:, pg * per_page + c:pg * per_page + c + 1] > 0.5) for c in range(per_page)])
        parts.append((jnp.where(ok, _dot_nt(qb, ks_pages[pg][0, 0].astype(MM)), NEG),
                      vs_pages[pg][0, 0].astype(MM)))
    new_h, new_t = key_cols(new_rows)
    new_ok = (new_h == row_h) & (new_t <= row_t)
    parts.append((jnp.where(new_ok & (sel[:, nb_past:nb_past + 1] > 0.5),
                            _dot_nt(qb, ksn_ref[0].astype(MM)), NEG), vsn_ref[0].astype(MM)))
    o_s = _softmax_parts(parts)

    buf_h, buf_r = key_cols(buf_rows)
    buf_pos = past - wb + buf_r
    dist = qpos - buf_pos
    buf_ok = (buf_h == row_h) & (dist >= 0) & (dist < WINDOW) & (buf_pos >= 0)
    o_w = _softmax_parts([
        (jnp.where(buf_ok, _dot_nt(qb, kbuf_ref[0, 0].astype(MM)), NEG), vbuf_ref[0, 0].astype(MM)),
        (jnp.where(new_ok, _dot_nt(qb, kwn_ref[0].astype(MM)), NEG), vwn_ref[0].astype(MM))])

    gate = jax.nn.sigmoid(gbr_ref[0])
    for g in range(Q_PER_KV):
        for h in range(KV_HEADS):
            rs = slice(g * hq + h * t, g * hq + (h + 1) * t)
            c0 = h * LANES + 3 * g
            o = (o_c[rs] * gate[:, c0:c0 + 1] + o_s[rs] * gate[:, c0 + 1:c0 + 2]
                 + o_w[rs] * gate[:, c0 + 2:c0 + 3])
            col0 = (h * Q_PER_KV + g) * HEAD_DIM
            o_ref[0, :, col0:col0 + HEAD_DIM] = o.astype(o_ref.dtype)


def _sample_attention(a3, kc, vc, pool_ks, pool_vs, layer, page_table, kbuf, vbuf, ksn, vsn, kwn, vwn, gt3):
    db, t, _ = a3.shape
    n_pages = page_table.shape[1]
    buf_rows = kbuf.shape[2]
    page_rows = PAGE_SIZE * KV_HEADS

    def page_spec(p):
        return pl.BlockSpec((1, 1, page_rows, HEAD_DIM), lambda s, pt, p=p: (layer, pt[s, p], 0, 0))

    row3 = lambda s, pt: (s, 0, 0)
    cmp_spec = pl.BlockSpec((1, KV_HEADS, kc.shape[2], HEAD_DIM), lambda s, pt: (s, 0, 0, 0))
    buf_spec = pl.BlockSpec((1, 1, buf_rows, HEAD_DIM), lambda s, pt: (layer, s, 0, 0))
    out_buf_spec = pl.BlockSpec((1, 1, buf_rows, HEAD_DIM), lambda s, pt: (0, s, 0, 0))
    new_spec = pl.BlockSpec((1, t * KV_HEADS, HEAD_DIM), row3)
    buf_shape = jax.ShapeDtypeStruct((1, db, buf_rows, HEAD_DIM), F32)
    return pl.pallas_call(
        functools.partial(_sample_attn_kernel, n_pages),
        grid_spec=pltpu.PrefetchScalarGridSpec(
            num_scalar_prefetch=1,
            grid=(db,),
            in_specs=[page_spec(p) for p in range(n_pages)] * 2
                     + [pl.BlockSpec((1, t, Q_COLS), lambda s, pt: (s, 0, 2 * A_WIDTH // Q_COLS)),
                        cmp_spec, cmp_spec, buf_spec, buf_spec, new_spec, new_spec, new_spec, new_spec,
                        pl.BlockSpec((1, t, KV_HEADS * LANES),
                                     lambda s, pt: (s, 0, 2 * D_MODEL // (KV_HEADS * LANES)))],
            out_specs=[pl.BlockSpec((1, t, Q_COLS), row3), out_buf_spec, out_buf_spec]),
        out_shape=[jax.ShapeDtypeStruct((db, t, Q_COLS), MM), buf_shape, buf_shape],
        compiler_params=_cparams(("arbitrary",)),
        name="sample_attention",
    )(page_table, *([pool_ks] * n_pages), *([pool_vs] * n_pages),
      a3, kc, vc, kbuf, vbuf, ksn, vsn, kwn, vwn, gt3)


def _mix_kernel(a_ref, o_ref, wpa_ref, wpb_ref, ga_ref, gb_ref, m_ref):
    y_a = jnp.dot(a_ref[...], wpa_ref[...], preferred_element_type=F32)
    y_b = jnp.dot(o_ref[...], wpb_ref[...], preferred_element_type=F32)
    m = jax.nn.sigmoid(ga_ref[...]) * y_a + jax.nn.sigmoid(gb_ref[...]) * y_b
    m_ref[...] = m.astype(m_ref.dtype)


def _mix(a_out, o, w_pa, w_pb, gt, tm):
    n = a_out.shape[0]
    gb0 = D_MODEL // TN
    return pl.pallas_call(
        _mix_kernel,
        grid=(n // tm, D_MODEL // TN),
        in_specs=[pl.BlockSpec((tm, A_WIDTH), lambda i, j: (i, 0)),
                  pl.BlockSpec((tm, Q_COLS), lambda i, j: (i, 0)),
                  pl.BlockSpec((A_WIDTH, TN), lambda i, j: (0, j)),
                  pl.BlockSpec((Q_COLS, TN), lambda i, j: (0, j)),
                  pl.BlockSpec((tm, TN), lambda i, j: (i, j)),
                  pl.BlockSpec((tm, TN), lambda i, j: (i, gb0 + j))],
        out_specs=pl.BlockSpec((tm, TN), lambda i, j: (i, j)),
        out_shape=jax.ShapeDtypeStruct((n, D_MODEL), MM),
        compiler_params=_cparams(("parallel", "arbitrary")),
        name="merge_gate",
    )(a_out, o, w_pa, w_pb, gt, gt)


def _out_proj_kernel(m_ref, w_ref, x_ref, o_ref):
    o_ref[...] = x_ref[...] + jnp.dot(m_ref[...], w_ref[...], preferred_element_type=F32)


def _out_proj(m, w_o, x2d, tm):
    n = m.shape[0]
    return pl.pallas_call(
        _out_proj_kernel,
        grid=(n // tm, D_MODEL // TN),
        in_specs=[pl.BlockSpec((tm, D_MODEL), lambda i, j: (i, 0)),
                  pl.BlockSpec((D_MODEL, TN), lambda i, j: (0, j)),
                  pl.BlockSpec((tm, TN), lambda i, j: (i, j))],
        out_specs=pl.BlockSpec((tm, TN), lambda i, j: (i, j)),
        out_shape=jax.ShapeDtypeStruct((n, D_MODEL), F32),
        compiler_params=_cparams(("parallel", "arbitrary")),
        name="out_proj",
    )(m, w_o, x2d)


def _route(logits):
    lane = lax.broadcasted_iota(jnp.int32, logits.shape, 1)
    big = jnp.int32(LANES)
    is_g = lane < N_GROUPS
    gl = jnp.where(is_g, logits, -jnp.inf)
    gm = jnp.max(gl, axis=1, keepdims=True)
    ge = jnp.exp(gl - gm)
    pg = ge / jnp.sum(ge, axis=1, keepdims=True)
    pg_top = jnp.max(pg, axis=1, keepdims=True)
    g_idx = jnp.min(jnp.where(is_g & (pg == pg_top), lane, big), axis=1, keepdims=True)
    in_grp = (lane >= N_GROUPS) & (((lane - N_GROUPS) >> 3) == g_idx)
    el = jnp.where(in_grp, logits, -jnp.inf)
    em = jnp.max(el, axis=1, keepdims=True)
    ee = jnp.exp(el - em)
    pe = ee / jnp.sum(ee, axis=1, keepdims=True)
    p1 = jnp.max(pe, axis=1, keepdims=True)
    i1 = jnp.min(jnp.where(in_grp & (pe == p1), lane, big), axis=1, keepdims=True)
    rest = in_grp & (lane != i1)
    p2 = jnp.max(jnp.where(rest, pe, -1.0), axis=1, keepdims=True)
    i2 = jnp.min(jnp.where(rest & (pe == p2), lane, big), axis=1, keepdims=True)
    tot = p1 + p2
    return jnp.where(lane == i1, pg_top * p1 / tot, 0.0) + jnp.where(lane == i2, pg_top * p2 / tot, 0.0)


def _moe_kernel(final_norm, x_ref, gf_ref, wr_hi_ref, wr_lo_ref, br_ref, wg_ref, wu_ref, wd_ref, gfin_ref,
                y_ref, h_scr, gate_scr, acc_scr):
    e = pl.program_id(1)

    @pl.when(e == 0)
    def _():
        x = x_ref[...]
        ms = jnp.mean(x * x, axis=-1, keepdims=True)
        h = x * lax.rsqrt(ms + EPS) * gf_ref[...]
        hi = h.astype(MM)
        lo = (h - hi.astype(F32)).astype(MM)
        logits = (jnp.dot(hi, wr_hi_ref[...], preferred_element_type=F32)
                  + (jnp.dot(hi, wr_lo_ref[...], preferred_element_type=F32)
                     + jnp.dot(lo, wr_hi_ref[...], preferred_element_type=F32))) + br_ref[...]
        h_scr[...] = hi
        gate_scr[...] = _route(logits)
        acc_scr[...] = jnp.zeros_like(acc_scr)

    h = h_scr[...]
    lane = lax.broadcasted_iota(jnp.int32, gate_scr.shape, 1)
    gcol = jnp.sum(jnp.where(lane == e + N_GROUPS, gate_scr[...], 0.0), axis=1, keepdims=True)
    hid = (jax.nn.silu(jnp.dot(h, wg_ref[0], preferred_element_type=F32))
           * jnp.dot(h, wu_ref[0], preferred_element_type=F32))
    acc_scr[...] += jnp.dot((hid * gcol).astype(MM), wd_ref[0], preferred_element_type=F32)

    @pl.when(e == pl.num_programs(1) - 1)
    def _():
        x2 = x_ref[...] + acc_scr[...]
        if final_norm:
            ms = jnp.mean(x2 * x2, axis=-1, keepdims=True)
            x2 = x2 * lax.rsqrt(ms + EPS) * gfin_ref[...]
        y_ref[...] = x2


def _moe(x1, g_ffn, wr_hi, wr_lo, b_r, w_gate, w_up, w_down, g_final, tm, final_norm):
    n = x1.shape[0]
    row = lambda i, e: (i, 0)
    const = lambda i, e: (0, 0)
    return pl.pallas_call(
        functools.partial(_moe_kernel, final_norm),
        grid=(n // tm, N_EXPERTS),
        in_specs=[pl.BlockSpec((tm, D_MODEL), row),
                  pl.BlockSpec((1, D_MODEL), const),
                  pl.BlockSpec((D_MODEL, LANES), const),
                  pl.BlockSpec((D_MODEL, LANES), const),
                  pl.BlockSpec((1, LANES), const),
                  pl.BlockSpec((1, D_MODEL, EXPERT_DIM), lambda i, e: (e, 0, 0)),
                  pl.BlockSpec((1, D_MODEL, EXPERT_DIM), lambda i, e: (e, 0, 0)),
                  pl.BlockSpec((1, EXPERT_DIM, D_MODEL), lambda i, e: (e, 0, 0)),
                  pl.BlockSpec((1, D_MODEL), const)],
        out_specs=pl.BlockSpec((tm, D_MODEL), row),
        out_shape=jax.ShapeDtypeStruct((n, D_MODEL), F32),
        scratch_shapes=[pltpu.VMEM((tm, D_MODEL), MM), pltpu.VMEM((tm, LANES), F32),
                        pltpu.VMEM((tm, D_MODEL), F32)],
        compiler_params=_cparams(("parallel", "arbitrary")),
        name="moe_final_norm",
    )(x1, g_ffn, wr_hi, wr_lo, b_r, w_gate, w_up, w_down, g_final)


def _router_weights(w_rg, b_rg, w_re, b_re):
    w = jnp.concatenate([w_rg, w_re], axis=1)
    w = jnp.pad(w, ((0, 0), (0, LANES - w.shape[1])))
    b = jnp.pad(jnp.concatenate([b_rg, b_re]), (0, LANES - N_GROUPS - N_EXPERTS))[None, :]
    hi = w.astype(MM)
    lo = (w - hi.astype(F32)).astype(MM)
    return hi, lo, b.astype(F32)


def _row_tile(n, tm=512):
    return tm if n % tm == 0 else n


def kernel(x_prompt, x_sample, cache_k_cmp, cache_v_cmp, cache_k_slc, cache_v_slc, state_k_win, state_v_win, page_table, g_mix, w_in, a_ln_g, a_ln_b, a_w_s, a_b_s, cmp_pe_k, cmp_w1_k, cmp_w2_k, cmp_pe_v, cmp_w1_v, cmp_w2_v, w_pa, w_pb, w_o, g_ffn, w_rg, b_rg, w_re, b_re, w_gate, w_up, w_down, g_final):
    depth = g_mix.shape[0]
    b, s, d = x_prompt.shape
    db, t, _ = x_sample.shape
    n_pages = page_table.shape[1]
    past = n_pages * PAGE_SIZE
    wl = min(WINDOW, s)
    assert s % Q_BLOCK == 0 and CHUNK % t == 0 and (db * t) % CHUNK == 0 and t < CMP_BLOCK

    tabs_p = _rope_tables(jnp.arange(s))
    tabs_s = _rope_tables(past + (jnp.arange(db * t) % t))
    prompt_pages = jnp.arange(b * s // PAGE_SIZE, dtype=jnp.int32).reshape(b, s // PAGE_SIZE)
    g_final2 = g_final[None, :]

    xp = x_prompt.reshape(b * s, d)
    xs = x_sample.reshape(db * t, d)
    outs_p = [[] for _ in range(6)]
    outs_s = [[] for _ in range(7)]
    for l in range(depth):
        w_a, w_kv = _layout_w_in(w_in[l])
        g_l = g_mix[l][None, :]
        ln_g, ln_b = a_ln_g[l][None, :], a_ln_b[l][None, :]
        cmp_k = (cmp_pe_k[l], cmp_w1_k[l], cmp_w2_k[l])
        cmp_v = (cmp_pe_v[l], cmp_w1_v[l], cmp_w2_v[l])
        wpa, wpb, wo = w_pa[l].astype(MM), w_pb[l].astype(MM), w_o[l].astype(MM)
        wr_hi, wr_lo, b_r = _router_weights(w_rg[l], b_rg[l], w_re[l], b_re[l])
        wg, wu, wd = w_gate[l].astype(MM), w_up[l].astype(MM), w_down[l].astype(MM)
        last = l == depth - 1

        def tail(x2d, a_out, o, gt):
            n = x2d.shape[0]
            m = _mix(a_out, o, wpa, wpb, gt, _row_tile(n, 1024))
            x1 = _out_proj(m, wo, x2d, _row_tile(n, 1024))
            return _moe(x1, g_ffn[l][None, :], wr_hi, wr_lo, b_r, wg, wu, wd, g_final2, _row_tile(n), last)

        a, gt, h = _project_a(xp, g_l, w_a, tabs_p, _row_tile(b * s, 1024))
        kc_r, vc_r, ks_r, vs_r, kw_r, vw_r, ks_b, vs_b, kw_b, vw_b = _project_kv(
            h, w_kv, tabs_p, _row_tile(b * s))
        mix_p, bias_p = _gmlp_mix(a_w_s[l], a_b_s[l], s)
        (a_out,) = _gmlp(a, ln_g, ln_b, mix_p, bias_p, False)
        pages = lambda r: r.reshape(1, b * s // PAGE_SIZE, PAGE_SIZE * KV_HEADS, HEAD_DIM)
        kc, vc = _compress(pages(kc_r), pages(vc_r), 0, prompt_pages, cmp_k, cmp_v)
        o = _prompt_attention(a, kc, vc, ks_b, vs_b, kw_b, vw_b, gt, b, s)
        xp = tail(xp, a_out, o, gt)
        heads = lambda r: r.reshape(b, s, KV_HEADS, HEAD_DIM)
        for lst, r in zip(outs_p, (kc_r, vc_r, ks_r, vs_r)):
            lst.append(heads(r))
        outs_p[4].append(heads(kw_r)[:, -wl:])
        outs_p[5].append(heads(vw_r)[:, -wl:])

        a, gt, h = _project_a(xs, g_l, w_a, tabs_s, _row_tile(db * t))
        kc_r, vc_r, ks_r, vs_r, kw_r, vw_r, _, _, _, _ = _project_kv(h, w_kv, tabs_s, _row_tile(db * t))
        mix_s, bias_s = _gmlp_mix(a_w_s[l], a_b_s[l], t)
        a_out, v_new = _gmlp(a, ln_g, ln_b, mix_s, bias_s, True)
        flat = lambda c: c.reshape(c.shape[0], c.shape[1], c.shape[2] * KV_HEADS, HEAD_DIM)
        kc, vc = _compress(flat(cache_k_cmp), flat(cache_v_cmp), l, page_table, cmp_k, cmp_v)
        r3 = lambda r: r.reshape(db, t, r.shape[-1])
        new = lambda r: r.reshape(db, t * KV_HEADS, HEAD_DIM)
        o, kwin, vwin = _sample_attention(
            r3(a), kc, vc, flat(cache_k_slc), flat(cache_v_slc), l, page_table,
            flat(state_k_win), flat(state_v_win), new(ks_r), new(vs_r), new(kw_r), new(vw_r), r3(gt))
        xs = tail(xs, a_out, o.reshape(db * t, Q_COLS), gt)
        r4 = lambda r: r.reshape(db, -1, KV_HEADS, HEAD_DIM)
        for lst, r in zip(outs_s, (kc_r, vc_r, ks_r, vs_r, kwin, vwin)):
            lst.append(r4(r))
        outs_s[6].append(v_new.reshape(db, t, A_WIDTH))

    y_prompt = xp.reshape(b, s, d)
    y_sample = xs.reshape(db, t, d)
    return (y_prompt, y_sample, *[jnp.stack(o) for o in outs_p], *[jnp.stack(o) for o in outs_s])
```

```python
import functools
import math

import jax
import jax.numpy as jnp
from jax import lax
from jax.experimental import pallas as pl
from jax.experimental.pallas import tpu as pltpu

F32 = jnp.float32
MM = jnp.bfloat16

D_MODEL = 2048
A_WIDTH = 2048
A_GROUPS = 8
A_GROUP_DIM = A_WIDTH // A_GROUPS
CHUNK = 128
N_HEADS = 16
KV_HEADS = 4
HEAD_DIM = 128
Q_PER_KV = N_HEADS // KV_HEADS
ROPE_DIM = HEAD_DIM // 4
ROPE_THETA = 500000.0
CMP_BLOCK = 32
SLC_BLOCK = 64
_SLC_SHIFT = 6
N_SELECT = 8
WINDOW = 512
Q_BLOCK = 128
PAGE_SIZE = 128
SCALE = HEAD_DIM ** -0.5
LOG2E = math.log2(math.e)
N_GROUPS = 4
EXPERTS_PER_GROUP = 8
N_EXPERTS = N_GROUPS * EXPERTS_PER_GROUP
EXPERT_DIM = 256
EPS = 1e-6
NEG = -1e30
FORCE_BONUS = 1e4
Q_COLS = N_HEADS * HEAD_DIM
KV_COLS = KV_HEADS * HEAD_DIM

LANES = 128
TN = 512
SEL_CHUNK = 512
CMP_PITCH = CMP_BLOCK * KV_HEADS + 8
EXPERT_TILE = 256
ROUTED_MIN_TOKENS = N_EXPERTS * EXPERT_TILE // 8
VMEM_LIMIT = 56 * 1024 * 1024

_A_TILES = 3 * D_MODEL // TN
_G_TILES = 2 * D_MODEL // TN + 1
_KV_TILES = 6


def _cparams(sem):
    return pltpu.CompilerParams(dimension_semantics=sem, vmem_limit_bytes=VMEM_LIMIT)


def _rope_tile(z, cos, s1, s2):
    outs = []
    for h in range(z.shape[1] // HEAD_DIM):
        zh = z[:, h * HEAD_DIM:(h + 1) * HEAD_DIM]
        outs.append(zh * cos + pltpu.roll(zh, ROPE_DIM // 2, 1) * s1
                    + pltpu.roll(zh, HEAD_DIM - ROPE_DIM // 2, 1) * s2)
    return jnp.concatenate(outs, axis=1)


def _proj_a_kernel(x_ref, g_ref, w_ref, cos_ref, s1_ref, s2_ref, a_ref, gt_ref, h_ref):
    j = pl.program_id(1)

    @pl.when(j == 0)
    def _():
        x = x_ref[...]
        ms = jnp.mean(x * x, axis=-1, keepdims=True)
        h_ref[...] = (x * lax.rsqrt(ms + EPS) * g_ref[...]).astype(MM)

    z = jnp.dot(h_ref[...], w_ref[...], preferred_element_type=F32)

    @pl.when(j < 2 * D_MODEL // TN)
    def _():
        a_ref[...] = z

    @pl.when((j >= 2 * D_MODEL // TN) & (j < _A_TILES))
    def _():
        a_ref[...] = _rope_tile(z, cos_ref[...], s1_ref[...], s2_ref[...])

    @pl.when(j >= _A_TILES)
    def _():
        gt_ref[...] = z


def _project_a(x2d, g, w, tabs, tm):
    n, d = x2d.shape
    cos, s1, s2 = tabs
    tab_blocks = cos.shape[0] // tm
    row = lambda i, j: (i, 0)
    tab = lambda i, j: (i % tab_blocks, 0)
    return pl.pallas_call(
        _proj_a_kernel,
        grid=(n // tm, _A_TILES + _G_TILES),
        in_specs=[pl.BlockSpec((tm, d), row),
                  pl.BlockSpec((1, d), lambda i, j: (0, 0)),
                  pl.BlockSpec((d, TN), lambda i, j: (0, j)),
                  pl.BlockSpec((tm, LANES), tab), pl.BlockSpec((tm, LANES), tab),
                  pl.BlockSpec((tm, LANES), tab)],
        out_specs=[pl.BlockSpec((tm, TN), lambda i, j: (i, jnp.minimum(j, _A_TILES - 1))),
                   pl.BlockSpec((tm, TN), lambda i, j: (i, jnp.maximum(j - _A_TILES, 0))),
                   pl.BlockSpec((tm, d), row)],
        out_shape=[jax.ShapeDtypeStruct((n, _A_TILES * TN), F32),
                   jax.ShapeDtypeStruct((n, _G_TILES * TN), F32),
                   jax.ShapeDtypeStruct((n, d), MM)],
        compiler_params=_cparams(("parallel", "arbitrary")),
        name="norm_project",
    )(x2d, g, w, cos, s1, s2)


def _proj_kv_kernel(h_ref, w_ref, cos_ref, s1_ref, s2_ref,
                    kc_ref, vc_ref, ks_ref, vs_ref, kw_ref, vw_ref, ksb_ref, vsb_ref, kwb_ref, vwb_ref):
    j = pl.program_id(1)
    z = jnp.dot(h_ref[...], w_ref[...], preferred_element_type=F32)
    outs = ((kc_ref, None, True), (vc_ref, None, False), (ks_ref, ksb_ref, True),
            (vs_ref, vsb_ref, False), (kw_ref, kwb_ref, True), (vw_ref, vwb_ref, False))
    for t, (ref, packed_ref, roped) in enumerate(outs):
        @pl.when(j == t)
        def _(ref=ref, packed_ref=packed_ref, roped=roped):
            v = _rope_tile(z, cos_ref[...], s1_ref[...], s2_ref[...]) if roped else z
            for h in range(KV_HEADS):
                ref[pl.ds(h, z.shape[0], stride=KV_HEADS), :] = v[:, h * HEAD_DIM:(h + 1) * HEAD_DIM]
            if packed_ref is not None:
                packed_ref[...] = v.astype(MM)


def _project_kv(h, w, tabs, tm):
    n, d = h.shape
    cos, s1, s2 = tabs
    tab_blocks = cos.shape[0] // tm
    tab = lambda i, j: (i % tab_blocks, 0)
    return pl.pallas_call(
        _proj_kv_kernel,
        grid=(n // tm, _KV_TILES),
        in_specs=[pl.BlockSpec((tm, d), lambda i, j: (i, 0)),
                  pl.BlockSpec((d, TN), lambda i, j: (0, j)),
                  pl.BlockSpec((tm, LANES), tab), pl.BlockSpec((tm, LANES), tab),
                  pl.BlockSpec((tm, LANES), tab)],
        out_specs=[pl.BlockSpec((tm * KV_HEADS, HEAD_DIM), lambda i, j: (i, 0))] * 6
                  + [pl.BlockSpec((tm, KV_COLS), lambda i, j: (i, 0))] * 4,
        out_shape=[jax.ShapeDtypeStruct((n * KV_HEADS, HEAD_DIM), F32)] * 6
                  + [jax.ShapeDtypeStruct((n, KV_COLS), MM)] * 4,
        compiler_params=_cparams(("parallel", "arbitrary")),
        name="kv_project",
    )(h, w, cos, s1, s2)


def _rope_tables(pos):
    half = ROPE_DIM // 2
    inv = jnp.power(jnp.float32(ROPE_THETA), -jnp.arange(0, ROPE_DIM, 2, dtype=F32) / ROPE_DIM)
    ang = pos.astype(F32)[:, None] * inv[None, :]
    cos, sin = jnp.cos(ang), jnp.sin(ang)
    r = pos.shape[0]
    one = jnp.ones((r, HEAD_DIM - ROPE_DIM), F32)
    zero = jnp.zeros((r, HEAD_DIM - ROPE_DIM), F32)
    zh = jnp.zeros((r, half), F32)
    return (jnp.concatenate([cos, cos, one], axis=1),
            jnp.concatenate([zh, sin, zero], axis=1),
            jnp.concatenate([-sin, zh, zero], axis=1))


def _layout_w_in(w_in):
    c_a = 2 * A_WIDTH + Q_COLS
    c_kv = c_a + 6 * KV_COLS
    g_br, g_ab = w_in[:, c_kv:c_kv + 3 * N_HEADS], w_in[:, c_kv + 3 * N_HEADS:]
    per_head = 3 * Q_PER_KV
    g_br = g_br.reshape(-1, KV_HEADS, per_head)
    g_br = jnp.pad(g_br, ((0, 0), (0, 0), (0, LANES - per_head))).reshape(-1, KV_HEADS * LANES)
    w_a = jnp.concatenate([w_in[:, :c_a], g_ab, g_br], axis=1).astype(MM)
    return w_a, w_in[:, c_a:c_kv].astype(MM)


def _gmlp_kernel(u_ref, v_ref, lng_ref, lnb_ref, mix_ref, bias_ref, o_ref, *maybe_vout):
    u = jax.nn.gelu(u_ref[...])
    v = jax.nn.gelu(v_ref[...])
    mu = jnp.mean(v, axis=-1, keepdims=True)
    vc = v - mu
    var = jnp.mean(vc * vc, axis=-1, keepdims=True)
    v = vc * lax.rsqrt(var + EPS) * lng_ref[...] + lnb_ref[...]
    if maybe_vout:
        maybe_vout[0][...] = v
    for g in range(A_GROUPS):
        sl = slice(g * A_GROUP_DIM, (g + 1) * A_GROUP_DIM)
        s = jnp.dot(mix_ref[g], v[:, sl].astype(MM), preferred_element_type=F32)
        b = bias_ref[g]
        s = s + jnp.concatenate([b] * (A_GROUP_DIM // LANES), axis=1)
        o_ref[:, sl] = (u[:, sl] * s).astype(o_ref.dtype)


def _gmlp(a, ln_g, ln_b, mix, bias, emit_v):
    n = a.shape[0]
    out_shape = [jax.ShapeDtypeStruct((n, A_WIDTH), MM)]
    out_specs = [pl.BlockSpec((CHUNK, A_WIDTH), lambda i: (i, 0))]
    if emit_v:
        out_shape.append(jax.ShapeDtypeStruct((n, A_WIDTH), F32))
        out_specs.append(pl.BlockSpec((CHUNK, A_WIDTH), lambda i: (i, 0)))
    const3 = lambda i: (0, 0, 0)
    return pl.pallas_call(
        _gmlp_kernel,
        grid=(n // CHUNK,),
        in_specs=[pl.BlockSpec((CHUNK, A_WIDTH), lambda i: (i, 0)),
                  pl.BlockSpec((CHUNK, A_WIDTH), lambda i: (i, 1)),
                  pl.BlockSpec((1, A_WIDTH), lambda i: (0, 0)),
                  pl.BlockSpec((1, A_WIDTH), lambda i: (0, 0)),
                  pl.BlockSpec((A_GROUPS, CHUNK, CHUNK), const3),
                  pl.BlockSpec((A_GROUPS, CHUNK, LANES), const3)],
        out_specs=out_specs,
        out_shape=out_shape,
        compiler_params=_cparams(("parallel",)),
        name="gmlp",
    )(a, a, ln_g, ln_b, mix, bias)


def _gmlp_mix(w_s, b_s, t):
    ln = min(t, CHUNK)
    causal = jnp.tril(jnp.ones((ln, ln), dtype=bool))
    w = jnp.where(causal, w_s[:, :ln, :ln], 0.0)
    reps = CHUNK // ln
    eye = jnp.eye(reps, dtype=w.dtype)
    mix = jnp.einsum('ab,gij->gaibj', eye, w).reshape(A_GROUPS, CHUNK, CHUNK)
    bias = jnp.tile(b_s[:, :ln], (1, reps))
    return mix.astype(MM), jnp.broadcast_to(bias[:, :, None], (A_GROUPS, CHUNK, LANES)).astype(F32)


def _compress_kernel(n_pages, pt_ref, *refs):
    k_pages, v_pages = refs[:n_pages], refs[n_pages:2 * n_pages]
    pek_ref, w1k_ref, w2k_ref, pev_ref, w1v_ref, w2v_ref, kc_ref, vc_ref, scr = refs[2 * n_pages:]
    blocks = n_pages * PAGE_SIZE // CMP_BLOCK
    half = blocks // 2

    page_rows = PAGE_SIZE * KV_HEADS
    blk_rows = CMP_BLOCK * KV_HEADS

    region = half * CMP_PITCH

    def one(pages, pe_ref, w1_ref, w2_ref, out_ref):
        for p in range(n_pages):
            page = pages[p][0, 0] + pe_ref[...]
            for bl in range(PAGE_SIZE // CMP_BLOCK):
                m = p * (PAGE_SIZE // CMP_BLOCK) + bl
                off = (m % 2) * region + (m // 2) * CMP_PITCH
                scr[off:off + blk_rows, :] = page[bl * blk_rows:(bl + 1) * blk_rows, :]
        acc = jnp.zeros((KV_HEADS * blocks, HEAD_DIM), F32)
        for jp in range(CMP_BLOCK // 2):
            parts = []
            for j in (2 * jp, 2 * jp + 1):
                rows = [scr[pl.ds(parity * region + j * KV_HEADS + h, half, stride=CMP_PITCH), :]
                        for h in range(KV_HEADS) for parity in (0, 1)]
                parts.append(jnp.concatenate(rows, axis=0).astype(MM))
            lhs = jnp.concatenate(parts, axis=1)
            acc = acc + jnp.dot(lhs, w1_ref[jp], preferred_element_type=F32)
        hid = jax.nn.gelu(acc)
        out = jnp.dot(hid.astype(MM), w2_ref[...], preferred_element_type=F32)
        out_ref[0] = out.reshape(KV_HEADS, blocks, HEAD_DIM)

    one(k_pages, pek_ref, w1k_ref, w2k_ref, kc_ref)
    one(v_pages, pev_ref, w1v_ref, w2v_ref, vc_ref)


def _compress(pool_k, pool_v, layer, page_table, cmp_k, cmp_v):
    n_seq, n_pages = page_table.shape
    blocks = n_pages * PAGE_SIZE // CMP_BLOCK
    page_rows = PAGE_SIZE * KV_HEADS

    def page_spec(p):
        return pl.BlockSpec((1, 1, page_rows, HEAD_DIM), lambda s, pt, p=p: (layer, pt[s, p], 0, 0))

    def prep(c):
        pe, w1, w2 = c
        pe_t = jnp.tile(jnp.repeat(pe, KV_HEADS, axis=0), (PAGE_SIZE // CMP_BLOCK, 1)).astype(F32)
        return pe_t, w1.reshape(CMP_BLOCK // 2, 2 * HEAD_DIM, HEAD_DIM).astype(MM), w2.astype(MM)

    const2 = lambda s, pt: (0, 0)
    const3 = lambda s, pt: (0, 0, 0)
    w_specs = [pl.BlockSpec((page_rows, HEAD_DIM), const2),
               pl.BlockSpec((CMP_BLOCK // 2, 2 * HEAD_DIM, HEAD_DIM), const3),
               pl.BlockSpec((HEAD_DIM, HEAD_DIM), const2)]
    out_spec = pl.BlockSpec((1, KV_HEADS, blocks, HEAD_DIM), lambda s, pt: (s, 0, 0, 0))
    out_shape = jax.ShapeDtypeStruct((n_seq, KV_HEADS, blocks, HEAD_DIM), F32)
    return pl.pallas_call(
        functools.partial(_compress_kernel, n_pages),
        grid_spec=pltpu.PrefetchScalarGridSpec(
            num_scalar_prefetch=1,
            grid=(n_seq,),
            in_specs=[page_spec(p) for p in range(n_pages)] * 2 + w_specs * 2,
            out_specs=[out_spec, out_spec],
            scratch_shapes=[pltpu.VMEM((blocks * CMP_PITCH, HEAD_DIM), F32)]),
        out_shape=[out_shape, out_shape],
        compiler_params=_cparams(("arbitrary",)),
        name="compress",
    )(page_table, *([pool_k] * n_pages), *([pool_v] * n_pages), *prep(cmp_k), *prep(cmp_v))


def _select_rows(imp, qpos, n_blocks):
    j = lax.broadcasted_iota(jnp.int32, (n_blocks, 1), 0)
    forced = (j == 0) | (j == (qpos >> _SLC_SHIFT))
    future = j * SLC_BLOCK > qpos
    score = jnp.where(future, -1.0, imp + jnp.where(forced, FORCE_BONUS, 0.0))
    rank = jnp.zeros(score.shape, jnp.int32)
    for i in range(n_blocks):
        ri = score[i:i + 1, :]
        beats = (ri > score) | ((ri == score) & (i < j))
        rank = rank + beats.astype(jnp.int32)
    return rank < min(N_SELECT, n_blocks)


def _prompt_attn_kernel(q_ref, kc_ref, vc_ref, ks_ref, vs_ref, kw_ref, vw_ref, gbr_ref, o_ref,
                        vst_scr, vwt_scr, bias_scr, acc_scr):
    n = pl.program_id(2)
    seq = ks_ref.shape[0]
    n_tiles = seq // Q_BLOCK
    n_sel_blocks = seq // SLC_BLOCK
    n_cmp = kc_ref.shape[2]
    lane_groups = [slice(g * Q_BLOCK, (g + 1) * Q_BLOCK) for g in range(Q_PER_KV)]

    tiles_per_chunk = SEL_CHUNK // Q_BLOCK

    @pl.when(n == 0)
    def _():
        for kt in range(n_tiles):
            rows = slice(kt * Q_BLOCK, (kt + 1) * Q_BLOCK)
            c, j = divmod(kt, tiles_per_chunk)
            vst_scr[c, :, j * Q_BLOCK:(j + 1) * Q_BLOCK] = vs_ref[rows, :].astype(F32).T.astype(MM)
            vwt_scr[kt] = vw_ref[rows, :].astype(F32).T.astype(MM)

    q = q_ref[...] * (SCALE * LOG2E)
    q_t = jnp.concatenate([q[:, g * HEAD_DIM:(g + 1) * HEAD_DIM].T for g in range(Q_PER_KV)],
                          axis=1).astype(MM)
    qpos = n * Q_BLOCK + lax.broadcasted_iota(jnp.int32, (1, Q_BLOCK), 1)

    s_c = jnp.dot(kc_ref[0, 0].astype(MM), q_t, preferred_element_type=F32)
    r = lax.broadcasted_iota(jnp.int32, (n_cmp, 1), 0)
    blk = jnp.where(r < n_cmp // 2, 2 * r, 2 * (r - n_cmp // 2) + 1)
    ok = ((blk + 1) * CMP_BLOCK - 1) <= qpos
    imp = jnp.zeros((n_cmp, Q_BLOCK), F32)
    probs = []
    for sl in lane_groups:
        s = jnp.where(ok, s_c[:, sl], NEG)
        e = jnp.exp2(s - jnp.max(s, axis=0, keepdims=True))
        p = jnp.where(ok, e / jnp.sum(e, axis=0, keepdims=True), 0.0)
        imp = imp + p
        probs.append(p.astype(MM))
    oc_t = jnp.dot(vc_ref[0, 0].T.astype(MM), jnp.concatenate(probs, axis=1), preferred_element_type=F32)

    sel = _select_rows(imp[:n_cmp // 2] + imp[n_cmp // 2:], qpos, n_sel_blocks)
    sel_bias = jnp.where(sel, 0.0, NEG)
    blocks_per_chunk = SEL_CHUNK // SLC_BLOCK
    key_row = lax.broadcasted_iota(jnp.int32, (SEL_CHUNK, 1), 0)
    n_chunks = lax.div(n, tiles_per_chunk) + 1
    for c in range(seq // SEL_CHUNK):
        @pl.when(c < n_chunks)
        def _(c=c):
            tile = jnp.concatenate(
                [jnp.broadcast_to(sel_bias[c * blocks_per_chunk + i:c * blocks_per_chunk + i + 1, :],
                                  (SLC_BLOCK, Q_BLOCK)) for i in range(blocks_per_chunk)], axis=0)
            bias_scr[c] = jnp.where(c * SEL_CHUNK + key_row <= qpos, tile, NEG)

    acc_scr[...] = jnp.zeros_like(acc_scr)

    def body(c, carry):
        ms, ls = carry
        start = pl.multiple_of(c * SEL_CHUNK, SEL_CHUNK)
        s_t = jnp.dot(ks_ref[pl.ds(start, SEL_CHUNK), :], q_t, preferred_element_type=F32)
        bias = bias_scr[c]
        new_ms, new_ls, ps, alphas = [], [], [], []
        for g, sl in enumerate(lane_groups):
            s = s_t[:, sl] + bias
            m_new = jnp.maximum(ms[g], jnp.max(s, axis=0, keepdims=True))
            alpha = jnp.exp2(ms[g] - m_new)
            p = jnp.exp2(s - m_new)
            new_ls.append(alpha * ls[g] + jnp.sum(p, axis=0, keepdims=True))
            new_ms.append(m_new)
            alphas.append(alpha)
            ps.append(p.astype(MM))
        pv = jnp.dot(vst_scr[c], jnp.concatenate(ps, axis=1), preferred_element_type=F32)
        for g, sl in enumerate(lane_groups):
            acc_scr[:, sl] = alphas[g] * acc_scr[:, sl] + pv[:, sl]
        return tuple(new_ms), tuple(new_ls)

    init = (tuple(jnp.full((1, Q_BLOCK), NEG, F32) for _ in lane_groups),
            tuple(jnp.zeros((1, Q_BLOCK), F32) for _ in lane_groups))
    _, ls_s = lax.fori_loop(0, n_chunks, body, init)

    win_keys = WINDOW + Q_BLOCK
    w_tile0 = jnp.maximum(n - WINDOW // Q_BLOCK, 0)
    w_start = pl.multiple_of(w_tile0 * Q_BLOCK, Q_BLOCK)
    s_w = jnp.dot(kw_ref[pl.ds(w_start, win_keys), :], q_t, preferred_element_type=F32)
    dist = qpos - (w_start + lax.broadcasted_iota(jnp.int32, (win_keys, 1), 0))
    w_bias = jnp.where((dist >= 0) & (dist < WINDOW), 0.0, NEG)
    ls_w, ps = [], []
    for sl in lane_groups:
        s = s_w[:, sl] + w_bias
        p = jnp.exp2(s - jnp.max(s, axis=0, keepdims=True))
        ls_w.append(jnp.sum(p, axis=0, keepdims=True))
        ps.append(p.astype(MM))
    p_w = jnp.concatenate(ps, axis=1)
    ow_t = functools.reduce(
        lambda a, b: a + b,
        [jnp.dot(vwt_scr[w_tile0 + j], p_w[j * Q_BLOCK:(j + 1) * Q_BLOCK, :], preferred_element_type=F32)
         for j in range(win_keys // Q_BLOCK)])

    gate_t = jax.nn.sigmoid(gbr_ref[...]).T
    for g, sl in enumerate(lane_groups):
        o_t = (oc_t[:, sl] * gate_t[3 * g:3 * g + 1, :]
               + (acc_scr[:, sl] / ls_s[g]) * gate_t[3 * g + 1:3 * g + 2, :]
               + (ow_t[:, sl] / ls_w[g]) * gate_t[3 * g + 2:3 * g + 3, :])
        o_ref[:, g * HEAD_DIM:(g + 1) * HEAD_DIM] = o_t.T.astype(o_ref.dtype)


def _prompt_attention(a, kc, vc, ks, vs, kw, vw, gt, batch, seq):
    n_qb = seq // Q_BLOCK
    q_col0 = 2 * A_WIDTH // (Q_PER_KV * HEAD_DIM)
    gbr_col0 = 2 * D_MODEL // LANES
    cmp_spec = pl.BlockSpec((1, 1, kc.shape[2], HEAD_DIM), lambda b, h, n: (b, h, 0, 0))
    kv_spec = pl.BlockSpec((seq, HEAD_DIM), lambda b, h, n: (b, h))
    assert seq % SEL_CHUNK == 0 and seq >= WINDOW + Q_BLOCK
    scratch = [pltpu.VMEM((seq // SEL_CHUNK, HEAD_DIM, SEL_CHUNK), MM),
               pltpu.VMEM((n_qb, HEAD_DIM, Q_BLOCK), MM),
               pltpu.VMEM((seq // SEL_CHUNK, SEL_CHUNK, Q_BLOCK), F32),
               pltpu.VMEM((HEAD_DIM, Q_PER_KV * Q_BLOCK), F32)]
    return pl.pallas_call(
        _prompt_attn_kernel,
        grid=(batch, KV_HEADS, n_qb),
        in_specs=[pl.BlockSpec((Q_BLOCK, Q_PER_KV * HEAD_DIM), lambda b, h, n: (b * n_qb + n, q_col0 + h)),
                  cmp_spec, cmp_spec, kv_spec, kv_spec, kv_spec, kv_spec,
                  pl.BlockSpec((Q_BLOCK, LANES), lambda b, h, n: (b * n_qb + n, gbr_col0 + h))],
        out_specs=pl.BlockSpec((Q_BLOCK, Q_PER_KV * HEAD_DIM), lambda b, h, n: (b * n_qb + n, h)),
        out_shape=jax.ShapeDtypeStruct((batch * seq, Q_COLS), MM),
        scratch_shapes=scratch,
        compiler_params=_cparams(("parallel", "parallel", "arbitrary")),
        name="prompt_attention",
    )(a, kc, vc, ks, vs, kw, vw, gt)


def _dot_nt(a, b):
    return lax.dot_general(a, b, (((1,), (1,)), ((), ())), preferred_element_type=F32)


def _select_cols(imp, qpos, n_blocks):
    j = lax.broadcasted_iota(jnp.int32, (1, n_blocks), 1)
    forced = (j == 0) | (j == (qpos >> _SLC_SHIFT))
    future = j * SLC_BLOCK > qpos
    score = jnp.where(future, -1.0, imp + jnp.where(forced, FORCE_BONUS, 0.0))
    rank = jnp.zeros(score.shape, jnp.int32)
    for i in range(n_blocks):
        ci = score[:, i:i + 1]
        beats = (ci > score) | ((ci == score) & (i < j))
        rank = rank + beats.astype(jnp.int32)
    return (rank < min(N_SELECT, n_blocks)).astype(F32)


def _softmax_parts(parts):
    m = functools.reduce(jnp.maximum, [jnp.max(s, axis=1, keepdims=True) for s, _ in parts])
    es = [jnp.exp2(s - m) for s, _ in parts]
    den = functools.reduce(lambda a, b: a + b, [jnp.sum(e, axis=1, keepdims=True) for e in es])
    acc = None
    for e, (_, v) in zip(es, parts):
        o = jnp.dot((e / den).astype(MM), v, preferred_element_type=F32)
        acc = o if acc is None else acc + o
    return acc


def _sample_attn_kernel(n_pages, pt_ref, *refs):
    ks_pages, vs_pages = refs[:n_pages], refs[n_pages:2 * n_pages]
    (q_ref, kc_ref, vc_ref, kbuf_ref, vbuf_ref, ksn_ref, vsn_ref, kwn_ref, vwn_ref, gbr_ref,
     o_ref, kwo_ref, vwo_ref) = refs[2 * n_pages:]
    t = q_ref.shape[1]
    past = n_pages * PAGE_SIZE
    nb_past = past // SLC_BLOCK
    buf_rows = kbuf_ref.shape[2]
    wb = buf_rows // KV_HEADS
    new_rows = t * KV_HEADS
    hq = KV_HEADS * t
    n_cmp = kc_ref.shape[2]
    head_shift, t_shift, cmp_shift = (v.bit_length() - 1 for v in (KV_HEADS, t, n_cmp))
    assert (KV_HEADS, t, n_cmp) == (1 << head_shift, 1 << t_shift, 1 << cmp_shift) and LANES % KV_HEADS == 0

    q = q_ref[0] * (SCALE * LOG2E)
    qb = jnp.concatenate([q[:, (h * Q_PER_KV + g) * HEAD_DIM:(h * Q_PER_KV + g + 1) * HEAD_DIM]
                          for g in range(Q_PER_KV) for h in range(KV_HEADS)], axis=0).astype(MM)
    ri = lax.broadcasted_iota(jnp.int32, (Q_PER_KV * hq, 1), 0)
    row_h = (ri >> t_shift) & (KV_HEADS - 1)
    row_t = ri & (t - 1)
    qpos = past + row_t

    def key_cols(n):
        c = lax.broadcasted_iota(jnp.int32, (1, n), 1)
        return c & (KV_HEADS - 1), c >> head_shift

    kwo_ref[0, 0, :buf_rows - new_rows] = kbuf_ref[0, 0, new_rows:]
    kwo_ref[0, 0, buf_rows - new_rows:] = kwn_ref[0]
    vwo_ref[0, 0, :buf_rows - new_rows] = vbuf_ref[0, 0, new_rows:]
    vwo_ref[0, 0, buf_rows - new_rows:] = vwn_ref[0]

    kc = kc_ref[0].reshape(KV_HEADS * n_cmp, HEAD_DIM).astype(MM)
    vc = vc_ref[0].reshape(KV_HEADS * n_cmp, HEAD_DIM).astype(MM)
    col = lax.broadcasted_iota(jnp.int32, (1, KV_HEADS * n_cmp), 1)
    slot = col & (n_cmp - 1)
    blk = jnp.where(slot < n_cmp // 2, 2 * slot, 2 * (slot - n_cmp // 2) + 1)
    ok = ((col >> cmp_shift) == row_h) & (((blk + 1) * CMP_BLOCK - 1) <= qpos)
    s = jnp.where(ok, _dot_nt(qb, kc), NEG)
    e = jnp.exp2(s - jnp.max(s, axis=1, keepdims=True))
    p = jnp.where(ok, e / jnp.sum(e, axis=1, keepdims=True), 0.0)
    o_c = jnp.dot(p.astype(MM), vc, preferred_element_type=F32)
    p = functools.reduce(lambda a, b: a + b, [p[:, h * n_cmp:(h + 1) * n_cmp] for h in range(KV_HEADS)])
    p = functools.reduce(lambda a, b: a + b, [p[g * hq:(g + 1) * hq] for g in range(Q_PER_KV)])
    imp = jnp.concatenate([p[:, :n_cmp // 2] + p[:, n_cmp // 2:], jnp.zeros((hq, 1), F32)], axis=1)
    sel = _select_cols(imp, qpos[:hq], nb_past + 1)
    sel = jnp.concatenate([sel] * Q_PER_KV, axis=0)

    page_h, page_r = key_cols(PAGE_SIZE * KV_HEADS)
    own_head = page_h == row_h
    page_blk = page_r >> _SLC_SHIFT
    per_page = PAGE_SIZE // SLC_BLOCK
    parts = []
    for pg in range(n_pages):
        ok = own_head & functools.reduce(
            lambda a, b: a | b,
            [(page_blk == c) & (sel[:, pg * per_page + c:pg * per_page + c + 1] > 0.5) for c in range(per_page)])
        parts.append((jnp.where(ok, _dot_nt(qb, ks_pages[pg][0, 0].astype(MM)), NEG),
                      vs_pages[pg][0, 0].astype(MM)))
    new_h, new_t = key_cols(new_rows)
    new_ok = (new_h == row_h) & (new_t <= row_t)
    parts.append((jnp.where(new_ok & (sel[:, nb_past:nb_past + 1] > 0.5),
                            _dot_nt(qb, ksn_ref[0].astype(MM)), NEG), vsn_ref[0].astype(MM)))
    o_s = _softmax_parts(parts)

    buf_h, buf_r = key_cols(buf_rows)
    buf_pos = past - wb + buf_r
    dist = qpos - buf_pos
    buf_ok = (buf_h == row_h) & (dist >= 0) & (dist < WINDOW) & (buf_pos >= 0)
    o_w = _softmax_parts([
        (jnp.where(buf_ok, _dot_nt(qb, kbuf_ref[0, 0].astype(MM)), NEG), vbuf_ref[0, 0].astype(MM)),
        (jnp.where(new_ok, _dot_nt(qb, kwn_ref[0].astype(MM)), NEG), vwn_ref[0].astype(MM))])

    gate = jax.nn.sigmoid(gbr_ref[0])
    for g in range(Q_PER_KV):
        for h in range(KV_HEADS):
            rs = slice(g * hq + h * t, g * hq + (h + 1) * t)
            c0 = h * LANES + 3 * g
            o = (o_c[rs] * gate[:, c0:c0 + 1] + o_s[rs] * gate[:, c0 + 1:c0 + 2]
                 + o_w[rs] * gate[:, c0 + 2:c0 + 3])
            col0 = (h * Q_PER_KV + g) * HEAD_DIM
            o_ref[0, :, col0:col0 + HEAD_DIM] = o.astype(o_ref.dtype)


def _sample_attention(a3, kc, vc, pool_ks, pool_vs, layer, page_table, kbuf, vbuf, ksn, vsn, kwn, vwn, gt3):
    db, t, _ = a3.shape
    n_pages = page_table.shape[1]
    buf_rows = kbuf.shape[2]
    page_rows = PAGE_SIZE * KV_HEADS

    def page_spec(p):
        return pl.BlockSpec((1, 1, page_rows, HEAD_DIM), lambda s, pt, p=p: (layer, pt[s, p], 0, 0))

    row3 = lambda s, pt: (s, 0, 0)
    cmp_spec = pl.BlockSpec((1, KV_HEADS, kc.shape[2], HEAD_DIM), lambda s, pt: (s, 0, 0, 0))
    buf_spec = pl.BlockSpec((1, 1, buf_rows, HEAD_DIM), lambda s, pt: (layer, s, 0, 0))
    out_buf_spec = pl.BlockSpec((1, 1, buf_rows, HEAD_DIM), lambda s, pt: (0, s, 0, 0))
    new_spec = pl.BlockSpec((1, t * KV_HEADS, HEAD_DIM), row3)
    buf_shape = jax.ShapeDtypeStruct((1, db, buf_rows, HEAD_DIM), F32)
    return pl.pallas_call(
        functools.partial(_sample_attn_kernel, n_pages),
        grid_spec=pltpu.PrefetchScalarGridSpec(
            num_scalar_prefetch=1,
            grid=(db,),
            in_specs=[page_spec(p) for p in range(n_pages)] * 2
                     + [pl.BlockSpec((1, t, Q_COLS), lambda s, pt: (s, 0, 2 * A_WIDTH // Q_COLS)),
                        cmp_spec, cmp_spec, buf_spec, buf_spec, new_spec, new_spec, new_spec, new_spec,
                        pl.BlockSpec((1, t, KV_HEADS * LANES),
                                     lambda s, pt: (s, 0, 2 * D_MODEL // (KV_HEADS * LANES)))],
            out_specs=[pl.BlockSpec((1, t, Q_COLS), row3), out_buf_spec, out_buf_spec]),
        out_shape=[jax.ShapeDtypeStruct((db, t, Q_COLS), MM), buf_shape, buf_shape],
        compiler_params=_cparams(("arbitrary",)),
        name="sample_attention",
    )(page_table, *([pool_ks] * n_pages), *([pool_vs] * n_pages),
      a3, kc, vc, kbuf, vbuf, ksn, vsn, kwn, vwn, gt3)


def _mix_kernel(a_ref, o_ref, wpa_ref, wpb_ref, ga_ref, gb_ref, m_ref):
    y_a = jnp.dot(a_ref[...], wpa_ref[...], preferred_element_type=F32)
    y_b = jnp.dot(o_ref[...], wpb_ref[...], preferred_element_type=F32)
    m = jax.nn.sigmoid(ga_ref[...]) * y_a + jax.nn.sigmoid(gb_ref[...]) * y_b
    m_ref[...] = m.astype(m_ref.dtype)


def _mix(a_out, o, w_pa, w_pb, gt, tm):
    n = a_out.shape[0]
    gb0 = D_MODEL // TN
    return pl.pallas_call(
        _mix_kernel,
        grid=(n // tm, D_MODEL // TN),
        in_specs=[pl.BlockSpec((tm, A_WIDTH), lambda i, j: (i, 0)),
                  pl.BlockSpec((tm, Q_COLS), lambda i, j: (i, 0)),
                  pl.BlockSpec((A_WIDTH, TN), lambda i, j: (0, j)),
                  pl.BlockSpec((Q_COLS, TN), lambda i, j: (0, j)),
                  pl.BlockSpec((tm, TN), lambda i, j: (i, j)),
                  pl.BlockSpec((tm, TN), lambda i, j: (i, gb0 + j))],
        out_specs=pl.BlockSpec((tm, TN), lambda i, j: (i, j)),
        out_shape=jax.ShapeDtypeStruct((n, D_MODEL), MM),
        compiler_params=_cparams(("parallel", "arbitrary")),
        name="merge_gate",
    )(a_out, o, w_pa, w_pb, gt, gt)


def _out_proj_kernel(m_ref, w_ref, x_ref, o_ref):
    o_ref[...] = x_ref[...] + jnp.dot(m_ref[...], w_ref[...], preferred_element_type=F32)


def _out_proj(m, w_o, x2d, tm):
    n = m.shape[0]
    return pl.pallas_call(
        _out_proj_kernel,
        grid=(n // tm, D_MODEL // TN),
        in_specs=[pl.BlockSpec((tm, D_MODEL), lambda i, j: (i, 0)),
                  pl.BlockSpec((D_MODEL, TN), lambda i, j: (0, j)),
                  pl.BlockSpec((tm, TN), lambda i, j: (i, j))],
        out_specs=pl.BlockSpec((tm, TN), lambda i, j: (i, j)),
        out_shape=jax.ShapeDtypeStruct((n, D_MODEL), F32),
        compiler_params=_cparams(("parallel", "arbitrary")),
        name="out_proj",
    )(m, w_o, x2d)


def _normed_logits(x, gf_ref, wr_hi_ref, wr_lo_ref, br_ref):
    ms = jnp.mean(x * x, axis=-1, keepdims=True)
    h = x * lax.rsqrt(ms + EPS) * gf_ref[...]
    hi = h.astype(MM)
    lo = (h - hi.astype(F32)).astype(MM)
    logits = (jnp.dot(hi, wr_hi_ref[...], preferred_element_type=F32)
              + (jnp.dot(hi, wr_lo_ref[...], preferred_element_type=F32)
                 + jnp.dot(lo, wr_hi_ref[...], preferred_element_type=F32))) + br_ref[...]
    return hi, logits


def _route(logits):
    lane = lax.broadcasted_iota(jnp.int32, logits.shape, 1)
    big = jnp.int32(LANES)
    is_g = lane < N_GROUPS
    gl = jnp.where(is_g, logits, -jnp.inf)
    gm = jnp.max(gl, axis=1, keepdims=True)
    ge = jnp.exp(gl - gm)
    pg = ge / jnp.sum(ge, axis=1, keepdims=True)
    pg_top = jnp.max(pg, axis=1, keepdims=True)
    g_idx = jnp.min(jnp.where(is_g & (pg == pg_top), lane, big), axis=1, keepdims=True)
    in_grp = (lane >= N_GROUPS) & (((lane - N_GROUPS) >> 3) == g_idx)
    el = jnp.where(in_grp, logits, -jnp.inf)
    em = jnp.max(el, axis=1, keepdims=True)
    ee = jnp.exp(el - em)
    pe = ee / jnp.sum(ee, axis=1, keepdims=True)
    p1 = jnp.max(pe, axis=1, keepdims=True)
    i1 = jnp.min(jnp.where(in_grp & (pe == p1), lane, big), axis=1, keepdims=True)
    rest = in_grp & (lane != i1)
    p2 = jnp.max(jnp.where(rest, pe, -1.0), axis=1, keepdims=True)
    i2 = jnp.min(jnp.where(rest & (pe == p2), lane, big), axis=1, keepdims=True)
    tot = p1 + p2
    return i1, i2, pg_top * p1 / tot, pg_top * p2 / tot


def _moe_kernel(final_norm, x_ref, gf_ref, wr_hi_ref, wr_lo_ref, br_ref, wg_ref, wu_ref, wd_ref, gfin_ref,
                y_ref, h_scr, gate_scr, acc_scr):
    e = pl.program_id(1)

    @pl.when(e == 0)
    def _():
        hi, logits = _normed_logits(x_ref[...], gf_ref, wr_hi_ref, wr_lo_ref, br_ref)
        i1, i2, w1, w2 = _route(logits)
        lane = lax.broadcasted_iota(jnp.int32, logits.shape, 1)
        h_scr[...] = hi
        gate_scr[...] = jnp.where(lane == i1, w1, 0.0) + jnp.where(lane == i2, w2, 0.0)
        acc_scr[...] = jnp.zeros_like(acc_scr)

    h = h_scr[...]
    lane = lax.broadcasted_iota(jnp.int32, gate_scr.shape, 1)
    gcol = jnp.sum(jnp.where(lane == e + N_GROUPS, gate_scr[...], 0.0), axis=1, keepdims=True)
    hid = (jax.nn.silu(jnp.dot(h, wg_ref[0], preferred_element_type=F32))
           * jnp.dot(h, wu_ref[0], preferred_element_type=F32))
    acc_scr[...] += jnp.dot((hid * gcol).astype(MM), wd_ref[0], preferred_element_type=F32)

    @pl.when(e == pl.num_programs(1) - 1)
    def _():
        x2 = x_ref[...] + acc_scr[...]
        if final_norm:
            ms = jnp.mean(x2 * x2, axis=-1, keepdims=True)
            x2 = x2 * lax.rsqrt(ms + EPS) * gfin_ref[...]
        y_ref[...] = x2


def _moe(x1, g_ffn, wr_hi, wr_lo, b_r, w_gate, w_up, w_down, g_final, tm, final_norm):
    n = x1.shape[0]
    row = lambda i, e: (i, 0)
    const = lambda i, e: (0, 0)
    return pl.pallas_call(
        functools.partial(_moe_kernel, final_norm),
        grid=(n // tm, N_EXPERTS),
        in_specs=[pl.BlockSpec((tm, D_MODEL), row),
                  pl.BlockSpec((1, D_MODEL), const),
                  pl.BlockSpec((D_MODEL, LANES), const),
                  pl.BlockSpec((D_MODEL, LANES), const),
                  pl.BlockSpec((1, LANES), const),
                  pl.BlockSpec((1, D_MODEL, EXPERT_DIM), lambda i, e: (e, 0, 0)),
                  pl.BlockSpec((1, D_MODEL, EXPERT_DIM), lambda i, e: (e, 0, 0)),
                  pl.BlockSpec((1, EXPERT_DIM, D_MODEL), lambda i, e: (e, 0, 0)),
                  pl.BlockSpec((1, D_MODEL), const)],
        out_specs=pl.BlockSpec((tm, D_MODEL), row),
        out_shape=jax.ShapeDtypeStruct((n, D_MODEL), F32),
        scratch_shapes=[pltpu.VMEM((tm, D_MODEL), MM), pltpu.VMEM((tm, LANES), F32),
                        pltpu.VMEM((tm, D_MODEL), F32)],
        compiler_params=_cparams(("parallel", "arbitrary")),
        name="moe_final_norm",
    )(x1, g_ffn, wr_hi, wr_lo, b_r, w_gate, w_up, w_down, g_final)


def _router_weights(w_rg, b_rg, w_re, b_re):
    w = jnp.concatenate([w_rg, w_re], axis=1)
    w = jnp.pad(w, ((0, 0), (0, LANES - w.shape[1])))
    b = jnp.pad(jnp.concatenate([b_rg, b_re]), (0, LANES - N_GROUPS - N_EXPERTS))[None, :]
    hi = w.astype(MM)
    lo = (w - hi.astype(F32)).astype(MM)
    return hi, lo, b.astype(F32)


def _router_kernel(x_ref, gf_ref, wr_hi_ref, wr_lo_ref, br_ref, tril_ref, meta_ref, cnt_ref, base_scr):
    @pl.when(pl.program_id(0) == 0)
    def _():
        base_scr[...] = jnp.zeros_like(base_scr)

    _, logits = _normed_logits(x_ref[...], gf_ref, wr_hi_ref, wr_lo_ref, br_ref)
    i1, i2, w1, w2 = _route(logits)
    lane = lax.broadcasted_iota(jnp.int32, logits.shape, 1)
    hit1, hit2 = lane == i1, lane == i2
    chosen = jnp.where(hit1 | hit2, 1.0, 0.0)
    before = jnp.dot(tril_ref[...], chosen.astype(MM), preferred_element_type=F32) + base_scr[...]
    r1 = jnp.sum(jnp.where(hit1, before, 0.0), axis=1, keepdims=True)
    r2 = jnp.sum(jnp.where(hit2, before, 0.0), axis=1, keepdims=True)
    base_scr[...] += jnp.sum(chosen, axis=0, keepdims=True)
    cols = ((i1 - N_GROUPS).astype(F32), (i2 - N_GROUPS).astype(F32), w1, w2, r1, r2)
    meta_ref[...] = functools.reduce(lambda a, b: a + b,
                                     [jnp.where(lane == k, c, 0.0) for k, c in enumerate(cols)])
    cnt_ref[...] = base_scr[...]


def _router(x1, g_ffn, wr_hi, wr_lo, b_r, tm):
    n = x1.shape[0]
    const = lambda i: (0, 0)
    tril = jnp.tril(jnp.ones((tm, tm), MM), -1)
    return pl.pallas_call(
        _router_kernel,
        grid=(n // tm,),
        in_specs=[pl.BlockSpec((tm, D_MODEL), lambda i: (i, 0)),
                  pl.BlockSpec((1, D_MODEL), const),
                  pl.BlockSpec((D_MODEL, LANES), const),
                  pl.BlockSpec((D_MODEL, LANES), const),
                  pl.BlockSpec((1, LANES), const),
                  pl.BlockSpec((tm, tm), const)],
        out_specs=[pl.BlockSpec((tm, LANES), lambda i: (i, 0)), pl.BlockSpec((1, LANES), const)],
        out_shape=[jax.ShapeDtypeStruct((n, LANES), F32), jax.ShapeDtypeStruct((1, LANES), F32)],
        scratch_shapes=[pltpu.VMEM((1, LANES), F32)],
        compiler_params=_cparams(("arbitrary",)),
        name="moe_router",
    )(x1, g_ffn, wr_hi, wr_lo, b_r, tril)


def _dispatch_kernel(fill_lo_ref, fill_hi_ref, x_ref, s1_ref, s2_ref, xs_ref, sem):
    tb = x_ref.shape[0]

    def row_copy(r, slot):
        return pltpu.make_async_copy(x_ref.at[pl.ds(r, 1)], xs_ref.at[pl.ds(slot, 1)], sem)

    def start(r, c):
        row_copy(r, s1_ref[0, 0, r]).start()
        row_copy(r, s2_ref[0, 0, r]).start()
        return c

    def wait(r, c):
        row_copy(r, s1_ref[0, 0, r]).wait()
        row_copy(r, s2_ref[0, 0, r]).wait()
        return c

    lax.fori_loop(0, tb, start, 0)

    @pl.when(pl.program_id(0) == 0)
    def _():
        def fill_start(s, c):
            row_copy(0, s).start()
            return c

        def fill_wait(s, c):
            row_copy(0, s).wait()
            return c

        for e in range(N_EXPERTS):
            lax.fori_loop(fill_lo_ref[e], fill_hi_ref[e], fill_start, 0)
        for e in range(N_EXPERTS):
            lax.fori_loop(fill_lo_ref[e], fill_hi_ref[e], fill_wait, 0)

    lax.fori_loop(0, tb, wait, 0)


def _dispatch(x1, slot1, slot2, fill_lo, fill_hi, n_slots, tb):
    n, d = x1.shape
    smem_row = pl.BlockSpec((1, 1, tb), lambda i, lo, hi: (i, 0, 0), memory_space=pltpu.SMEM)
    return pl.pallas_call(
        _dispatch_kernel,
        grid_spec=pltpu.PrefetchScalarGridSpec(
            num_scalar_prefetch=2,
            grid=(n // tb,),
            in_specs=[pl.BlockSpec((tb, d), lambda i, lo, hi: (i, 0)), smem_row, smem_row],
            out_specs=pl.BlockSpec(memory_space=pl.ANY),
            scratch_shapes=[pltpu.SemaphoreType.DMA(())]),
        out_shape=jax.ShapeDtypeStruct((n_slots, d), F32),
        compiler_params=_cparams(("arbitrary",)),
        name="moe_dispatch",
    )(fill_lo, fill_hi, x1, slot1.reshape(n // tb, 1, tb), slot2.reshape(n // tb, 1, tb))


def _expert_kernel(te_ref, tv_ref, xs_ref, gf_ref, wg_ref, wu_ref, wd_ref, ys_ref):
    used = tv_ref[pl.program_id(0)] == 1

    @pl.when(jnp.logical_not(used))
    def _():
        ys_ref[...] = jnp.zeros_like(ys_ref)

    @pl.when(used)
    def _():
        x = xs_ref[...]
        ms = jnp.mean(x * x, axis=-1, keepdims=True)
        h = (x * lax.rsqrt(ms + EPS) * gf_ref[...]).astype(MM)
        hid = (jax.nn.silu(jnp.dot(h, wg_ref[0], preferred_element_type=F32))
               * jnp.dot(h, wu_ref[0], preferred_element_type=F32))
        ys_ref[...] = jnp.dot(hid.astype(MM), wd_ref[0], preferred_element_type=F32)


def _experts(xs, tile_expert, tile_valid, g_ffn, w_gate, w_up, w_down):
    n_slots, d = xs.shape
    tile = lambda i, te, tv: (i, 0)
    weight = lambda i, te, tv: (te[i], 0, 0)
    return pl.pallas_call(
        _expert_kernel,
        grid_spec=pltpu.PrefetchScalarGridSpec(
            num_scalar_prefetch=2,
            grid=(tile_expert.shape[0],),
            in_specs=[pl.BlockSpec((EXPERT_TILE, d), tile),
                      pl.BlockSpec((1, d), lambda i, te, tv: (0, 0)),
                      pl.BlockSpec((1, d, EXPERT_DIM), weight),
                      pl.BlockSpec((1, d, EXPERT_DIM), weight),
                      pl.BlockSpec((1, EXPERT_DIM, d), weight)],
            out_specs=pl.BlockSpec((EXPERT_TILE, d), tile)),
        out_shape=jax.ShapeDtypeStruct((n_slots, d), F32),
        compiler_params=_cparams(("arbitrary",)),
        name="moe_experts",
    )(tile_expert, tile_valid, xs, g_ffn, w_gate, w_up, w_down)


def _combine_kernel(final_norm, x_ref, meta_ref, s1_ref, s2_ref, ys_ref, gfin_ref, y_ref, buf1, buf2, sem):
    tb = x_ref.shape[0]

    def row_copy(buf, r, slot):
        return pltpu.make_async_copy(ys_ref.at[pl.ds(slot, 1)], buf.at[pl.ds(r, 1)], sem)

    def start(r, c):
        row_copy(buf1, r, s1_ref[0, 0, r]).start()
        row_copy(buf2, r, s2_ref[0, 0, r]).start()
        return c

    def wait(r, c):
        row_copy(buf1, r, s1_ref[0, 0, r]).wait()
        row_copy(buf2, r, s2_ref[0, 0, r]).wait()
        return c

    lax.fori_loop(0, tb, start, 0)
    lax.fori_loop(0, tb, wait, 0)
    meta = meta_ref[...]
    x2 = x_ref[...] + (meta[:, 2:3] * buf1[...] + meta[:, 3:4] * buf2[...])
    if final_norm:
        ms = jnp.mean(x2 * x2, axis=-1, keepdims=True)
        x2 = x2 * lax.rsqrt(ms + EPS) * gfin_ref[...]
    y_ref[...] = x2


def _combine(x1, meta, slot1, slot2, ys, g_final, tb, final_norm):
    n, d = x1.shape
    smem_row = pl.BlockSpec((1, 1, tb), lambda i: (i, 0, 0), memory_space=pltpu.SMEM)
    return pl.pallas_call(
        functools.partial(_combine_kernel, final_norm),
        grid=(n // tb,),
        in_specs=[pl.BlockSpec((tb, d), lambda i: (i, 0)),
                  pl.BlockSpec((tb, LANES), lambda i: (i, 0)),
                  smem_row, smem_row,
                  pl.BlockSpec(memory_space=pl.ANY),
                  pl.BlockSpec((1, d), lambda i: (0, 0))],
        out_specs=pl.BlockSpec((tb, d), lambda i: (i, 0)),
        out_shape=jax.ShapeDtypeStruct((n, d), F32),
        scratch_shapes=[pltpu.VMEM((tb, d), F32), pltpu.VMEM((tb, d), F32), pltpu.SemaphoreType.DMA(())],
        compiler_params=_cparams(("arbitrary",)),
        name="moe_combine_norm",
    )(x1, meta, slot1.reshape(n // tb, 1, tb), slot2.reshape(n // tb, 1, tb), ys, g_final)


def _routed_moe(x1, g_ffn, wr_hi, wr_lo, b_r, w_gate, w_up, w_down, g_final, final_norm):
    n = x1.shape[0]
    meta, cnt = _router(x1, g_ffn, wr_hi, wr_lo, b_r, _row_tile(n))
    e1, e2, r1, r2 = (meta[:, k].astype(jnp.int32) for k in (0, 1, 4, 5))
    counts = cnt[0, N_GROUPS:N_GROUPS + N_EXPERTS].astype(jnp.int32)
    padded = (counts + EXPERT_TILE - 1) // EXPERT_TILE * EXPERT_TILE
    ends = jnp.cumsum(padded)
    offs = ends - padded
    slot1, slot2 = offs[e1] + r1, offs[e2] + r2
    n_tiles = 2 * n // EXPERT_TILE + N_EXPERTS
    n_slots = n_tiles * EXPERT_TILE
    tile_start = jnp.arange(n_tiles, dtype=jnp.int32) * EXPERT_TILE
    tile_expert = jnp.minimum(jnp.searchsorted(ends, tile_start, side='right'), N_EXPERTS - 1).astype(jnp.int32)
    tile_valid = (tile_start < ends[-1]).astype(jnp.int32)
    fill_hi = ends.at[N_EXPERTS - 1].set(n_slots)
    xs = _dispatch(x1, slot1, slot2, offs + counts, fill_hi, n_slots, _row_tile(n))
    ys = _experts(xs, tile_expert, tile_valid, g_ffn, w_gate, w_up, w_down)
    return _combine(x1, meta, slot1, slot2, ys, g_final, _row_tile(n, 256), final_norm)


def _row_tile(n, tm=512):
    return tm if n % tm == 0 else n


def kernel(x_prompt, x_sample, cache_k_cmp, cache_v_cmp, cache_k_slc, cache_v_slc, state_k_win, state_v_win, page_table, g_mix, w_in, a_ln_g, a_ln_b, a_w_s, a_b_s, cmp_pe_k, cmp_w1_k, cmp_w2_k, cmp_pe_v, cmp_w1_v, cmp_w2_v, w_pa, w_pb, w_o, g_ffn, w_rg, b_rg, w_re, b_re, w_gate, w_up, w_down, g_final):
    depth = g_mix.shape[0]
    b, s, d = x_prompt.shape
    db, t, _ = x_sample.shape
    n_pages = page_table.shape[1]
    past = n_pages * PAGE_SIZE
    wl = min(WINDOW, s)
    assert s % Q_BLOCK == 0 and CHUNK % t == 0 and (db * t) % CHUNK == 0 and t < CMP_BLOCK

    tabs_p = _rope_tables(jnp.arange(s))
    tabs_s = _rope_tables(past + (jnp.arange(db * t) % t))
    prompt_pages = jnp.arange(b * s // PAGE_SIZE, dtype=jnp.int32).reshape(b, s // PAGE_SIZE)
    g_final2 = g_final[None, :]

    xp = x_prompt.reshape(b * s, d)
    xs = x_sample.reshape(db * t, d)
    outs_p = [[] for _ in range(6)]
    outs_s = [[] for _ in range(7)]
    for l in range(depth):
        w_a, w_kv = _layout_w_in(w_in[l])
        g_l = g_mix[l][None, :]
        ln_g, ln_b = a_ln_g[l][None, :], a_ln_b[l][None, :]
        cmp_k = (cmp_pe_k[l], cmp_w1_k[l], cmp_w2_k[l])
        cmp_v = (cmp_pe_v[l], cmp_w1_v[l], cmp_w2_v[l])
        wpa, wpb, wo = w_pa[l].astype(MM), w_pb[l].astype(MM), w_o[l].astype(MM)
        wr_hi, wr_lo, b_r = _router_weights(w_rg[l], b_rg[l], w_re[l], b_re[l])
        wg, wu, wd = w_gate[l].astype(MM), w_up[l].astype(MM), w_down[l].astype(MM)
        last = l == depth - 1

        def tail(x2d, a_out, o, gt):
            n = x2d.shape[0]
            m = _mix(a_out, o, wpa, wpb, gt, _row_tile(n, 1024))
            x1 = _out_proj(m, wo, x2d, _row_tile(n, 1024))
            moe_args = (x1, g_ffn[l][None, :], wr_hi, wr_lo, b_r, wg, wu, wd, g_final2)
            if n >= ROUTED_MIN_TOKENS:
                return _routed_moe(*moe_args, last)
            return _moe(*moe_args, _row_tile(n), last)

        a, gt, h = _project_a(xp, g_l, w_a, tabs_p, _row_tile(b * s, 1024))
        kc_r, vc_r, ks_r, vs_r, kw_r, vw_r, ks_b, vs_b, kw_b, vw_b = _project_kv(
            h, w_kv, tabs_p, _row_tile(b * s))
        mix_p, bias_p = _gmlp_mix(a_w_s[l], a_b_s[l], s)
        (a_out,) = _gmlp(a, ln_g, ln_b, mix_p, bias_p, False)
        pages = lambda r: r.reshape(1, b * s // PAGE_SIZE, PAGE_SIZE * KV_HEADS, HEAD_DIM)
        kc, vc = _compress(pages(kc_r), pages(vc_r), 0, prompt_pages, cmp_k, cmp_v)
        o = _prompt_attention(a, kc, vc, ks_b, vs_b, kw_b, vw_b, gt, b, s)
        xp = tail(xp, a_out, o, gt)
        heads = lambda r: r.reshape(b, s, KV_HEADS, HEAD_DIM)
        for lst, r in zip(outs_p, (kc_r, vc_r, ks_r, vs_r)):
            lst.append(heads(r))
        outs_p[4].append(heads(kw_r)[:, -wl:])
        outs_p[5].append(heads(vw_r)[:, -wl:])

        a, gt, h = _project_a(xs, g_l, w_a, tabs_s, _row_tile(db * t))
        kc_r, vc_r, ks_r, vs_r, kw_r, vw_r, _, _, _, _ = _project_kv(h, w_kv, tabs_s, _row_tile(db * t))
        mix_s, bias_s = _gmlp_mix(a_w_s[l], a_b_s[l], t)
        a_out, v_new = _gmlp(a, ln_g, ln_b, mix_s, bias_s, True)
        flat = lambda c: c.reshape(c.shape[0], c.shape[1], c.shape[2] * KV_HEADS, HEAD_DIM)
        kc, vc = _compress(flat(cache_k_cmp), flat(cache_v_cmp), l, page_table, cmp_k, cmp_v)
        r3 = lambda r: r.reshape(db, t, r.shape[-1])
        new = lambda r: r.reshape(db, t * KV_HEADS, HEAD_DIM)
        o, kwin, vwin = _sample_attention(
            r3(a), kc, vc, flat(cache_k_slc), flat(cache_v_slc), l, page_table,
            flat(state_k_win), flat(state_v_win), new(ks_r), new(vs_r), new(kw_r), new(vw_r), r3(gt))
        xs = tail(xs, a_out, o.reshape(db * t, Q_COLS), gt)
        r4 = lambda r: r.reshape(db, -1, KV_HEADS, HEAD_DIM)
        for lst, r in zip(outs_s, (kc_r, vc_r, ks_r, vs_r, kwin, vwin)):
            lst.append(r4(r))
        outs_s[6].append(v_new.reshape(db, t, A_WIDTH))

    y_prompt = xp.reshape(b, s, d)
    y_sample = xs.reshape(db, t, d)
    return (y_prompt, y_sample, *[jnp.stack(o) for o in outs_p], *[jnp.stack(o) for o in outs_s])
```

```python
import functools
import math

import jax
import jax.numpy as jnp
from jax import lax
from jax.experimental import pallas as pl
from jax.experimental.pallas import tpu as pltpu

F32 = jnp.float32
MM = jnp.bfloat16

D_MODEL = 2048
A_WIDTH = 2048
A_GROUPS = 8
A_GROUP_DIM = A_WIDTH // A_GROUPS
CHUNK = 128
N_HEADS = 16
KV_HEADS = 4
HEAD_DIM = 128
Q_PER_KV = N_HEADS // KV_HEADS
ROPE_DIM = HEAD_DIM // 4
ROPE_THETA = 500000.0
CMP_BLOCK = 32
SLC_BLOCK = 64
_SLC_SHIFT = 6
N_SELECT = 8
WINDOW = 512
Q_BLOCK = 128
PAGE_SIZE = 128
SCALE = HEAD_DIM ** -0.5
LOG2E = math.log2(math.e)
N_GROUPS = 4
EXPERTS_PER_GROUP = 8
N_EXPERTS = N_GROUPS * EXPERTS_PER_GROUP
EXPERT_DIM = 256
EPS = 1e-6
NEG = -1e30
FORCE_BONUS = 1e4
Q_COLS = N_HEADS * HEAD_DIM
KV_COLS = KV_HEADS * HEAD_DIM

LANES = 128
TN = 512
SEL_CHUNK = 512
CMP_PITCH = CMP_BLOCK * KV_HEADS + 8
EXPERT_TILE = 256
SLOT_ROWS = 8
ROUTED_MIN_TOKENS = N_EXPERTS * EXPERT_TILE // 8
VMEM_LIMIT = 56 * 1024 * 1024

_A_TILES = 3 * D_MODEL // TN
_G_TILES = 2 * D_MODEL // TN + 1
_KV_TILES = 6


def _cparams(sem):
    return pltpu.CompilerParams(dimension_semantics=sem, vmem_limit_bytes=VMEM_LIMIT)


def _rope_tile(z, cos, s1, s2):
    outs = []
    for h in range(z.shape[1] // HEAD_DIM):
        zh = z[:, h * HEAD_DIM:(h + 1) * HEAD_DIM]
        outs.append(zh * cos + pltpu.roll(zh, ROPE_DIM // 2, 1) * s1
                    + pltpu.roll(zh, HEAD_DIM - ROPE_DIM // 2, 1) * s2)
    return jnp.concatenate(outs, axis=1)


def _proj_a_kernel(x_ref, g_ref, w_ref, cos_ref, s1_ref, s2_ref, a_ref, gt_ref, h_ref):
    j = pl.program_id(1)

    @pl.when(j == 0)
    def _():
        x = x_ref[...]
        ms = jnp.mean(x * x, axis=-1, keepdims=True)
        h_ref[...] = (x * lax.rsqrt(ms + EPS) * g_ref[...]).astype(MM)

    z = jnp.dot(h_ref[...], w_ref[...], preferred_element_type=F32)

    @pl.when(j < 2 * D_MODEL // TN)
    def _():
        a_ref[...] = z

    @pl.when((j >= 2 * D_MODEL // TN) & (j < _A_TILES))
    def _():
        a_ref[...] = _rope_tile(z, cos_ref[...], s1_ref[...], s2_ref[...])

    @pl.when(j >= _A_TILES)
    def _():
        gt_ref[...] = z


def _project_a(x2d, g, w, tabs, tm):
    n, d = x2d.shape
    cos, s1, s2 = tabs
    tab_blocks = cos.shape[0] // tm
    row = lambda i, j: (i, 0)
    tab = lambda i, j: (i % tab_blocks, 0)
    return pl.pallas_call(
        _proj_a_kernel,
        grid=(n // tm, _A_TILES + _G_TILES),
        in_specs=[pl.BlockSpec((tm, d), row),
                  pl.BlockSpec((1, d), lambda i, j: (0, 0)),
                  pl.BlockSpec((d, TN), lambda i, j: (0, j)),
                  pl.BlockSpec((tm, LANES), tab), pl.BlockSpec((tm, LANES), tab),
                  pl.BlockSpec((tm, LANES), tab)],
        out_specs=[pl.BlockSpec((tm, TN), lambda i, j: (i, jnp.minimum(j, _A_TILES - 1))),
                   pl.BlockSpec((tm, TN), lambda i, j: (i, jnp.maximum(j - _A_TILES, 0))),
                   pl.BlockSpec((tm, d), row)],
        out_shape=[jax.ShapeDtypeStruct((n, _A_TILES * TN), F32),
                   jax.ShapeDtypeStruct((n, _G_TILES * TN), F32),
                   jax.ShapeDtypeStruct((n, d), MM)],
        compiler_params=_cparams(("parallel", "arbitrary")),
        name="norm_project",
    )(x2d, g, w, cos, s1, s2)


def _proj_kv_kernel(h_ref, w_ref, cos_ref, s1_ref, s2_ref,
                    kc_ref, vc_ref, ks_ref, vs_ref, kw_ref, vw_ref, ksb_ref, vsb_ref, kwb_ref, vwb_ref):
    j = pl.program_id(1)
    z = jnp.dot(h_ref[...], w_ref[...], preferred_element_type=F32)
    outs = ((kc_ref, None, True), (vc_ref, None, False), (ks_ref, ksb_ref, True),
            (vs_ref, vsb_ref, False), (kw_ref, kwb_ref, True), (vw_ref, vwb_ref, False))
    for t, (ref, packed_ref, roped) in enumerate(outs):
        @pl.when(j == t)
        def _(ref=ref, packed_ref=packed_ref, roped=roped):
            v = _rope_tile(z, cos_ref[...], s1_ref[...], s2_ref[...]) if roped else z
            for h in range(KV_HEADS):
                ref[pl.ds(h, z.shape[0], stride=KV_HEADS), :] = v[:, h * HEAD_DIM:(h + 1) * HEAD_DIM]
            if packed_ref is not None:
                packed_ref[...] = v.astype(MM)


def _project_kv(h, w, tabs, tm):
    n, d = h.shape
    cos, s1, s2 = tabs
    tab_blocks = cos.shape[0] // tm
    tab = lambda i, j: (i % tab_blocks, 0)
    return pl.pallas_call(
        _proj_kv_kernel,
        grid=(n // tm, _KV_TILES),
        in_specs=[pl.BlockSpec((tm, d), lambda i, j: (i, 0)),
                  pl.BlockSpec((d, TN), lambda i, j: (0, j)),
                  pl.BlockSpec((tm, LANES), tab), pl.BlockSpec((tm, LANES), tab),
                  pl.BlockSpec((tm, LANES), tab)],
        out_specs=[pl.BlockSpec((tm * KV_HEADS, HEAD_DIM), lambda i, j: (i, 0))] * 6
                  + [pl.BlockSpec((tm, KV_COLS), lambda i, j: (i, 0))] * 4,
        out_shape=[jax.ShapeDtypeStruct((n * KV_HEADS, HEAD_DIM), F32)] * 6
                  + [jax.ShapeDtypeStruct((n, KV_COLS), MM)] * 4,
        compiler_params=_cparams(("parallel", "arbitrary")),
        name="kv_project",
    )(h, w, cos, s1, s2)


def _rope_tables(pos):
    half = ROPE_DIM // 2
    inv = jnp.power(jnp.float32(ROPE_THETA), -jnp.arange(0, ROPE_DIM, 2, dtype=F32) / ROPE_DIM)
    ang = pos.astype(F32)[:, None] * inv[None, :]
    cos, sin = jnp.cos(ang), jnp.sin(ang)
    r = pos.shape[0]
    one = jnp.ones((r, HEAD_DIM - ROPE_DIM), F32)
    zero = jnp.zeros((r, HEAD_DIM - ROPE_DIM), F32)
    zh = jnp.zeros((r, half), F32)
    return (jnp.concatenate([cos, cos, one], axis=1),
            jnp.concatenate([zh, sin, zero], axis=1),
            jnp.concatenate([-sin, zh, zero], axis=1))


def _layout_w_in(w_in):
    c_a = 2 * A_WIDTH + Q_COLS
    c_kv = c_a + 6 * KV_COLS
    g_br, g_ab = w_in[:, c_kv:c_kv + 3 * N_HEADS], w_in[:, c_kv + 3 * N_HEADS:]
    per_head = 3 * Q_PER_KV
    g_br = g_br.reshape(-1, KV_HEADS, per_head)
    g_br = jnp.pad(g_br, ((0, 0), (0, 0), (0, LANES - per_head))).reshape(-1, KV_HEADS * LANES)
    w_a = jnp.concatenate([w_in[:, :c_a], g_ab, g_br], axis=1).astype(MM)
    return w_a, w_in[:, c_a:c_kv].astype(MM)


def _gmlp_kernel(u_ref, v_ref, lng_ref, lnb_ref, mix_ref, bias_ref, o_ref, *maybe_vout):
    u = jax.nn.gelu(u_ref[...])
    v = jax.nn.gelu(v_ref[...])
    mu = jnp.mean(v, axis=-1, keepdims=True)
    vc = v - mu
    var = jnp.mean(vc * vc, axis=-1, keepdims=True)
    v = vc * lax.rsqrt(var + EPS) * lng_ref[...] + lnb_ref[...]
    if maybe_vout:
        maybe_vout[0][...] = v
    for g in range(A_GROUPS):
        sl = slice(g * A_GROUP_DIM, (g + 1) * A_GROUP_DIM)
        s = jnp.dot(mix_ref[g], v[:, sl].astype(MM), preferred_element_type=F32)
        b = bias_ref[g]
        s = s + jnp.concatenate([b] * (A_GROUP_DIM // LANES), axis=1)
        o_ref[:, sl] = (u[:, sl] * s).astype(o_ref.dtype)


def _gmlp(a, ln_g, ln_b, mix, bias, emit_v):
    n = a.shape[0]
    out_shape = [jax.ShapeDtypeStruct((n, A_WIDTH), MM)]
    out_specs = [pl.BlockSpec((CHUNK, A_WIDTH), lambda i: (i, 0))]
    if emit_v:
        out_shape.append(jax.ShapeDtypeStruct((n, A_WIDTH), F32))
        out_specs.append(pl.BlockSpec((CHUNK, A_WIDTH), lambda i: (i, 0)))
    const3 = lambda i: (0, 0, 0)
    return pl.pallas_call(
        _gmlp_kernel,
        grid=(n // CHUNK,),
        in_specs=[pl.BlockSpec((CHUNK, A_WIDTH), lambda i: (i, 0)),
                  pl.BlockSpec((CHUNK, A_WIDTH), lambda i: (i, 1)),
                  pl.BlockSpec((1, A_WIDTH), lambda i: (0, 0)),
                  pl.BlockSpec((1, A_WIDTH), lambda i: (0, 0)),
                  pl.BlockSpec((A_GROUPS, CHUNK, CHUNK), const3),
                  pl.BlockSpec((A_GROUPS, CHUNK, LANES), const3)],
        out_specs=out_specs,
        out_shape=out_shape,
        compiler_params=_cparams(("parallel",)),
        name="gmlp",
    )(a, a, ln_g, ln_b, mix, bias)


def _gmlp_mix(w_s, b_s, t):
    ln = min(t, CHUNK)
    causal = jnp.tril(jnp.ones((ln, ln), dtype=bool))
    w = jnp.where(causal, w_s[:, :ln, :ln], 0.0)
    reps = CHUNK // ln
    eye = jnp.eye(reps, dtype=w.dtype)
    mix = jnp.einsum('ab,gij->gaibj', eye, w).reshape(A_GROUPS, CHUNK, CHUNK)
    bias = jnp.tile(b_s[:, :ln], (1, reps))
    return mix.astype(MM), jnp.broadcast_to(bias[:, :, None], (A_GROUPS, CHUNK, LANES)).astype(F32)


def _compress_kernel(n_pages, pt_ref, *refs):
    k_pages, v_pages = refs[:n_pages], refs[n_pages:2 * n_pages]
    pek_ref, w1k_ref, w2k_ref, pev_ref, w1v_ref, w2v_ref, kc_ref, vc_ref, scr = refs[2 * n_pages:]
    blocks = n_pages * PAGE_SIZE // CMP_BLOCK
    half = blocks // 2

    page_rows = PAGE_SIZE * KV_HEADS
    blk_rows = CMP_BLOCK * KV_HEADS

    region = half * CMP_PITCH

    def one(pages, pe_ref, w1_ref, w2_ref, out_ref):
        for p in range(n_pages):
            page = pages[p][0, 0] + pe_ref[...]
            for bl in range(PAGE_SIZE // CMP_BLOCK):
                m = p * (PAGE_SIZE // CMP_BLOCK) + bl
                off = (m % 2) * region + (m // 2) * CMP_PITCH
                scr[off:off + blk_rows, :] = page[bl * blk_rows:(bl + 1) * blk_rows, :]
        acc = jnp.zeros((KV_HEADS * blocks, HEAD_DIM), F32)
        for jp in range(CMP_BLOCK // 2):
            parts = []
            for j in (2 * jp, 2 * jp + 1):
                rows = [scr[pl.ds(parity * region + j * KV_HEADS + h, half, stride=CMP_PITCH), :]
                        for h in range(KV_HEADS) for parity in (0, 1)]
                parts.append(jnp.concatenate(rows, axis=0).astype(MM))
            lhs = jnp.concatenate(parts, axis=1)
            acc = acc + jnp.dot(lhs, w1_ref[jp], preferred_element_type=F32)
        hid = jax.nn.gelu(acc)
        out = jnp.dot(hid.astype(MM), w2_ref[...], preferred_element_type=F32)
        out_ref[0] = out.reshape(KV_HEADS, blocks, HEAD_DIM)

    one(k_pages, pek_ref, w1k_ref, w2k_ref, kc_ref)
    one(v_pages, pev_ref, w1v_ref, w2v_ref, vc_ref)


def _compress(pool_k, pool_v, layer, page_table, cmp_k, cmp_v):
    n_seq, n_pages = page_table.shape
    blocks = n_pages * PAGE_SIZE // CMP_BLOCK
    page_rows = PAGE_SIZE * KV_HEADS

    def page_spec(p):
        return pl.BlockSpec((1, 1, page_rows, HEAD_DIM), lambda s, pt, p=p: (layer, pt[s, p], 0, 0))

    def prep(c):
        pe, w1, w2 = c
        pe_t = jnp.tile(jnp.repeat(pe, KV_HEADS, axis=0), (PAGE_SIZE // CMP_BLOCK, 1)).astype(F32)
        return pe_t, w1.reshape(CMP_BLOCK // 2, 2 * HEAD_DIM, HEAD_DIM).astype(MM), w2.astype(MM)

    const2 = lambda s, pt: (0, 0)
    const3 = lambda s, pt: (0, 0, 0)
    w_specs = [pl.BlockSpec((page_rows, HEAD_DIM), const2),
               pl.BlockSpec((CMP_BLOCK // 2, 2 * HEAD_DIM, HEAD_DIM), const3),
               pl.BlockSpec((HEAD_DIM, HEAD_DIM), const2)]
    out_spec = pl.BlockSpec((1, KV_HEADS, blocks, HEAD_DIM), lambda s, pt: (s, 0, 0, 0))
    out_shape = jax.ShapeDtypeStruct((n_seq, KV_HEADS, blocks, HEAD_DIM), F32)
    return pl.pallas_call(
        functools.partial(_compress_kernel, n_pages),
        grid_spec=pltpu.PrefetchScalarGridSpec(
            num_scalar_prefetch=1,
            grid=(n_seq,),
            in_specs=[page_spec(p) for p in range(n_pages)] * 2 + w_specs * 2,
            out_specs=[out_spec, out_spec],
            scratch_shapes=[pltpu.VMEM((blocks * CMP_PITCH, HEAD_DIM), F32)]),
        out_shape=[out_shape, out_shape],
        compiler_params=_cparams(("arbitrary",)),
        name="compress",
    )(page_table, *([pool_k] * n_pages), *([pool_v] * n_pages), *prep(cmp_k), *prep(cmp_v))


def _select_rows(imp, qpos, n_blocks):
    j = lax.broadcasted_iota(jnp.int32, (n_blocks, 1), 0)
    forced = (j == 0) | (j == (qpos >> _SLC_SHIFT))
    future = j * SLC_BLOCK > qpos
    score = jnp.where(future, -1.0, imp + jnp.where(forced, FORCE_BONUS, 0.0))
    rank = jnp.zeros(score.shape, jnp.int32)
    for i in range(n_blocks):
        ri = score[i:i + 1, :]
        beats = (ri > score) | ((ri == score) & (i < j))
        rank = rank + beats.astype(jnp.int32)
    return rank < min(N_SELECT, n_blocks)


def _prompt_attn_kernel(q_ref, kc_ref, vc_ref, ks_ref, vs_ref, kw_ref, vw_ref, gbr_ref, o_ref,
                        vst_scr, vwt_scr, bias_scr, acc_scr, ow_scr):
    n = pl.program_id(2)
    seq = ks_ref.shape[0]
    n_tiles = seq // Q_BLOCK
    n_sel_blocks = seq // SLC_BLOCK
    n_cmp = kc_ref.shape[2]
    lane_groups = [slice(g * Q_BLOCK, (g + 1) * Q_BLOCK) for g in range(Q_PER_KV)]

    tiles_per_chunk = SEL_CHUNK // Q_BLOCK

    @pl.when(n == 0)
    def _():
        for kt in range(n_tiles):
            rows = slice(kt * Q_BLOCK, (kt + 1) * Q_BLOCK)
            c, j = divmod(kt, tiles_per_chunk)
            vst_scr[c, :, j * Q_BLOCK:(j + 1) * Q_BLOCK] = vs_ref[rows, :].astype(F32).T.astype(MM)
            vwt_scr[kt] = vw_ref[rows, :].astype(F32).T.astype(MM)

    q = q_ref[...] * (SCALE * LOG2E)
    q_t = jnp.concatenate([q[:, g * HEAD_DIM:(g + 1) * HEAD_DIM].T for g in range(Q_PER_KV)],
                          axis=1).astype(MM)
    qpos = n * Q_BLOCK + lax.broadcasted_iota(jnp.int32, (1, Q_BLOCK), 1)

    win_keys = WINDOW + Q_BLOCK
    w_tile0 = jnp.maximum(n - WINDOW // Q_BLOCK, 0)
    w_start = pl.multiple_of(w_tile0 * Q_BLOCK, Q_BLOCK)
    s_w = jnp.dot(kw_ref[pl.ds(w_start, win_keys), :], q_t, preferred_element_type=F32)
    dist = qpos - (w_start + lax.broadcasted_iota(jnp.int32, (win_keys, 1), 0))
    w_bias = jnp.where((dist >= 0) & (dist < WINDOW), 0.0, NEG)
    ls_w, ps = [], []
    for sl in lane_groups:
        s = s_w[:, sl] + w_bias
        p = jnp.exp2(s - jnp.max(s, axis=0, keepdims=True))
        ls_w.append(jnp.sum(p, axis=0, keepdims=True))
        ps.append(p.astype(MM))
    p_w = jnp.concatenate(ps, axis=1)
    ow_t = functools.reduce(
        lambda a, b: a + b,
        [jnp.dot(vwt_scr[w_tile0 + j], p_w[j * Q_BLOCK:(j + 1) * Q_BLOCK, :], preferred_element_type=F32)
         for j in range(win_keys // Q_BLOCK)])
    ow_scr[...] = ow_t

    s_c = jnp.dot(kc_ref[0, 0].astype(MM), q_t, preferred_element_type=F32)
    r = lax.broadcasted_iota(jnp.int32, (n_cmp, 1), 0)
    blk = jnp.where(r < n_cmp // 2, 2 * r, 2 * (r - n_cmp // 2) + 1)
    ok = ((blk + 1) * CMP_BLOCK - 1) <= qpos
    imp = jnp.zeros((n_cmp, Q_BLOCK), F32)
    probs = []
    for sl in lane_groups:
        s = jnp.where(ok, s_c[:, sl], NEG)
        e = jnp.exp2(s - jnp.max(s, axis=0, keepdims=True))
        p = jnp.where(ok, e / jnp.sum(e, axis=0, keepdims=True), 0.0)
        imp = imp + p
        probs.append(p.astype(MM))
    oc_t = jnp.dot(vc_ref[0, 0].T.astype(MM), jnp.concatenate(probs, axis=1), preferred_element_type=F32)

    sel = _select_rows(imp[:n_cmp // 2] + imp[n_cmp // 2:], qpos, n_sel_blocks)
    sel_bias = jnp.where(sel, 0.0, NEG)
    blocks_per_chunk = SEL_CHUNK // SLC_BLOCK
    key_row = lax.broadcasted_iota(jnp.int32, (SEL_CHUNK, 1), 0)
    n_chunks = lax.div(n, tiles_per_chunk) + 1
    for c in range(seq // SEL_CHUNK):
        @pl.when(c < n_chunks)
        def _(c=c):
            tile = jnp.concatenate(
                [jnp.broadcast_to(sel_bias[c * blocks_per_chunk + i:c * blocks_per_chunk + i + 1, :],
                                  (SLC_BLOCK, Q_BLOCK)) for i in range(blocks_per_chunk)], axis=0)
            bias_scr[c] = jnp.where(c * SEL_CHUNK + key_row <= qpos, tile, NEG)

    acc_scr[...] = jnp.zeros_like(acc_scr)

    def body(c, carry):
        ms, ls = carry
        start = pl.multiple_of(c * SEL_CHUNK, SEL_CHUNK)
        s_t = jnp.dot(ks_ref[pl.ds(start, SEL_CHUNK), :], q_t, preferred_element_type=F32)
        bias = bias_scr[c]
        new_ms, new_ls, ps, alphas = [], [], [], []
        for g, sl in enumerate(lane_groups):
            s = s_t[:, sl] + bias
            m_new = jnp.maximum(ms[g], jnp.max(s, axis=0, keepdims=True))
            alpha = jnp.exp2(ms[g] - m_new)
            p = jnp.exp2(s - m_new)
            new_ls.append(alpha * ls[g] + jnp.sum(p, axis=0, keepdims=True))
            new_ms.append(m_new)
            alphas.append(alpha)
            ps.append(p.astype(MM))
        pv = jnp.dot(vst_scr[c], jnp.concatenate(ps, axis=1), preferred_element_type=F32)
        for g, sl in enumerate(lane_groups):
            acc_scr[:, sl] = alphas[g] * acc_scr[:, sl] + pv[:, sl]
        return tuple(new_ms), tuple(new_ls)

    init = (tuple(jnp.full((1, Q_BLOCK), NEG, F32) for _ in lane_groups),
            tuple(jnp.zeros((1, Q_BLOCK), F32) for _ in lane_groups))
    _, ls_s = lax.fori_loop(0, n_chunks, body, init)

    gate_t = jax.nn.sigmoid(gbr_ref[...]).T
    for g, sl in enumerate(lane_groups):
        o_t = (oc_t[:, sl] * gate_t[3 * g:3 * g + 1, :]
               + (acc_scr[:, sl] / ls_s[g]) * gate_t[3 * g + 1:3 * g + 2, :]
               + (ow_scr[:, sl] / ls_w[g]) * gate_t[3 * g + 2:3 * g + 3, :])
        o_ref[:, g * HEAD_DIM:(g + 1) * HEAD_DIM] = o_t.T.astype(o_ref.dtype)


def _prompt_attention(a, kc, vc, ks, vs, kw, vw, gt, batch, seq):
    n_qb = seq // Q_BLOCK
    q_col0 = 2 * A_WIDTH // (Q_PER_KV * HEAD_DIM)
    gbr_col0 = 2 * D_MODEL // LANES
    cmp_spec = pl.BlockSpec((1, 1, kc.shape[2], HEAD_DIM), lambda b, h, n: (b, h, 0, 0))
    kv_spec = pl.BlockSpec((seq, HEAD_DIM), lambda b, h, n: (b, h))
    assert seq % SEL_CHUNK == 0 and seq >= WINDOW + Q_BLOCK
    scratch = [pltpu.VMEM((seq // SEL_CHUNK, HEAD_DIM, SEL_CHUNK), MM),
               pltpu.VMEM((n_qb, HEAD_DIM, Q_BLOCK), MM),
               pltpu.VMEM((seq // SEL_CHUNK, SEL_CHUNK, Q_BLOCK), F32),
               pltpu.VMEM((HEAD_DIM, Q_PER_KV * Q_BLOCK), F32),
               pltpu.VMEM((HEAD_DIM, Q_PER_KV * Q_BLOCK), F32)]
    return pl.pallas_call(
        _prompt_attn_kernel,
        grid=(batch, KV_HEADS, n_qb),
        in_specs=[pl.BlockSpec((Q_BLOCK, Q_PER_KV * HEAD_DIM), lambda b, h, n: (b * n_qb + n, q_col0 + h)),
                  cmp_spec, cmp_spec, kv_spec, kv_spec, kv_spec, kv_spec,
                  pl.BlockSpec((Q_BLOCK, LANES), lambda b, h, n: (b * n_qb + n, gbr_col0 + h))],
        out_specs=pl.BlockSpec((Q_BLOCK, Q_PER_KV * HEAD_DIM), lambda b, h, n: (b * n_qb + n, h)),
        out_shape=jax.ShapeDtypeStruct((batch * seq, Q_COLS), MM),
        scratch_shapes=scratch,
        compiler_params=_cparams(("parallel", "parallel", "arbitrary")),
        name="prompt_attention",
    )(a, kc, vc, ks, vs, kw, vw, gt)


def _dot_nt(a, b):
    return lax.dot_general(a, b, (((1,), (1,)), ((), ())), preferred_element_type=F32)


def _select_cols(imp, qpos, n_blocks):
    j = lax.broadcasted_iota(jnp.int32, (1, n_blocks), 1)
    forced = (j == 0) | (j == (qpos >> _SLC_SHIFT))
    future = j * SLC_BLOCK > qpos
    score = jnp.where(future, -1.0, imp + jnp.where(forced, FORCE_BONUS, 0.0))
    rank = jnp.zeros(score.shape, jnp.int32)
    for i in range(n_blocks):
        ci = score[:, i:i + 1]
        beats = (ci > score) | ((ci == score) & (i < j))
        rank = rank + beats.astype(jnp.int32)
    return (rank < min(N_SELECT, n_blocks)).astype(F32)


def _softmax_parts(parts):
    m = functools.reduce(jnp.maximum, [jnp.max(s, axis=1, keepdims=True) for s, _ in parts])
    es = [jnp.exp2(s - m) for s, _ in parts]
    den = functools.reduce(lambda a, b: a + b, [jnp.sum(e, axis=1, keepdims=True) for e in es])
    acc = None
    for e, (_, v) in zip(es, parts):
        o = jnp.dot((e / den).astype(MM), v, preferred_element_type=F32)
        acc = o if acc is None else acc + o
    return acc


def _sample_attn_kernel(n_pages, pt_ref, *refs):
    ks_pages, vs_pages = refs[:n_pages], refs[n_pages:2 * n_pages]
    (q_ref, kc_ref, vc_ref, kbuf_ref, vbuf_ref, ksn_ref, vsn_ref, kwn_ref, vwn_ref, gbr_ref,
     o_ref, kwo_ref, vwo_ref) = refs[2 * n_pages:]
    t = q_ref.shape[1]
    past = n_pages * PAGE_SIZE
    nb_past = past // SLC_BLOCK
    buf_rows = kbuf_ref.shape[2]
    wb = buf_rows // KV_HEADS
    new_rows = t * KV_HEADS
    hq = KV_HEADS * t
    n_cmp = kc_ref.shape[2]
    head_shift, t_shift, cmp_shift = (v.bit_length() - 1 for v in (KV_HEADS, t, n_cmp))
    assert (KV_HEADS, t, n_cmp) == (1 << head_shift, 1 << t_shift, 1 << cmp_shift) and LANES % KV_HEADS == 0

    q = q_ref[0] * (SCALE * LOG2E)
    qb = jnp.concatenate([q[:, (h * Q_PER_KV + g) * HEAD_DIM:(h * Q_PER_KV + g + 1) * HEAD_DIM]
                          for g in range(Q_PER_KV) for h in range(KV_HEADS)], axis=0).astype(MM)
    ri = lax.broadcasted_iota(jnp.int32, (Q_PER_KV * hq, 1), 0)
    row_h = (ri >> t_shift) & (KV_HEADS - 1)
    row_t = ri & (t - 1)
    qpos = past + row_t

    def key_cols(n):
        c = lax.broadcasted_iota(jnp.int32, (1, n), 1)
        return c & (KV_HEADS - 1), c >> head_shift

    kwo_ref[0, 0, :buf_rows - new_rows] = kbuf_ref[0, 0, new_rows:]
    kwo_ref[0, 0, buf_rows - new_rows:] = kwn_ref[0]
    vwo_ref[0, 0, :buf_rows - new_rows] = vbuf_ref[0, 0, new_rows:]
    vwo_ref[0, 0, buf_rows - new_rows:] = vwn_ref[0]

    kc = kc_ref[0].reshape(KV_HEADS * n_cmp, HEAD_DIM).astype(MM)
    vc = vc_ref[0].reshape(KV_HEADS * n_cmp, HEAD_DIM).astype(MM)
    col = lax.broadcasted_iota(jnp.int32, (1, KV_HEADS * n_cmp), 1)
    slot = col & (n_cmp - 1)
    blk = jnp.where(slot < n_cmp // 2, 2 * slot, 2 * (slot - n_cmp // 2) + 1)
    ok = ((col >> cmp_shift) == row_h) & (((blk + 1) * CMP_BLOCK - 1) <= qpos)
    s = jnp.where(ok, _dot_nt(qb, kc), NEG)
    e = jnp.exp2(s - jnp.max(s, axis=1, keepdims=True))
    p = jnp.where(ok, e / jnp.sum(e, axis=1, keepdims=True), 0.0)
    o_c = jnp.dot(p.astype(MM), vc, preferred_element_type=F32)
    p = functools.reduce(lambda a, b: a + b, [p[:, h * n_cmp:(h + 1) * n_cmp] for h in range(KV_HEADS)])
    p = functools.reduce(lambda a, b: a + b, [p[g * hq:(g + 1) * hq] for g in range(Q_PER_KV)])
    imp = jnp.concatenate([p[:, :n_cmp // 2] + p[:, n_cmp // 2:], jnp.zeros((hq, 1), F32)], axis=1)
    sel = _select_cols(imp, qpos[:hq], nb_past + 1)
    sel = jnp.concatenate([sel] * Q_PER_KV, axis=0)

    page_h, page_r = key_cols(PAGE_SIZE * KV_HEADS)
    own_head = page_h == row_h
    page_blk = page_r >> _SLC_SHIFT
    per_page = PAGE_SIZE // SLC_BLOCK
    parts = []
    for pg in range(n_pages):
        ok = own_head & functools.reduce(
            lambda a, b: a | b,
            [(page_blk == c) & (sel[:, pg * per_page + c:pg * per_page + c + 1] > 0.5) for c in range(per_page)])
        parts.append((jnp.where(ok, _dot_nt(qb, ks_pages[pg][0, 0].astype(MM)), NEG),
                      vs_pages[pg][0, 0].astype(MM)))
    new_h, new_t = key_cols(new_rows)
    new_ok = (new_h == row_h) & (new_t <= row_t)
    parts.append((jnp.where(new_ok & (sel[:, nb_past:nb_past + 1] > 0.5),
                            _dot_nt(qb, ksn_ref[0].astype(MM)), NEG), vsn_ref[0].astype(MM)))
    o_s = _softmax_parts(parts)

    buf_h, buf_r = key_cols(buf_rows)
    buf_pos = past - wb + buf_r
    dist = qpos - buf_pos
    buf_ok = (buf_h == row_h) & (dist >= 0) & (dist < WINDOW) & (buf_pos >= 0)
    o_w = _softmax_parts([
        (jnp.where(buf_ok, _dot_nt(qb, kbuf_ref[0, 0].astype(MM)), NEG), vbuf_ref[0, 0].astype(MM)),
        (jnp.where(new_ok, _dot_nt(qb, kwn_ref[0].astype(MM)), NEG), vwn_ref[0].astype(MM))])

    gate = jax.nn.sigmoid(gbr_ref[0])
    for g in range(Q_PER_KV):
        for h in range(KV_HEADS):
            rs = slice(g * hq + h * t, g * hq + (h + 1) * t)
            c0 = h * LANES + 3 * g
            o = (o_c[rs] * gate[:, c0:c0 + 1] + o_s[rs] * gate[:, c0 + 1:c0 + 2]
                 + o_w[rs] * gate[:, c0 + 2:c0 + 3])
            col0 = (h * Q_PER_KV + g) * HEAD_DIM
            o_ref[0, :, col0:col0 + HEAD_DIM] = o.astype(o_ref.dtype)


def _sample_attention(a3, kc, vc, pool_ks, pool_vs, layer, page_table, kbuf, vbuf, ksn, vsn, kwn, vwn, gt3):
    db, t, _ = a3.shape
    n_pages = page_table.shape[1]
    buf_rows = kbuf.shape[2]
    page_rows = PAGE_SIZE * KV_HEADS

    def page_spec(p):
        return pl.BlockSpec((1, 1, page_rows, HEAD_DIM), lambda s, pt, p=p: (layer, pt[s, p], 0, 0))

    row3 = lambda s, pt: (s, 0, 0)
    cmp_spec = pl.BlockSpec((1, KV_HEADS, kc.shape[2], HEAD_DIM), lambda s, pt: (s, 0, 0, 0))
    buf_spec = pl.BlockSpec((1, 1, buf_rows, HEAD_DIM), lambda s, pt: (layer, s, 0, 0))
    out_buf_spec = pl.BlockSpec((1, 1, buf_rows, HEAD_DIM), lambda s, pt: (0, s, 0, 0))
    new_spec = pl.BlockSpec((1, t * KV_HEADS, HEAD_DIM), row3)
    buf_shape = jax.ShapeDtypeStruct((1, db, buf_rows, HEAD_DIM), F32)
    return pl.pallas_call(
        functools.partial(_sample_attn_kernel, n_pages),
        grid_spec=pltpu.PrefetchScalarGridSpec(
            num_scalar_prefetch=1,
            grid=(db,),
            in_specs=[page_spec(p) for p in range(n_pages)] * 2
                     + [pl.BlockSpec((1, t, Q_COLS), lambda s, pt: (s, 0, 2 * A_WIDTH // Q_COLS)),
                        cmp_spec, cmp_spec, buf_spec, buf_spec, new_spec, new_spec, new_spec, new_spec,
                        pl.BlockSpec((1, t, KV_HEADS * LANES),
                                     lambda s, pt: (s, 0, 2 * D_MODEL // (KV_HEADS * LANES)))],
            out_specs=[pl.BlockSpec((1, t, Q_COLS), row3), out_buf_spec, out_buf_spec]),
        out_shape=[jax.ShapeDtypeStruct((db, t, Q_COLS), MM), buf_shape, buf_shape],
        compiler_params=_cparams(("arbitrary",)),
        name="sample_attention",
    )(page_table, *([pool_ks] * n_pages), *([pool_vs] * n_pages),
      a3, kc, vc, kbuf, vbuf, ksn, vsn, kwn, vwn, gt3)


def _mix_kernel(a_ref, o_ref, wpa_ref, wpb_ref, ga_ref, gb_ref, m_ref):
    y_a = jnp.dot(a_ref[...], wpa_ref[...], preferred_element_type=F32)
    y_b = jnp.dot(o_ref[...], wpb_ref[...], preferred_element_type=F32)
    m = jax.nn.sigmoid(ga_ref[...]) * y_a + jax.nn.sigmoid(gb_ref[...]) * y_b
    m_ref[...] = m.astype(m_ref.dtype)


def _mix(a_out, o, w_pa, w_pb, gt, tm):
    n = a_out.shape[0]
    gb0 = D_MODEL // TN
    return pl.pallas_call(
        _mix_kernel,
        grid=(n // tm, D_MODEL // TN),
        in_specs=[pl.BlockSpec((tm, A_WIDTH), lambda i, j: (i, 0)),
                  pl.BlockSpec((tm, Q_COLS), lambda i, j: (i, 0)),
                  pl.BlockSpec((A_WIDTH, TN), lambda i, j: (0, j)),
                  pl.BlockSpec((Q_COLS, TN), lambda i, j: (0, j)),
                  pl.BlockSpec((tm, TN), lambda i, j: (i, j)),
                  pl.BlockSpec((tm, TN), lambda i, j: (i, gb0 + j))],
        out_specs=pl.BlockSpec((tm, TN), lambda i, j: (i, j)),
        out_shape=jax.ShapeDtypeStruct((n, D_MODEL), MM),
        compiler_params=_cparams(("parallel", "arbitrary")),
        name="merge_gate",
    )(a_out, o, w_pa, w_pb, gt, gt)


def _out_proj_kernel(m_ref, w_ref, x_ref, o_ref):
    o_ref[...] = x_ref[...] + jnp.dot(m_ref[...], w_ref[...], preferred_element_type=F32)


def _out_proj(m, w_o, x2d, tm):
    n = m.shape[0]
    return pl.pallas_call(
        _out_proj_kernel,
        grid=(n // tm, D_MODEL // TN),
        in_specs=[pl.BlockSpec((tm, D_MODEL), lambda i, j: (i, 0)),
                  pl.BlockSpec((D_MODEL, TN), lambda i, j: (0, j)),
                  pl.BlockSpec((tm, TN), lambda i, j: (i, j))],
        out_specs=pl.BlockSpec((tm, TN), lambda i, j: (i, j)),
        out_shape=jax.ShapeDtypeStruct((n, D_MODEL), F32),
        compiler_params=_cparams(("parallel", "arbitrary")),
        name="out_proj",
    )(m, w_o, x2d)


def _normed_logits(x, gf_ref, wr_hi_ref, wr_lo_ref, br_ref):
    ms = jnp.mean(x * x, axis=-1, keepdims=True)
    h = x * lax.rsqrt(ms + EPS) * gf_ref[...]
    hi = h.astype(MM)
    lo = (h - hi.astype(F32)).astype(MM)
    logits = (jnp.dot(hi, wr_hi_ref[...], preferred_element_type=F32)
              + (jnp.dot(hi, wr_lo_ref[...], preferred_element_type=F32)
                 + jnp.dot(lo, wr_hi_ref[...], preferred_element_type=F32))) + br_ref[...]
    return hi, logits


def _route(logits):
    lane = lax.broadcasted_iota(jnp.int32, logits.shape, 1)
    big = jnp.int32(LANES)
    is_g = lane < N_GROUPS
    gl = jnp.where(is_g, logits, -jnp.inf)
    gm = jnp.max(gl, axis=1, keepdims=True)
    ge = jnp.exp(gl - gm)
    pg = ge / jnp.sum(ge, axis=1, keepdims=True)
    pg_top = jnp.max(pg, axis=1, keepdims=True)
    g_idx = jnp.min(jnp.where(is_g & (pg == pg_top), lane, big), axis=1, keepdims=True)
    in_grp = (lane >= N_GROUPS) & (((lane - N_GROUPS) >> 3) == g_idx)
    el = jnp.where(in_grp, logits, -jnp.inf)
    em = jnp.max(el, axis=1, keepdims=True)
    ee = jnp.exp(el - em)
    pe = ee / jnp.sum(ee, axis=1, keepdims=True)
    p1 = jnp.max(pe, axis=1, keepdims=True)
    i1 = jnp.min(jnp.where(in_grp & (pe == p1), lane, big), axis=1, keepdims=True)
    rest = in_grp & (lane != i1)
    p2 = jnp.max(jnp.where(rest, pe, -1.0), axis=1, keepdims=True)
    i2 = jnp.min(jnp.where(rest & (pe == p2), lane, big), axis=1, keepdims=True)
    tot = p1 + p2
    return i1, i2, pg_top * p1 / tot, pg_top * p2 / tot


def _moe_kernel(final_norm, x_ref, gf_ref, wr_hi_ref, wr_lo_ref, br_ref, wg_ref, wu_ref, wd_ref, gfin_ref,
                y_ref, h_scr, gate_scr, acc_scr):
    e = pl.program_id(1)

    @pl.when(e == 0)
    def _():
        hi, logits = _normed_logits(x_ref[...], gf_ref, wr_hi_ref, wr_lo_ref, br_ref)
        i1, i2, w1, w2 = _route(logits)
        lane = lax.broadcasted_iota(jnp.int32, logits.shape, 1)
        h_scr[...] = hi
        gate_scr[...] = jnp.where(lane == i1, w1, 0.0) + jnp.where(lane == i2, w2, 0.0)
        acc_scr[...] = jnp.zeros_like(acc_scr)

    h = h_scr[...]
    lane = lax.broadcasted_iota(jnp.int32, gate_scr.shape, 1)
    gcol = jnp.sum(jnp.where(lane == e + N_GROUPS, gate_scr[...], 0.0), axis=1, keepdims=True)
    hid = (jax.nn.silu(jnp.dot(h, wg_ref[0], preferred_element_type=F32))
           * jnp.dot(h, wu_ref[0], preferred_element_type=F32))
    acc_scr[...] += jnp.dot((hid * gcol).astype(MM), wd_ref[0], preferred_element_type=F32)

    @pl.when(e == pl.num_programs(1) - 1)
    def _():
        x2 = x_ref[...] + acc_scr[...]
        if final_norm:
            ms = jnp.mean(x2 * x2, axis=-1, keepdims=True)
            x2 = x2 * lax.rsqrt(ms + EPS) * gfin_ref[...]
        y_ref[...] = x2


def _moe(x1, g_ffn, wr_hi, wr_lo, b_r, w_gate, w_up, w_down, g_final, tm, final_norm):
    n = x1.shape[0]
    row = lambda i, e: (i, 0)
    const = lambda i, e: (0, 0)
    return pl.pallas_call(
        functools.partial(_moe_kernel, final_norm),
        grid=(n // tm, N_EXPERTS),
        in_specs=[pl.BlockSpec((tm, D_MODEL), row),
                  pl.BlockSpec((1, D_MODEL), const),
                  pl.BlockSpec((D_MODEL, LANES), const),
                  pl.BlockSpec((D_MODEL, LANES), const),
                  pl.BlockSpec((1, LANES), const),
                  pl.BlockSpec((1, D_MODEL, EXPERT_DIM), lambda i, e: (e, 0, 0)),
                  pl.BlockSpec((1, D_MODEL, EXPERT_DIM), lambda i, e: (e, 0, 0)),
                  pl.BlockSpec((1, EXPERT_DIM, D_MODEL), lambda i, e: (e, 0, 0)),
                  pl.BlockSpec((1, D_MODEL), const)],
        out_specs=pl.BlockSpec((tm, D_MODEL), row),
        out_shape=jax.ShapeDtypeStruct((n, D_MODEL), F32),
        scratch_shapes=[pltpu.VMEM((tm, D_MODEL), MM), pltpu.VMEM((tm, LANES), F32),
                        pltpu.VMEM((tm, D_MODEL), F32)],
        compiler_params=_cparams(("parallel", "arbitrary")),
        name="moe_final_norm",
    )(x1, g_ffn, wr_hi, wr_lo, b_r, w_gate, w_up, w_down, g_final)


def _router_weights(w_rg, b_rg, w_re, b_re):
    w = jnp.concatenate([w_rg, w_re], axis=1)
    w = jnp.pad(w, ((0, 0), (0, LANES - w.shape[1])))
    b = jnp.pad(jnp.concatenate([b_rg, b_re]), (0, LANES - N_GROUPS - N_EXPERTS))[None, :]
    hi = w.astype(MM)
    lo = (w - hi.astype(F32)).astype(MM)
    return hi, lo, b.astype(F32)


def _router_kernel(x_ref, gf_ref, wr_hi_ref, wr_lo_ref, br_ref, tril_ref, meta_ref, cnt_ref, base_scr):
    @pl.when(pl.program_id(0) == 0)
    def _():
        base_scr[...] = jnp.zeros_like(base_scr)

    _, logits = _normed_logits(x_ref[...], gf_ref, wr_hi_ref, wr_lo_ref, br_ref)
    i1, i2, w1, w2 = _route(logits)
    lane = lax.broadcasted_iota(jnp.int32, logits.shape, 1)
    hit1, hit2 = lane == i1, lane == i2
    chosen = jnp.where(hit1 | hit2, 1.0, 0.0)
    before = jnp.dot(tril_ref[...], chosen.astype(MM), preferred_element_type=F32) + base_scr[...]
    r1 = jnp.sum(jnp.where(hit1, before, 0.0), axis=1, keepdims=True)
    r2 = jnp.sum(jnp.where(hit2, before, 0.0), axis=1, keepdims=True)
    base_scr[...] += jnp.sum(chosen, axis=0, keepdims=True)
    cols = ((i1 - N_GROUPS).astype(F32), (i2 - N_GROUPS).astype(F32), w1, w2, r1, r2)
    meta_ref[...] = functools.reduce(lambda a, b: a + b,
                                     [jnp.where(lane == k, c, 0.0) for k, c in enumerate(cols)])
    cnt_ref[...] = base_scr[...]


def _router(x1, g_ffn, wr_hi, wr_lo, b_r, tm):
    n = x1.shape[0]
    const = lambda i: (0, 0)
    tril = jnp.tril(jnp.ones((tm, tm), MM), -1)
    return pl.pallas_call(
        _router_kernel,
        grid=(n // tm,),
        in_specs=[pl.BlockSpec((tm, D_MODEL), lambda i: (i, 0)),
                  pl.BlockSpec((1, D_MODEL), const),
                  pl.BlockSpec((D_MODEL, LANES), const),
                  pl.BlockSpec((D_MODEL, LANES), const),
                  pl.BlockSpec((1, LANES), const),
                  pl.BlockSpec((tm, tm), const)],
        out_specs=[pl.BlockSpec((tm, LANES), lambda i: (i, 0)), pl.BlockSpec((1, LANES), const)],
        out_shape=[jax.ShapeDtypeStruct((n, LANES), F32), jax.ShapeDtypeStruct((1, LANES), F32)],
        scratch_shapes=[pltpu.VMEM((1, LANES), F32)],
        compiler_params=_cparams(("arbitrary",)),
        name="moe_router",
    )(x1, g_ffn, wr_hi, wr_lo, b_r, tril)


def _slots_kernel(meta_ref, offs_ref, s_ref):
    meta = meta_ref[...]
    lane = lax.broadcasted_iota(jnp.int32, meta.shape, 1)

    def slot(e_col, r_col):
        e_lane = meta[:, e_col:e_col + 1].astype(jnp.int32) + N_GROUPS
        first = jnp.sum(jnp.where(lane == e_lane, offs_ref[...], 0.0), axis=1, keepdims=True)
        return first + meta[:, r_col:r_col + 1]

    both = jnp.where(lane == 0, slot(0, 4), 0.0) + jnp.where(lane == 1, slot(1, 5), 0.0)
    s_ref[0] = both.T[:SLOT_ROWS, :].astype(jnp.int32)


def _slots(meta, offs_lanes, tb):
    n = meta.shape[0]
    return pl.pallas_call(
        _slots_kernel,
        grid=(n // tb,),
        in_specs=[pl.BlockSpec((tb, LANES), lambda i: (i, 0)), pl.BlockSpec((1, LANES), lambda i: (0, 0))],
        out_specs=pl.BlockSpec((1, SLOT_ROWS, tb), lambda i: (i, 0, 0)),
        out_shape=jax.ShapeDtypeStruct((n // tb, SLOT_ROWS, tb), jnp.int32),
        compiler_params=_cparams(("parallel",)),
        name="moe_slots",
    )(meta, offs_lanes)


def _dispatch_kernel(fill_lo_ref, fill_hi_ref, x_ref, s_ref, xs_ref, sem):
    tb = x_ref.shape[0]

    def row_copy(r, slot):
        return pltpu.make_async_copy(x_ref.at[pl.ds(r, 1)], xs_ref.at[pl.ds(slot, 1)], sem)

    def start(r, c):
        row_copy(r, s_ref[0, 0, r]).start(priority=0)
        row_copy(r, s_ref[0, 1, r]).start(priority=1)
        return c

    def wait(r, c):
        row_copy(r, s_ref[0, 0, r]).wait()
        row_copy(r, s_ref[0, 1, r]).wait()
        return c

    lax.fori_loop(0, tb, start, 0)

    @pl.when(pl.program_id(0) == 0)
    def _():
        def fill_start(s, c):
            row_copy(0, s).start()
            return c

        def fill_wait(s, c):
            row_copy(0, s).wait()
            return c

        for e in range(N_EXPERTS):
            lax.fori_loop(fill_lo_ref[e], fill_hi_ref[e], fill_start, 0)
        for e in range(N_EXPERTS):
            lax.fori_loop(fill_lo_ref[e], fill_hi_ref[e], fill_wait, 0)

    lax.fori_loop(0, tb, wait, 0)


def _dispatch(x1, slots, fill_lo, fill_hi, n_slots):
    n, d = x1.shape
    tb = slots.shape[2]
    return pl.pallas_call(
        _dispatch_kernel,
        grid_spec=pltpu.PrefetchScalarGridSpec(
            num_scalar_prefetch=2,
            grid=(n // tb,),
            in_specs=[pl.BlockSpec((tb, d), lambda i, lo, hi: (i, 0)),
                      pl.BlockSpec((1, SLOT_ROWS, tb), lambda i, lo, hi: (i, 0, 0), memory_space=pltpu.SMEM)],
            out_specs=pl.BlockSpec(memory_space=pl.ANY),
            scratch_shapes=[pltpu.SemaphoreType.DMA(())]),
        out_shape=jax.ShapeDtypeStruct((n_slots, d), F32),
        compiler_params=_cparams(("arbitrary",)),
        name="moe_dispatch",
    )(fill_lo, fill_hi, x1, slots)


def _expert_kernel(te_ref, tv_ref, xs_ref, gf_ref, wg_ref, wu_ref, wd_ref, ys_ref):
    used = tv_ref[pl.program_id(0)] == 1

    @pl.when(jnp.logical_not(used))
    def _():
        ys_ref[...] = jnp.zeros_like(ys_ref)

    @pl.when(used)
    def _():
        x = xs_ref[...]
        ms = jnp.mean(x * x, axis=-1, keepdims=True)
        h = (x * lax.rsqrt(ms + EPS) * gf_ref[...]).astype(MM)
        hid = (jax.nn.silu(jnp.dot(h, wg_ref[0], preferred_element_type=F32))
               * jnp.dot(h, wu_ref[0], preferred_element_type=F32))
        ys_ref[...] = jnp.dot(hid.astype(MM), wd_ref[0], preferred_element_type=F32)


def _experts(xs, tile_expert, tile_valid, g_ffn, w_gate, w_up, w_down):
    n_slots, d = xs.shape
    tile = lambda i, te, tv: (i, 0)
    weight = lambda i, te, tv: (te[i], 0, 0)
    return pl.pallas_call(
        _expert_kernel,
        grid_spec=pltpu.PrefetchScalarGridSpec(
            num_scalar_prefetch=2,
            grid=(tile_expert.shape[0],),
            in_specs=[pl.BlockSpec((EXPERT_TILE, d), tile),
                      pl.BlockSpec((1, d), lambda i, te, tv: (0, 0)),
                      pl.BlockSpec((1, d, EXPERT_DIM), weight),
                      pl.BlockSpec((1, d, EXPERT_DIM), weight),
                      pl.BlockSpec((1, EXPERT_DIM, d), weight)],
            out_specs=pl.BlockSpec((EXPERT_TILE, d), tile)),
        out_shape=jax.ShapeDtypeStruct((n_slots, d), F32),
        compiler_params=_cparams(("arbitrary",)),
        name="moe_experts",
    )(tile_expert, tile_valid, xs, g_ffn, w_gate, w_up, w_down)


def _combine_kernel(final_norm, x_ref, meta_ref, s_ref, ys_ref, gfin_ref, y_ref, buf1, buf2, sem):
    tb = x_ref.shape[0]

    def row_copy(buf, r, slot):
        return pltpu.make_async_copy(ys_ref.at[pl.ds(slot, 1)], buf.at[pl.ds(r, 1)], sem)

    def start(r, c):
        row_copy(buf1, r, s_ref[0, 0, r]).start(priority=0)
        row_copy(buf2, r, s_ref[0, 1, r]).start(priority=1)
        return c

    def wait(r, c):
        row_copy(buf1, r, s_ref[0, 0, r]).wait()
        row_copy(buf2, r, s_ref[0, 1, r]).wait()
        return c

    lax.fori_loop(0, tb, start, 0)
    lax.fori_loop(0, tb, wait, 0)
    meta = meta_ref[...]
    x2 = x_ref[...] + (meta[:, 2:3] * buf1[...] + meta[:, 3:4] * buf2[...])
    if final_norm:
        ms = jnp.mean(x2 * x2, axis=-1, keepdims=True)
        x2 = x2 * lax.rsqrt(ms + EPS) * gfin_ref[...]
    y_ref[...] = x2


def _combine(x1, meta, slots, ys, g_final, final_norm):
    n, d = x1.shape
    tb = slots.shape[2]
    return pl.pallas_call(
        functools.partial(_combine_kernel, final_norm),
        grid=(n // tb,),
        in_specs=[pl.BlockSpec((tb, d), lambda i: (i, 0)),
                  pl.BlockSpec((tb, LANES), lambda i: (i, 0)),
                  pl.BlockSpec((1, SLOT_ROWS, tb), lambda i: (i, 0, 0), memory_space=pltpu.SMEM),
                  pl.BlockSpec(memory_space=pl.ANY),
                  pl.BlockSpec((1, d), lambda i: (0, 0))],
        out_specs=pl.BlockSpec((tb, d), lambda i: (i, 0)),
        out_shape=jax.ShapeDtypeStruct((n, d), F32),
        scratch_shapes=[pltpu.VMEM((tb, d), F32), pltpu.VMEM((tb, d), F32), pltpu.SemaphoreType.DMA(())],
        compiler_params=_cparams(("arbitrary",)),
        name="moe_combine_norm",
    )(x1, meta, slots, ys, g_final)


def _routed_moe(x1, g_ffn, wr_hi, wr_lo, b_r, w_gate, w_up, w_down, g_final, final_norm):
    n = x1.shape[0]
    meta, cnt = _router(x1, g_ffn, wr_hi, wr_lo, b_r, _row_tile(n))
    counts = cnt[0, N_GROUPS:N_GROUPS + N_EXPERTS].astype(jnp.int32)
    padded = (counts + EXPERT_TILE - 1) // EXPERT_TILE * EXPERT_TILE
    ends = jnp.cumsum(padded)
    offs = ends - padded
    n_tiles = 2 * n // EXPERT_TILE + N_EXPERTS
    n_slots = n_tiles * EXPERT_TILE
    tile_start = jnp.arange(n_tiles, dtype=jnp.int32) * EXPERT_TILE
    tile_expert = jnp.minimum(jnp.sum((tile_start[:, None] >= ends[None, :]).astype(jnp.int32), axis=1),
                              N_EXPERTS - 1)
    tile_valid = (tile_start < ends[-1]).astype(jnp.int32)
    offs_lanes = jnp.pad(offs.astype(F32), (N_GROUPS, LANES - N_GROUPS - N_EXPERTS))[None, :]
    slots = _slots(meta, offs_lanes, _row_tile(n))
    fill_hi = jnp.concatenate([ends[:-1], jnp.full((1,), n_slots, jnp.int32)])
    xs = _dispatch(x1, slots, offs + counts, fill_hi, n_slots)
    ys = _experts(xs, tile_expert, tile_valid, g_ffn, w_gate, w_up, w_down)
    return _combine(x1, meta, slots, ys, g_final, final_norm)


def _row_tile(n, tm=512):
    return tm if n % tm == 0 else n


def kernel(x_prompt, x_sample, cache_k_cmp, cache_v_cmp, cache_k_slc, cache_v_slc, state_k_win, state_v_win, page_table, g_mix, w_in, a_ln_g, a_ln_b, a_w_s, a_b_s, cmp_pe_k, cmp_w1_k, cmp_w2_k, cmp_pe_v, cmp_w1_v, cmp_w2_v, w_pa, w_pb, w_o, g_ffn, w_rg, b_rg, w_re, b_re, w_gate, w_up, w_down, g_final):
    depth = g_mix.shape[0]
    b, s, d = x_prompt.shape
    db, t, _ = x_sample.shape
    n_pages = page_table.shape[1]
    past = n_pages * PAGE_SIZE
    wl = min(WINDOW, s)
    assert s % Q_BLOCK == 0 and CHUNK % t == 0 and (db * t) % CHUNK == 0 and t < CMP_BLOCK

    tabs_p = _rope_tables(jnp.arange(s))
    tabs_s = _rope_tables(past + (jnp.arange(db * t) % t))
    prompt_pages = jnp.arange(b * s // PAGE_SIZE, dtype=jnp.int32).reshape(b, s // PAGE_SIZE)
    g_final2 = g_final[None, :]

    xp = x_prompt.reshape(b * s, d)
    xs = x_sample.reshape(db * t, d)
    outs_p = [[] for _ in range(6)]
    outs_s = [[] for _ in range(7)]
    for l in range(depth):
        w_a, w_kv = _layout_w_in(w_in[l])
        g_l = g_mix[l][None, :]
        ln_g, ln_b = a_ln_g[l][None, :], a_ln_b[l][None, :]
        cmp_k = (cmp_pe_k[l], cmp_w1_k[l], cmp_w2_k[l])
        cmp_v = (cmp_pe_v[l], cmp_w1_v[l], cmp_w2_v[l])
        wpa, wpb, wo = w_pa[l].astype(MM), w_pb[l].astype(MM), w_o[l].astype(MM)
        wr_hi, wr_lo, b_r = _router_weights(w_rg[l], b_rg[l], w_re[l], b_re[l])
        wg, wu, wd = w_gate[l].astype(MM), w_up[l].astype(MM), w_down[l].astype(MM)
        last = l == depth - 1

        def tail(x2d, a_out, o, gt):
            n = x2d.shape[0]
            m = _mix(a_out, o, wpa, wpb, gt, _row_tile(n, 1024))
            x1 = _out_proj(m, wo, x2d, _row_tile(n, 1024))
            moe_args = (x1, g_ffn[l][None, :], wr_hi, wr_lo, b_r, wg, wu, wd, g_final2)
            if n >= ROUTED_MIN_TOKENS:
                return _routed_moe(*moe_args, last)
            return _moe(*moe_args, _row_tile(n), last)

        a, gt, h = _project_a(xp, g_l, w_a, tabs_p, _row_tile(b * s, 1024))
        kc_r, vc_r, ks_r, vs_r, kw_r, vw_r, ks_b, vs_b, kw_b, vw_b = _project_kv(
            h, w_kv, tabs_p, _row_tile(b * s))
        mix_p, bias_p = _gmlp_mix(a_w_s[l], a_b_s[l], s)
        (a_out,) = _gmlp(a, ln_g, ln_b, mix_p, bias_p, False)
        pages = lambda r: r.reshape(1, b * s // PAGE_SIZE, PAGE_SIZE * KV_HEADS, HEAD_DIM)
        kc, vc = _compress(pages(kc_r), pages(vc_r), 0, prompt_pages, cmp_k, cmp_v)
        o = _prompt_attention(a, kc, vc, ks_b, vs_b, kw_b, vw_b, gt, b, s)
        xp = tail(xp, a_out, o, gt)
        heads = lambda r: r.reshape(b, s, KV_HEADS, HEAD_DIM)
        for lst, r in zip(outs_p, (kc_r, vc_r, ks_r, vs_r)):
            lst.append(heads(r))
        outs_p[4].append(heads(kw_r)[:, -wl:])
        outs_p[5].append(heads(vw_r)[:, -wl:])

        a, gt, h = _project_a(xs, g_l, w_a, tabs_s, _row_tile(db * t))
        kc_r, vc_r, ks_r, vs_r, kw_r, vw_r, _, _, _, _ = _project_kv(h, w_kv, tabs_s, _row_tile(db * t))
        mix_s, bias_s = _gmlp_mix(a_w_s[l], a_b_s[l], t)
        a_out, v_new = _gmlp(a, ln_g, ln_b, mix_s, bias_s, True)
        flat = lambda c: c.reshape(c.shape[0], c.shape[1], c.shape[2] * KV_HEADS, HEAD_DIM)
        kc, vc = _compress(flat(cache_k_cmp), flat(cache_v_cmp), l, page_table, cmp_k, cmp_v)
        r3 = lambda r: r.reshape(db, t, r.shape[-1])
        new = lambda r: r.reshape(db, t * KV_HEADS, HEAD_DIM)
        o, kwin, vwin = _sample_attention(
            r3(a), kc, vc, flat(cache_k_slc), flat(cache_v_slc), l, page_table,
            flat(state_k_win), flat(state_v_win), new(ks_r), new(vs_r), new(kw_r), new(vw_r), r3(gt))
        xs = tail(xs, a_out, o.reshape(db * t, Q_COLS), gt)
        r4 = lambda r: r.reshape(db, -1, KV_HEADS, HEAD_DIM)
        for lst, r in zip(outs_s, (kc_r, vc_r, ks_r, vs_r, kwin, vwin)):
            lst.append(r4(r))
        outs_s[6].append(v_new.reshape(db, t, A_WIDTH))

    y_prompt = xp.reshape(b, s, d)
    y_sample = xs.reshape(db, t, d)
    return (y_prompt, y_sample, *[jnp.stack(o) for o in outs_p], *[jnp.stack(o) for o in outs_s])
```

```python
import functools
import math

import jax
import jax.numpy as jnp
from jax import lax
from jax.experimental import pallas as pl
from jax.experimental.pallas import tpu as pltpu

F32 = jnp.float32
MM = jnp.bfloat16

D_MODEL = 2048
A_WIDTH = 2048
A_GROUPS = 8
A_GROUP_DIM = A_WIDTH // A_GROUPS
CHUNK = 128
N_HEADS = 16
KV_HEADS = 4
HEAD_DIM = 128
Q_PER_KV = N_HEADS // KV_HEADS
ROPE_DIM = HEAD_DIM // 4
ROPE_THETA = 500000.0
CMP_BLOCK = 32
SLC_BLOCK = 64
_SLC_SHIFT = 6
N_SELECT = 8
WINDOW = 512
Q_BLOCK = 128
PAGE_SIZE = 128
SCALE = HEAD_DIM ** -0.5
LOG2E = math.log2(math.e)
N_GROUPS = 4
EXPERTS_PER_GROUP = 8
N_EXPERTS = N_GROUPS * EXPERTS_PER_GROUP
EXPERT_DIM = 256
EPS = 1e-6
NEG = -1e30
FORCE_BONUS = 1e4
Q_COLS = N_HEADS * HEAD_DIM
KV_COLS = KV_HEADS * HEAD_DIM

LANES = 128
TN = 512
SEL_CHUNK = 512
CMP_PITCH = CMP_BLOCK * KV_HEADS + 8
EXPERT_TILE = 256
SLOT_ROWS = 8
ROUTED_MIN_TOKENS = N_EXPERTS * EXPERT_TILE // 8
VMEM_LIMIT = 56 * 1024 * 1024

_A_TILES = 3 * D_MODEL // TN
_G_TILES = 2 * D_MODEL // TN + 1
_KV_TILES = 6


def _cparams(sem):
    return pltpu.CompilerParams(dimension_semantics=sem, vmem_limit_bytes=VMEM_LIMIT)


def _rope_tile(z, cos, s1, s2):
    outs = []
    for h in range(z.shape[1] // HEAD_DIM):
        zh = z[:, h * HEAD_DIM:(h + 1) * HEAD_DIM]
        outs.append(zh * cos + pltpu.roll(zh, ROPE_DIM // 2, 1) * s1
                    + pltpu.roll(zh, HEAD_DIM - ROPE_DIM // 2, 1) * s2)
    return jnp.concatenate(outs, axis=1)


def _proj_a_kernel(x_ref, g_ref, w_ref, cos_ref, s1_ref, s2_ref, a_ref, gt_ref, h_ref):
    j = pl.program_id(1)

    @pl.when(j == 0)
    def _():
        x = x_ref[...]
        ms = jnp.mean(x * x, axis=-1, keepdims=True)
        h_ref[...] = (x * lax.rsqrt(ms + EPS) * g_ref[...]).astype(MM)

    def z():
        return jnp.dot(h_ref[...], w_ref[...], preferred_element_type=F32)

    @pl.when(j < 2 * D_MODEL // TN)
    def _():
        a_ref[...] = z()

    @pl.when((j >= 2 * D_MODEL // TN) & (j < _A_TILES))
    def _():
        a_ref[...] = _rope_tile(z(), cos_ref[...], s1_ref[...], s2_ref[...])

    @pl.when(j >= _A_TILES)
    def _():
        gt_ref[...] = z()


def _project_a(x2d, g, w, tabs, tm):
    n, d = x2d.shape
    cos, s1, s2 = tabs
    tab_blocks = cos.shape[0] // tm
    row = lambda i, j: (i, 0)
    tab = lambda i, j: (i % tab_blocks, 0)
    return pl.pallas_call(
        _proj_a_kernel,
        grid=(n // tm, _A_TILES + _G_TILES),
        in_specs=[pl.BlockSpec((tm, d), row),
                  pl.BlockSpec((1, d), lambda i, j: (0, 0)),
                  pl.BlockSpec((d, TN), lambda i, j: (0, j)),
                  pl.BlockSpec((tm, LANES), tab), pl.BlockSpec((tm, LANES), tab),
                  pl.BlockSpec((tm, LANES), tab)],
        out_specs=[pl.BlockSpec((tm, TN), lambda i, j: (i, jnp.minimum(j, _A_TILES - 1))),
                   pl.BlockSpec((tm, TN), lambda i, j: (i, jnp.maximum(j - _A_TILES, 0))),
                   pl.BlockSpec((tm, d), row)],
        out_shape=[jax.ShapeDtypeStruct((n, _A_TILES * TN), F32),
                   jax.ShapeDtypeStruct((n, _G_TILES * TN), F32),
                   jax.ShapeDtypeStruct((n, d), MM)],
        compiler_params=_cparams(("parallel", "arbitrary")),
        name="norm_project",
    )(x2d, g, w, cos, s1, s2)


def _proj_kv_kernel(h_ref, w_ref, cos_ref, s1_ref, s2_ref,
                    kc_ref, vc_ref, ks_ref, vs_ref, kw_ref, vw_ref, ksb_ref, vsb_ref, kwb_ref, vwb_ref):
    j = pl.program_id(1)
    outs = ((kc_ref, None, True), (vc_ref, None, False), (ks_ref, ksb_ref, True),
            (vs_ref, vsb_ref, False), (kw_ref, kwb_ref, True), (vw_ref, vwb_ref, False))
    for t, (ref, packed_ref, roped) in enumerate(outs):
        @pl.when(j == t)
        def _(ref=ref, packed_ref=packed_ref, roped=roped):
            z = jnp.dot(h_ref[...], w_ref[...], preferred_element_type=F32)
            v = _rope_tile(z, cos_ref[...], s1_ref[...], s2_ref[...]) if roped else z
            for h in range(KV_HEADS):
                ref[pl.ds(h, z.shape[0], stride=KV_HEADS), :] = v[:, h * HEAD_DIM:(h + 1) * HEAD_DIM]
            if packed_ref is not None:
                packed_ref[...] = v.astype(MM)


def _project_kv(h, w, tabs, tm):
    n, d = h.shape
    cos, s1, s2 = tabs
    tab_blocks = cos.shape[0] // tm
    tab = lambda i, j: (i % tab_blocks, 0)
    return pl.pallas_call(
        _proj_kv_kernel,
        grid=(n // tm, _KV_TILES),
        in_specs=[pl.BlockSpec((tm, d), lambda i, j: (i, 0)),
                  pl.BlockSpec((d, TN), lambda i, j: (0, j)),
                  pl.BlockSpec((tm, LANES), tab), pl.BlockSpec((tm, LANES), tab),
                  pl.BlockSpec((tm, LANES), tab)],
        out_specs=[pl.BlockSpec((tm * KV_HEADS, HEAD_DIM), lambda i, j: (i, 0))] * 6
                  + [pl.BlockSpec((tm, KV_COLS), lambda i, j: (i, 0))] * 4,
        out_shape=[jax.ShapeDtypeStruct((n * KV_HEADS, HEAD_DIM), F32)] * 6
                  + [jax.ShapeDtypeStruct((n, KV_COLS), MM)] * 4,
        compiler_params=_cparams(("parallel", "arbitrary")),
        name="kv_project",
    )(h, w, cos, s1, s2)


def _rope_tables(pos):
    half = ROPE_DIM // 2
    inv = jnp.power(jnp.float32(ROPE_THETA), -jnp.arange(0, ROPE_DIM, 2, dtype=F32) / ROPE_DIM)
    ang = pos.astype(F32)[:, None] * inv[None, :]
    cos, sin = jnp.cos(ang), jnp.sin(ang)
    r = pos.shape[0]
    one = jnp.ones((r, HEAD_DIM - ROPE_DIM), F32)
    zero = jnp.zeros((r, HEAD_DIM - ROPE_DIM), F32)
    zh = jnp.zeros((r, half), F32)
    return (jnp.concatenate([cos, cos, one], axis=1),
            jnp.concatenate([zh, sin, zero], axis=1),
            jnp.concatenate([-sin, zh, zero], axis=1))


def _layout_w_in(w_in):
    c_a = 2 * A_WIDTH + Q_COLS
    c_kv = c_a + 6 * KV_COLS
    g_br, g_ab = w_in[:, c_kv:c_kv + 3 * N_HEADS], w_in[:, c_kv + 3 * N_HEADS:]
    per_head = 3 * Q_PER_KV
    g_br = g_br.reshape(-1, KV_HEADS, per_head)
    g_br = jnp.pad(g_br, ((0, 0), (0, 0), (0, LANES - per_head))).reshape(-1, KV_HEADS * LANES)
    w_a = jnp.concatenate([w_in[:, :c_a], g_ab, g_br], axis=1).astype(MM)
    return w_a, w_in[:, c_a:c_kv].astype(MM)


def _gmlp_kernel(u_ref, v_ref, lng_ref, lnb_ref, mix_ref, bias_ref, o_ref, *maybe_vout):
    u = jax.nn.gelu(u_ref[...])
    v = jax.nn.gelu(v_ref[...])
    mu = jnp.mean(v, axis=-1, keepdims=True)
    vc = v - mu
    var = jnp.mean(vc * vc, axis=-1, keepdims=True)
    v = vc * lax.rsqrt(var + EPS) * lng_ref[...] + lnb_ref[...]
    if maybe_vout:
        maybe_vout[0][...] = v
    for g in range(A_GROUPS):
        sl = slice(g * A_GROUP_DIM, (g + 1) * A_GROUP_DIM)
        s = jnp.dot(mix_ref[g], v[:, sl].astype(MM), preferred_element_type=F32)
        b = bias_ref[g]
        s = s + jnp.concatenate([b] * (A_GROUP_DIM // LANES), axis=1)
        o_ref[:, sl] = (u[:, sl] * s).astype(o_ref.dtype)


def _gmlp(a, ln_g, ln_b, mix, bias, emit_v):
    n = a.shape[0]
    out_shape = [jax.ShapeDtypeStruct((n, A_WIDTH), MM)]
    out_specs = [pl.BlockSpec((CHUNK, A_WIDTH), lambda i: (i, 0))]
    if emit_v:
        out_shape.append(jax.ShapeDtypeStruct((n, A_WIDTH), F32))
        out_specs.append(pl.BlockSpec((CHUNK, A_WIDTH), lambda i: (i, 0)))
    const3 = lambda i: (0, 0, 0)
    return pl.pallas_call(
        _gmlp_kernel,
        grid=(n // CHUNK,),
        in_specs=[pl.BlockSpec((CHUNK, A_WIDTH), lambda i: (i, 0)),
                  pl.BlockSpec((CHUNK, A_WIDTH), lambda i: (i, 1)),
                  pl.BlockSpec((1, A_WIDTH), lambda i: (0, 0)),
                  pl.BlockSpec((1, A_WIDTH), lambda i: (0, 0)),
                  pl.BlockSpec((A_GROUPS, CHUNK, CHUNK), const3),
                  pl.BlockSpec((A_GROUPS, CHUNK, LANES), const3)],
        out_specs=out_specs,
        out_shape=out_shape,
        compiler_params=_cparams(("parallel",)),
        name="gmlp",
    )(a, a, ln_g, ln_b, mix, bias)


def _gmlp_mix(w_s, b_s, t):
    ln = min(t, CHUNK)
    causal = jnp.tril(jnp.ones((ln, ln), dtype=bool))
    w = jnp.where(causal, w_s[:, :ln, :ln], 0.0)
    reps = CHUNK // ln
    eye = jnp.eye(reps, dtype=w.dtype)
    mix = jnp.einsum('ab,gij->gaibj', eye, w).reshape(A_GROUPS, CHUNK, CHUNK)
    bias = jnp.tile(b_s[:, :ln], (1, reps))
    return mix.astype(MM), jnp.broadcast_to(bias[:, :, None], (A_GROUPS, CHUNK, LANES)).astype(F32)


def _compress_kernel(n_pages, pt_ref, *refs):
    k_pages, v_pages = refs[:n_pages], refs[n_pages:2 * n_pages]
    pek_ref, w1k_ref, w2k_ref, pev_ref, w1v_ref, w2v_ref, kc_ref, vc_ref, scr = refs[2 * n_pages:]
    blocks = n_pages * PAGE_SIZE // CMP_BLOCK
    half = blocks // 2

    page_rows = PAGE_SIZE * KV_HEADS
    blk_rows = CMP_BLOCK * KV_HEADS

    region = half * CMP_PITCH

    def one(pages, pe_ref, w1_ref, w2_ref, out_ref):
        for p in range(n_pages):
            page = pages[p][0, 0] + pe_ref[...]
            for bl in range(PAGE_SIZE // CMP_BLOCK):
                m = p * (PAGE_SIZE // CMP_BLOCK) + bl
                off = (m % 2) * region + (m // 2) * CMP_PITCH
                scr[off:off + blk_rows, :] = page[bl * blk_rows:(bl + 1) * blk_rows, :]
        acc = jnp.zeros((KV_HEADS * blocks, HEAD_DIM), F32)
        for jp in range(CMP_BLOCK // 2):
            parts = []
            for j in (2 * jp, 2 * jp + 1):
                rows = [scr[pl.ds(parity * region + j * KV_HEADS + h, half, stride=CMP_PITCH), :]
                        for h in range(KV_HEADS) for parity in (0, 1)]
                parts.append(jnp.concatenate(rows, axis=0).astype(MM))
            lhs = jnp.concatenate(parts, axis=1)
            acc = acc + jnp.dot(lhs, w1_ref[jp], preferred_element_type=F32)
        hid = jax.nn.gelu(acc)
        out = jnp.dot(hid.astype(MM), w2_ref[...], preferred_element_type=F32)
        out_ref[0] = out.reshape(KV_HEADS, blocks, HEAD_DIM)

    one(k_pages, pek_ref, w1k_ref, w2k_ref, kc_ref)
    one(v_pages, pev_ref, w1v_ref, w2v_ref, vc_ref)


def _compress(pool_k, pool_v, layer, page_table, cmp_k, cmp_v):
    n_seq, n_pages = page_table.shape
    blocks = n_pages * PAGE_SIZE // CMP_BLOCK
    page_rows = PAGE_SIZE * KV_HEADS

    def page_spec(p):
        return pl.BlockSpec((1, 1, page_rows, HEAD_DIM), lambda s, pt, p=p: (layer, pt[s, p], 0, 0))

    def prep(c):
        pe, w1, w2 = c
        pe_t = jnp.tile(jnp.repeat(pe, KV_HEADS, axis=0), (PAGE_SIZE // CMP_BLOCK, 1)).astype(F32)
        return pe_t, w1.reshape(CMP_BLOCK // 2, 2 * HEAD_DIM, HEAD_DIM).astype(MM), w2.astype(MM)

    const2 = lambda s, pt: (0, 0)
    const3 = lambda s, pt: (0, 0, 0)
    w_specs = [pl.BlockSpec((page_rows, HEAD_DIM), const2),
               pl.BlockSpec((CMP_BLOCK // 2, 2 * HEAD_DIM, HEAD_DIM), const3),
               pl.BlockSpec((HEAD_DIM, HEAD_DIM), const2)]
    out_spec = pl.BlockSpec((1, KV_HEADS, blocks, HEAD_DIM), lambda s, pt: (s, 0, 0, 0))
    out_shape = jax.ShapeDtypeStruct((n_seq, KV_HEADS, blocks, HEAD_DIM), F32)
    return pl.pallas_call(
        functools.partial(_compress_kernel, n_pages),
        grid_spec=pltpu.PrefetchScalarGridSpec(
            num_scalar_prefetch=1,
            grid=(n_seq,),
            in_specs=[page_spec(p) for p in range(n_pages)] * 2 + w_specs * 2,
            out_specs=[out_spec, out_spec],
            scratch_shapes=[pltpu.VMEM((blocks * CMP_PITCH, HEAD_DIM), F32)]),
        out_shape=[out_shape, out_shape],
        compiler_params=_cparams(("arbitrary",)),
        name="compress",
    )(page_table, *([pool_k] * n_pages), *([pool_v] * n_pages), *prep(cmp_k), *prep(cmp_v))


def _select_rows(imp, qpos, n_blocks):
    j = lax.broadcasted_iota(jnp.int32, (n_blocks, 1), 0)
    forced = (j == 0) | (j == (qpos >> _SLC_SHIFT))
    future = j * SLC_BLOCK > qpos
    score = jnp.where(future, -1.0, imp + jnp.where(forced, FORCE_BONUS, 0.0))
    rank = jnp.zeros(score.shape, jnp.int32)
    for i in range(n_blocks):
        ri = score[i:i + 1, :]
        beats = (ri > score) | ((ri == score) & (i < j))
        rank = rank + beats.astype(jnp.int32)
    return rank < min(N_SELECT, n_blocks)


def _prompt_attn_kernel(q_ref, kc_ref, vc_ref, ks_ref, vs_ref, kw_ref, vw_ref, gbr_ref, o_ref,
                        vst_scr, vwt_scr, bias_scr, acc_scr, ow_scr):
    n = pl.program_id(2)
    seq = ks_ref.shape[0]
    n_tiles = seq // Q_BLOCK
    n_sel_blocks = seq // SLC_BLOCK
    n_cmp = kc_ref.shape[2]
    lane_groups = [slice(g * Q_BLOCK, (g + 1) * Q_BLOCK) for g in range(Q_PER_KV)]

    tiles_per_chunk = SEL_CHUNK // Q_BLOCK

    @pl.when(n == 0)
    def _():
        for kt in range(n_tiles):
            rows = slice(kt * Q_BLOCK, (kt + 1) * Q_BLOCK)
            c, j = divmod(kt, tiles_per_chunk)
            vst_scr[c, :, j * Q_BLOCK:(j + 1) * Q_BLOCK] = vs_ref[rows, :].astype(F32).T.astype(MM)
            vwt_scr[kt] = vw_ref[rows, :].astype(F32).T.astype(MM)

    q = q_ref[...] * (SCALE * LOG2E)
    q_t = jnp.concatenate([q[:, g * HEAD_DIM:(g + 1) * HEAD_DIM].T for g in range(Q_PER_KV)],
                          axis=1).astype(MM)
    qpos = n * Q_BLOCK + lax.broadcasted_iota(jnp.int32, (1, Q_BLOCK), 1)

    win_keys = WINDOW + Q_BLOCK
    w_tile0 = jnp.maximum(n - WINDOW // Q_BLOCK, 0)
    w_start = pl.multiple_of(w_tile0 * Q_BLOCK, Q_BLOCK)
    s_w = jnp.dot(kw_ref[pl.ds(w_start, win_keys), :], q_t, preferred_element_type=F32)
    dist = qpos - (w_start + lax.broadcasted_iota(jnp.int32, (win_keys, 1), 0))
    w_bias = jnp.where((dist >= 0) & (dist < WINDOW), 0.0, NEG)
    ls_w, ps = [], []
    for sl in lane_groups:
        s = s_w[:, sl] + w_bias
        p = jnp.exp2(s - jnp.max(s, axis=0, keepdims=True))
        ls_w.append(jnp.sum(p, axis=0, keepdims=True))
        ps.append(p.astype(MM))
    p_w = jnp.concatenate(ps, axis=1)
    ow_t = functools.reduce(
        lambda a, b: a + b,
        [jnp.dot(vwt_scr[w_tile0 + j], p_w[j * Q_BLOCK:(j + 1) * Q_BLOCK, :], preferred_element_type=F32)
         for j in range(win_keys // Q_BLOCK)])
    ow_scr[...] = ow_t

    s_c = jnp.dot(kc_ref[0, 0].astype(MM), q_t, preferred_element_type=F32)
    r = lax.broadcasted_iota(jnp.int32, (n_cmp, 1), 0)
    blk = jnp.where(r < n_cmp // 2, 2 * r, 2 * (r - n_cmp // 2) + 1)
    ok = ((blk + 1) * CMP_BLOCK - 1) <= qpos
    imp = jnp.zeros((n_cmp, Q_BLOCK), F32)
    probs = []
    for sl in lane_groups:
        s = jnp.where(ok, s_c[:, sl], NEG)
        e = jnp.exp2(s - jnp.max(s, axis=0, keepdims=True))
        p = jnp.where(ok, e / jnp.sum(e, axis=0, keepdims=True), 0.0)
        imp = imp + p
        probs.append(p.astype(MM))
    oc_t = jnp.dot(vc_ref[0, 0].T.astype(MM), jnp.concatenate(probs, axis=1), preferred_element_type=F32)

    sel = _select_rows(imp[:n_cmp // 2] + imp[n_cmp // 2:], qpos, n_sel_blocks)
    sel_bias = jnp.where(sel, 0.0, NEG)
    blocks_per_chunk = SEL_CHUNK // SLC_BLOCK
    key_row = lax.broadcasted_iota(jnp.int32, (SEL_CHUNK, 1), 0)
    n_chunks = lax.div(n, tiles_per_chunk) + 1
    for c in range(seq // SEL_CHUNK):
        @pl.when(c < n_chunks)
        def _(c=c):
            tile = jnp.concatenate(
                [jnp.broadcast_to(sel_bias[c * blocks_per_chunk + i:c * blocks_per_chunk + i + 1, :],
                                  (SLC_BLOCK, Q_BLOCK)) for i in range(blocks_per_chunk)], axis=0)
            bias_scr[c] = jnp.where(c * SEL_CHUNK + key_row <= qpos, tile, NEG)

    acc_scr[...] = jnp.zeros_like(acc_scr)

    def body(c, carry):
        ms, ls = carry
        start = pl.multiple_of(c * SEL_CHUNK, SEL_CHUNK)
        s_t = jnp.dot(ks_ref[pl.ds(start, SEL_CHUNK), :], q_t, preferred_element_type=F32)
        bias = bias_scr[c]
        new_ms, new_ls, ps, alphas = [], [], [], []
        for g, sl in enumerate(lane_groups):
            s = s_t[:, sl] + bias
            m_new = jnp.maximum(ms[g], jnp.max(s, axis=0, keepdims=True))
            alpha = jnp.exp2(ms[g] - m_new)
            p = jnp.exp2(s - m_new)
            new_ls.append(alpha * ls[g] + jnp.sum(p, axis=0, keepdims=True))
            new_ms.append(m_new)
            alphas.append(alpha)
            ps.append(p.astype(MM))
        pv = jnp.dot(vst_scr[c], jnp.concatenate(ps, axis=1), preferred_element_type=F32)
        for g, sl in enumerate(lane_groups):
            acc_scr[:, sl] = alphas[g] * acc_scr[:, sl] + pv[:, sl]
        return tuple(new_ms), tuple(new_ls)

    init = (tuple(jnp.full((1, Q_BLOCK), NEG, F32) for _ in lane_groups),
            tuple(jnp.zeros((1, Q_BLOCK), F32) for _ in lane_groups))
    _, ls_s = lax.fori_loop(0, n_chunks, body, init)

    gate_t = jax.nn.sigmoid(gbr_ref[...]).T
    for g, sl in enumerate(lane_groups):
        o_t = (oc_t[:, sl] * gate_t[3 * g:3 * g + 1, :]
               + (acc_scr[:, sl] / ls_s[g]) * gate_t[3 * g + 1:3 * g + 2, :]
               + (ow_scr[:, sl] / ls_w[g]) * gate_t[3 * g + 2:3 * g + 3, :])
        o_ref[:, g * HEAD_DIM:(g + 1) * HEAD_DIM] = o_t.T.astype(o_ref.dtype)


def _prompt_attention(a, kc, vc, ks, vs, kw, vw, gt, batch, seq):
    n_qb = seq // Q_BLOCK
    q_col0 = 2 * A_WIDTH // (Q_PER_KV * HEAD_DIM)
    gbr_col0 = 2 * D_MODEL // LANES
    cmp_spec = pl.BlockSpec((1, 1, kc.shape[2], HEAD_DIM), lambda b, h, n: (b, h, 0, 0))
    kv_spec = pl.BlockSpec((seq, HEAD_DIM), lambda b, h, n: (b, h))
    assert seq % SEL_CHUNK == 0 and seq >= WINDOW + Q_BLOCK
    scratch = [pltpu.VMEM((seq // SEL_CHUNK, HEAD_DIM, SEL_CHUNK), MM),
               pltpu.VMEM((n_qb, HEAD_DIM, Q_BLOCK), MM),
               pltpu.VMEM((seq // SEL_CHUNK, SEL_CHUNK, Q_BLOCK), F32),
               pltpu.VMEM((HEAD_DIM, Q_PER_KV * Q_BLOCK), F32),
               pltpu.VMEM((HEAD_DIM, Q_PER_KV * Q_BLOCK), F32)]
    return pl.pallas_call(
        _prompt_attn_kernel,
        grid=(batch, KV_HEADS, n_qb),
        in_specs=[pl.BlockSpec((Q_BLOCK, Q_PER_KV * HEAD_DIM), lambda b, h, n: (b * n_qb + n, q_col0 + h)),
                  cmp_spec, cmp_spec, kv_spec, kv_spec, kv_spec, kv_spec,
                  pl.BlockSpec((Q_BLOCK, LANES), lambda b, h, n: (b * n_qb + n, gbr_col0 + h))],
        out_specs=pl.BlockSpec((Q_BLOCK, Q_PER_KV * HEAD_DIM), lambda b, h, n: (b * n_qb + n, h)),
        out_shape=jax.ShapeDtypeStruct((batch * seq, Q_COLS), MM),
        scratch_shapes=scratch,
        compiler_params=_cparams(("parallel", "parallel", "arbitrary")),
        name="prompt_attention",
    )(a, kc, vc, ks, vs, kw, vw, gt)


def _dot_nt(a, b):
    return lax.dot_general(a, b, (((1,), (1,)), ((), ())), preferred_element_type=F32)


def _select_cols(imp, qpos, n_blocks):
    j = lax.broadcasted_iota(jnp.int32, (1, n_blocks), 1)
    forced = (j == 0) | (j == (qpos >> _SLC_SHIFT))
    future = j * SLC_BLOCK > qpos
    score = jnp.where(future, -1.0, imp + jnp.where(forced, FORCE_BONUS, 0.0))
    rank = jnp.zeros(score.shape, jnp.int32)
    for i in range(n_blocks):
        ci = score[:, i:i + 1]
        beats = (ci > score) | ((ci == score) & (i < j))
        rank = rank + beats.astype(jnp.int32)
    return (rank < min(N_SELECT, n_blocks)).astype(F32)


def _softmax_parts(parts):
    m = functools.reduce(jnp.maximum, [jnp.max(s, axis=1, keepdims=True) for s, _ in parts])
    es = [jnp.exp2(s - m) for s, _ in parts]
    den = functools.reduce(lambda a, b: a + b, [jnp.sum(e, axis=1, keepdims=True) for e in es])
    acc = None
    for e, (_, v) in zip(es, parts):
        o = jnp.dot((e / den).astype(MM), v, preferred_element_type=F32)
        acc = o if acc is None else acc + o
    return acc


def _sample_attn_kernel(n_pages, pt_ref, *refs):
    ks_pages, vs_pages = refs[:n_pages], refs[n_pages:2 * n_pages]
    (q_ref, kc_ref, vc_ref, kbuf_ref, vbuf_ref, ksn_ref, vsn_ref, kwn_ref, vwn_ref, gbr_ref,
     o_ref, kwo_ref, vwo_ref) = refs[2 * n_pages:]
    t = q_ref.shape[1]
    past = n_pages * PAGE_SIZE
    nb_past = past // SLC_BLOCK
    buf_rows = kbuf_ref.shape[2]
    wb = buf_rows // KV_HEADS
    new_rows = t * KV_HEADS
    hq = KV_HEADS * t
    n_cmp = kc_ref.shape[2]
    head_shift, t_shift, cmp_shift = (v.bit_length() - 1 for v in (KV_HEADS, t, n_cmp))
    assert (KV_HEADS, t, n_cmp) == (1 << head_shift, 1 << t_shift, 1 << cmp_shift) and LANES % KV_HEADS == 0

    q = q_ref[0] * (SCALE * LOG2E)
    qb = jnp.concatenate([q[:, (h * Q_PER_KV + g) * HEAD_DIM:(h * Q_PER_KV + g + 1) * HEAD_DIM]
                          for g in range(Q_PER_KV) for h in range(KV_HEADS)], axis=0).astype(MM)
    ri = lax.broadcasted_iota(jnp.int32, (Q_PER_KV * hq, 1), 0)
    row_h = (ri >> t_shift) & (KV_HEADS - 1)
    row_t = ri & (t - 1)
    qpos = past + row_t

    def key_cols(n):
        c = lax.broadcasted_iota(jnp.int32, (1, n), 1)
        return c & (KV_HEADS - 1), c >> head_shift

    kwo_ref[0, 0, :buf_rows - new_rows] = kbuf_ref[0, 0, new_rows:]
    kwo_ref[0, 0, buf_rows - new_rows:] = kwn_ref[0]
    vwo_ref[0, 0, :buf_rows - new_rows] = vbuf_ref[0, 0, new_rows:]
    vwo_ref[0, 0, buf_rows - new_rows:] = vwn_ref[0]

    kc = kc_ref[0].reshape(KV_HEADS * n_cmp, HEAD_DIM).astype(MM)
    vc = vc_ref[0].reshape(KV_HEADS * n_cmp, HEAD_DIM).astype(MM)
    col = lax.broadcasted_iota(jnp.int32, (1, KV_HEADS * n_cmp), 1)
    slot = col & (n_cmp - 1)
    blk = jnp.where(slot < n_cmp // 2, 2 * slot, 2 * (slot - n_cmp // 2) + 1)
    ok = ((col >> cmp_shift) == row_h) & (((blk + 1) * CMP_BLOCK - 1) <= qpos)
    s = jnp.where(ok, _dot_nt(qb, kc), NEG)
    e = jnp.exp2(s - jnp.max(s, axis=1, keepdims=True))
    p = jnp.where(ok, e / jnp.sum(e, axis=1, keepdims=True), 0.0)
    o_c = jnp.dot(p.astype(MM), vc, preferred_element_type=F32)
    p = functools.reduce(lambda a, b: a + b, [p[:, h * n_cmp:(h + 1) * n_cmp] for h in range(KV_HEADS)])
    p = functools.reduce(lambda a, b: a + b, [p[g * hq:(g + 1) * hq] for g in range(Q_PER_KV)])
    imp = jnp.concatenate([p[:, :n_cmp // 2] + p[:, n_cmp // 2:], jnp.zeros((hq, 1), F32)], axis=1)
    sel = _select_cols(imp, qpos[:hq], nb_past + 1)
    sel = jnp.concatenate([sel] * Q_PER_KV, axis=0)

    page_h, page_r = key_cols(PAGE_SIZE * KV_HEADS)
    own_head = page_h == row_h
    page_blk = page_r >> _SLC_SHIFT
    per_page = PAGE_SIZE // SLC_BLOCK
    parts = []
    for pg in range(n_pages):
        ok = own_head & functools.reduce(
            lambda a, b: a | b,
            [(page_blk == c) & (sel[:, pg * per_page + c:pg * per_page + c + 1] > 0.5) for c in range(per_page)])
        parts.append((jnp.where(ok, _dot_nt(qb, ks_pages[pg][0, 0].astype(MM)), NEG),
                      vs_pages[pg][0, 0].astype(MM)))
    new_h, new_t = key_cols(new_rows)
    new_ok = (new_h == row_h) & (new_t <= row_t)
    parts.append((jnp.where(new_ok & (sel[:, nb_past:nb_past + 1] > 0.5),
                            _dot_nt(qb, ksn_ref[0].astype(MM)), NEG), vsn_ref[0].astype(MM)))
    o_s = _softmax_parts(parts)

    buf_h, buf_r = key_cols(buf_rows)
    buf_pos = past - wb + buf_r
    dist = qpos - buf_pos
    buf_ok = (buf_h == row_h) & (dist >= 0) & (dist < WINDOW) & (buf_pos >= 0)
    o_w = _softmax_parts([
        (jnp.where(buf_ok, _dot_nt(qb, kbuf_ref[0, 0].astype(MM)), NEG), vbuf_ref[0, 0].astype(MM)),
        (jnp.where(new_ok, _dot_nt(qb, kwn_ref[0].astype(MM)), NEG), vwn_ref[0].astype(MM))])

    gate = jax.nn.sigmoid(gbr_ref[0])
    for g in range(Q_PER_KV):
        for h in range(KV_HEADS):
            rs = slice(g * hq + h * t, g * hq + (h + 1) * t)
            c0 = h * LANES + 3 * g
            o = (o_c[rs] * gate[:, c0:c0 + 1] + o_s[rs] * gate[:, c0 + 1:c0 + 2]
                 + o_w[rs] * gate[:, c0 + 2:c0 + 3])
            col0 = (h * Q_PER_KV + g) * HEAD_DIM
            o_ref[0, :, col0:col0 + HEAD_DIM] = o.astype(o_ref.dtype)


def _sample_attention(a3, kc, vc, pool_ks, pool_vs, layer, page_table, kbuf, vbuf, ksn, vsn, kwn, vwn, gt3):
    db, t, _ = a3.shape
    n_pages = page_table.shape[1]
    buf_rows = kbuf.shape[2]
    page_rows = PAGE_SIZE * KV_HEADS

    def page_spec(p):
        return pl.BlockSpec((1, 1, page_rows, HEAD_DIM), lambda s, pt, p=p: (layer, pt[s, p], 0, 0))

    row3 = lambda s, pt: (s, 0, 0)
    cmp_spec = pl.BlockSpec((1, KV_HEADS, kc.shape[2], HEAD_DIM), lambda s, pt: (s, 0, 0, 0))
    buf_spec = pl.BlockSpec((1, 1, buf_rows, HEAD_DIM), lambda s, pt: (layer, s, 0, 0))
    out_buf_spec = pl.BlockSpec((1, 1, buf_rows, HEAD_DIM), lambda s, pt: (0, s, 0, 0))
    new_spec = pl.BlockSpec((1, t * KV_HEADS, HEAD_DIM), row3)
    buf_shape = jax.ShapeDtypeStruct((1, db, buf_rows, HEAD_DIM), F32)
    return pl.pallas_call(
        functools.partial(_sample_attn_kernel, n_pages),
        grid_spec=pltpu.PrefetchScalarGridSpec(
            num_scalar_prefetch=1,
            grid=(db,),
            in_specs=[page_spec(p) for p in range(n_pages)] * 2
                     + [pl.BlockSpec((1, t, Q_COLS), lambda s, pt: (s, 0, 2 * A_WIDTH // Q_COLS)),
                        cmp_spec, cmp_spec, buf_spec, buf_spec, new_spec, new_spec, new_spec, new_spec,
                        pl.BlockSpec((1, t, KV_HEADS * LANES),
                                     lambda s, pt: (s, 0, 2 * D_MODEL // (KV_HEADS * LANES)))],
            out_specs=[pl.BlockSpec((1, t, Q_COLS), row3), out_buf_spec, out_buf_spec]),
        out_shape=[jax.ShapeDtypeStruct((db, t, Q_COLS), MM), buf_shape, buf_shape],
        compiler_params=_cparams(("arbitrary",)),
        name="sample_attention",
    )(page_table, *([pool_ks] * n_pages), *([pool_vs] * n_pages),
      a3, kc, vc, kbuf, vbuf, ksn, vsn, kwn, vwn, gt3)


def _mix_kernel(a_ref, o_ref, wpa_ref, wpb_ref, ga_ref, gb_ref, m_ref):
    y_a = jnp.dot(a_ref[...], wpa_ref[...], preferred_element_type=F32)
    y_b = jnp.dot(o_ref[...], wpb_ref[...], preferred_element_type=F32)
    m = jax.nn.sigmoid(ga_ref[...]) * y_a + jax.nn.sigmoid(gb_ref[...]) * y_b
    m_ref[...] = m.astype(m_ref.dtype)


def _mix(a_out, o, w_pa, w_pb, gt, tm):
    n = a_out.shape[0]
    gb0 = D_MODEL // TN
    return pl.pallas_call(
        _mix_kernel,
        grid=(n // tm, D_MODEL // TN),
        in_specs=[pl.BlockSpec((tm, A_WIDTH), lambda i, j: (i, 0)),
                  pl.BlockSpec((tm, Q_COLS), lambda i, j: (i, 0)),
                  pl.BlockSpec((A_WIDTH, TN), lambda i, j: (0, j)),
                  pl.BlockSpec((Q_COLS, TN), lambda i, j: (0, j)),
                  pl.BlockSpec((tm, TN), lambda i, j: (i, j)),
                  pl.BlockSpec((tm, TN), lambda i, j: (i, gb0 + j))],
        out_specs=pl.BlockSpec((tm, TN), lambda i, j: (i, j)),
        out_shape=jax.ShapeDtypeStruct((n, D_MODEL), MM),
        compiler_params=_cparams(("parallel", "arbitrary")),
        name="merge_gate",
    )(a_out, o, w_pa, w_pb, gt, gt)


def _out_proj_kernel(m_ref, w_ref, x_ref, o_ref):
    o_ref[...] = x_ref[...] + jnp.dot(m_ref[...], w_ref[...], preferred_element_type=F32)


def _out_proj(m, w_o, x2d, tm):
    n = m.shape[0]
    return pl.pallas_call(
        _out_proj_kernel,
        grid=(n // tm, D_MODEL // TN),
        in_specs=[pl.BlockSpec((tm, D_MODEL), lambda i, j: (i, 0)),
                  pl.BlockSpec((D_MODEL, TN), lambda i, j: (0, j)),
                  pl.BlockSpec((tm, TN), lambda i, j: (i, j))],
        out_specs=pl.BlockSpec((tm, TN), lambda i, j: (i, j)),
        out_shape=jax.ShapeDtypeStruct((n, D_MODEL), F32),
        compiler_params=_cparams(("parallel", "arbitrary")),
        name="out_proj",
    )(m, w_o, x2d)


def _normed_logits(x, gf_ref, wr_hi_ref, wr_lo_ref, br_ref):
    ms = jnp.mean(x * x, axis=-1, keepdims=True)
    h = x * lax.rsqrt(ms + EPS) * gf_ref[...]
    hi = h.astype(MM)
    lo = (h - hi.astype(F32)).astype(MM)
    logits = (jnp.dot(hi, wr_hi_ref[...], preferred_element_type=F32)
              + (jnp.dot(hi, wr_lo_ref[...], preferred_element_type=F32)
                 + jnp.dot(lo, wr_hi_ref[...], preferred_element_type=F32))) + br_ref[...]
    return hi, logits


def _route(logits):
    lane = lax.broadcasted_iota(jnp.int32, logits.shape, 1)
    big = jnp.int32(LANES)
    is_g = lane < N_GROUPS
    gl = jnp.where(is_g, logits, -jnp.inf)
    gm = jnp.max(gl, axis=1, keepdims=True)
    ge = jnp.exp(gl - gm)
    pg = ge / jnp.sum(ge, axis=1, keepdims=True)
    pg_top = jnp.max(pg, axis=1, keepdims=True)
    g_idx = jnp.min(jnp.where(is_g & (pg == pg_top), lane, big), axis=1, keepdims=True)
    in_grp = (lane >= N_GROUPS) & (((lane - N_GROUPS) >> 3) == g_idx)
    el = jnp.where(in_grp, logits, -jnp.inf)
    em = jnp.max(el, axis=1, keepdims=True)
    ee = jnp.exp(el - em)
    pe = ee / jnp.sum(ee, axis=1, keepdims=True)
    p1 = jnp.max(pe, axis=1, keepdims=True)
    i1 = jnp.min(jnp.where(in_grp & (pe == p1), lane, big), axis=1, keepdims=True)
    rest = in_grp & (lane != i1)
    p2 = jnp.max(jnp.where(rest, pe, -1.0), axis=1, keepdims=True)
    i2 = jnp.min(jnp.where(rest & (pe == p2), lane, big), axis=1, keepdims=True)
    tot = p1 + p2
    return i1, i2, pg_top * p1 / tot, pg_top * p2 / tot


def _moe_kernel(final_norm, x_ref, gf_ref, wr_hi_ref, wr_lo_ref, br_ref, wg_ref, wu_ref, wd_ref, gfin_ref,
                y_ref, h_scr, gate_scr, acc_scr):
    e = pl.program_id(1)

    @pl.when(e == 0)
    def _():
        hi, logits = _normed_logits(x_ref[...], gf_ref, wr_hi_ref, wr_lo_ref, br_ref)
        i1, i2, w1, w2 = _route(logits)
        lane = lax.broadcasted_iota(jnp.int32, logits.shape, 1)
        h_scr[...] = hi
        gate_scr[...] = jnp.where(lane == i1, w1, 0.0) + jnp.where(lane == i2, w2, 0.0)
        acc_scr[...] = jnp.zeros_like(acc_scr)

    h = h_scr[...]
    lane = lax.broadcasted_iota(jnp.int32, gate_scr.shape, 1)
    gcol = jnp.sum(jnp.where(lane == e + N_GROUPS, gate_scr[...], 0.0), axis=1, keepdims=True)
    hid = (jax.nn.silu(jnp.dot(h, wg_ref[0], preferred_element_type=F32))
           * jnp.dot(h, wu_ref[0], preferred_element_type=F32))
    acc_scr[...] += jnp.dot((hid * gcol).astype(MM), wd_ref[0], preferred_element_type=F32)

    @pl.when(e == pl.num_programs(1) - 1)
    def _():
        x2 = x_ref[...] + acc_scr[...]
        if final_norm:
            ms = jnp.mean(x2 * x2, axis=-1, keepdims=True)
            x2 = x2 * lax.rsqrt(ms + EPS) * gfin_ref[...]
        y_ref[...] = x2


def _moe(x1, g_ffn, wr_hi, wr_lo, b_r, w_gate, w_up, w_down, g_final, tm, final_norm):
    n = x1.shape[0]
    row = lambda i, e: (i, 0)
    const = lambda i, e: (0, 0)
    return pl.pallas_call(
        functools.partial(_moe_kernel, final_norm),
        grid=(n // tm, N_EXPERTS),
        in_specs=[pl.BlockSpec((tm, D_MODEL), row),
                  pl.BlockSpec((1, D_MODEL), const),
                  pl.BlockSpec((D_MODEL, LANES), const),
                  pl.BlockSpec((D_MODEL, LANES), const),
                  pl.BlockSpec((1, LANES), const),
                  pl.BlockSpec((1, D_MODEL, EXPERT_DIM), lambda i, e: (e, 0, 0)),
                  pl.BlockSpec((1, D_MODEL, EXPERT_DIM), lambda i, e: (e, 0, 0)),
                  pl.BlockSpec((1, EXPERT_DIM, D_MODEL), lambda i, e: (e, 0, 0)),
                  pl.BlockSpec((1, D_MODEL), const)],
        out_specs=pl.BlockSpec((tm, D_MODEL), row),
        out_shape=jax.ShapeDtypeStruct((n, D_MODEL), F32),
        scratch_shapes=[pltpu.VMEM((tm, D_MODEL), MM), pltpu.VMEM((tm, LANES), F32),
                        pltpu.VMEM((tm, D_MODEL), F32)],
        compiler_params=_cparams(("parallel", "arbitrary")),
        name="moe_final_norm",
    )(x1, g_ffn, wr_hi, wr_lo, b_r, w_gate, w_up, w_down, g_final)


def _router_weights(w_rg, b_rg, w_re, b_re):
    w = jnp.concatenate([w_rg, w_re], axis=1)
    w = jnp.pad(w, ((0, 0), (0, LANES - w.shape[1])))
    b = jnp.pad(jnp.concatenate([b_rg, b_re]), (0, LANES - N_GROUPS - N_EXPERTS))[None, :]
    hi = w.astype(MM)
    lo = (w - hi.astype(F32)).astype(MM)
    return hi, lo, b.astype(F32)


def _router_kernel(x_ref, gf_ref, wr_hi_ref, wr_lo_ref, br_ref, tril_ref, meta_ref, cnt_ref, base_scr):
    @pl.when(pl.program_id(0) == 0)
    def _():
        base_scr[...] = jnp.zeros_like(base_scr)

    _, logits = _normed_logits(x_ref[...], gf_ref, wr_hi_ref, wr_lo_ref, br_ref)
    i1, i2, w1, w2 = _route(logits)
    lane = lax.broadcasted_iota(jnp.int32, logits.shape, 1)
    hit1, hit2 = lane == i1, lane == i2
    chosen = jnp.where(hit1 | hit2, 1.0, 0.0)
    before = jnp.dot(tril_ref[...], chosen.astype(MM), preferred_element_type=F32) + base_scr[...]
    r1 = jnp.sum(jnp.where(hit1, before, 0.0), axis=1, keepdims=True)
    r2 = jnp.sum(jnp.where(hit2, before, 0.0), axis=1, keepdims=True)
    base_scr[...] += jnp.sum(chosen, axis=0, keepdims=True)
    cols = ((i1 - N_GROUPS).astype(F32), (i2 - N_GROUPS).astype(F32), w1, w2, r1, r2)
    meta_ref[...] = functools.reduce(lambda a, b: a + b,
                                     [jnp.where(lane == k, c, 0.0) for k, c in enumerate(cols)])
    cnt_ref[...] = base_scr[...]


def _router(x1, g_ffn, wr_hi, wr_lo, b_r, tm):
    n = x1.shape[0]
    const = lambda i: (0, 0)
    tril = jnp.tril(jnp.ones((tm, tm), MM), -1)
    return pl.pallas_call(
        _router_kernel,
        grid=(n // tm,),
        in_specs=[pl.BlockSpec((tm, D_MODEL), lambda i: (i, 0)),
                  pl.BlockSpec((1, D_MODEL), const),
                  pl.BlockSpec((D_MODEL, LANES), const),
                  pl.BlockSpec((D_MODEL, LANES), const),
                  pl.BlockSpec((1, LANES), const),
                  pl.BlockSpec((tm, tm), const)],
        out_specs=[pl.BlockSpec((tm, LANES), lambda i: (i, 0)), pl.BlockSpec((1, LANES), const)],
        out_shape=[jax.ShapeDtypeStruct((n, LANES), F32), jax.ShapeDtypeStruct((1, LANES), F32)],
        scratch_shapes=[pltpu.VMEM((1, LANES), F32)],
        compiler_params=_cparams(("arbitrary",)),
        name="moe_router",
    )(x1, g_ffn, wr_hi, wr_lo, b_r, tril)


def _slots_kernel(meta_ref, offs_ref, s_ref):
    meta = meta_ref[...]
    lane = lax.broadcasted_iota(jnp.int32, meta.shape, 1)

    def slot(e_col, r_col):
        e_lane = meta[:, e_col:e_col + 1].astype(jnp.int32) + N_GROUPS
        first = jnp.sum(jnp.where(lane == e_lane, offs_ref[...], 0.0), axis=1, keepdims=True)
        return first + meta[:, r_col:r_col + 1]

    both = jnp.where(lane == 0, slot(0, 4), 0.0) + jnp.where(lane == 1, slot(1, 5), 0.0)
    s_ref[0] = both.T[:SLOT_ROWS, :].astype(jnp.int32)


def _slots(meta, offs_lanes, tb):
    n = meta.shape[0]
    return pl.pallas_call(
        _slots_kernel,
        grid=(n // tb,),
        in_specs=[pl.BlockSpec((tb, LANES), lambda i: (i, 0)), pl.BlockSpec((1, LANES), lambda i: (0, 0))],
        out_specs=pl.BlockSpec((1, SLOT_ROWS, tb), lambda i: (i, 0, 0)),
        out_shape=jax.ShapeDtypeStruct((n // tb, SLOT_ROWS, tb), jnp.int32),
        compiler_params=_cparams(("parallel",)),
        name="moe_slots",
    )(meta, offs_lanes)


def _dispatch_kernel(fill_lo_ref, fill_hi_ref, x_ref, s_ref, xs_ref, sem):
    tb = x_ref.shape[0]

    def row_copy(r, slot):
        return pltpu.make_async_copy(x_ref.at[pl.ds(r, 1)], xs_ref.at[pl.ds(slot, 1)], sem)

    def start(r, c):
        row_copy(r, s_ref[0, 0, r]).start(priority=0)
        row_copy(r, s_ref[0, 1, r]).start(priority=1)
        return c

    def wait(r, c):
        row_copy(r, s_ref[0, 0, r]).wait()
        row_copy(r, s_ref[0, 1, r]).wait()
        return c

    lax.fori_loop(0, tb, start, 0)

    @pl.when(pl.program_id(0) == 0)
    def _():
        def fill_start(s, c):
            row_copy(0, s).start()
            return c

        def fill_wait(s, c):
            row_copy(0, s).wait()
            return c

        for e in range(N_EXPERTS):
            lax.fori_loop(fill_lo_ref[e], fill_hi_ref[e], fill_start, 0)
        for e in range(N_EXPERTS):
            lax.fori_loop(fill_lo_ref[e], fill_hi_ref[e], fill_wait, 0)

    lax.fori_loop(0, tb, wait, 0)


def _dispatch(x1, slots, fill_lo, fill_hi, n_slots):
    n, d = x1.shape
    tb = slots.shape[2]
    return pl.pallas_call(
        _dispatch_kernel,
        grid_spec=pltpu.PrefetchScalarGridSpec(
            num_scalar_prefetch=2,
            grid=(n // tb,),
            in_specs=[pl.BlockSpec((tb, d), lambda i, lo, hi: (i, 0)),
                      pl.BlockSpec((1, SLOT_ROWS, tb), lambda i, lo, hi: (i, 0, 0), memory_space=pltpu.SMEM)],
            out_specs=pl.BlockSpec(memory_space=pl.ANY),
            scratch_shapes=[pltpu.SemaphoreType.DMA(())]),
        out_shape=jax.ShapeDtypeStruct((n_slots, d), F32),
        compiler_params=_cparams(("arbitrary",)),
        name="moe_dispatch",
    )(fill_lo, fill_hi, x1, slots)


def _expert_kernel(te_ref, tv_ref, xs_ref, gf_ref, wg_ref, wu_ref, wd_ref, ys_ref):
    used = tv_ref[pl.program_id(0)] == 1

    @pl.when(jnp.logical_not(used))
    def _():
        ys_ref[...] = jnp.zeros_like(ys_ref)

    @pl.when(used)
    def _():
        x = xs_ref[...]
        ms = jnp.mean(x * x, axis=-1, keepdims=True)
        h = (x * lax.rsqrt(ms + EPS) * gf_ref[...]).astype(MM)
        hid = (jax.nn.silu(jnp.dot(h, wg_ref[0], preferred_element_type=F32))
               * jnp.dot(h, wu_ref[0], preferred_element_type=F32))
        ys_ref[...] = jnp.dot(hid.astype(MM), wd_ref[0], preferred_element_type=F32)


def _experts(xs, tile_expert, tile_valid, g_ffn, w_gate, w_up, w_down):
    n_slots, d = xs.shape
    tile = lambda i, te, tv: (i, 0)
    weight = lambda i, te, tv: (te[i], 0, 0)
    return pl.pallas_call(
        _expert_kernel,
        grid_spec=pltpu.PrefetchScalarGridSpec(
            num_scalar_prefetch=2,
            grid=(tile_expert.shape[0],),
            in_specs=[pl.BlockSpec((EXPERT_TILE, d), tile),
                      pl.BlockSpec((1, d), lambda i, te, tv: (0, 0)),
                      pl.BlockSpec((1, d, EXPERT_DIM), weight),
                      pl.BlockSpec((1, d, EXPERT_DIM), weight),
                      pl.BlockSpec((1, EXPERT_DIM, d), weight)],
            out_specs=pl.BlockSpec((EXPERT_TILE, d), tile)),
        out_shape=jax.ShapeDtypeStruct((n_slots, d), F32),
        compiler_params=_cparams(("arbitrary",)),
        name="moe_experts",
    )(tile_expert, tile_valid, xs, g_ffn, w_gate, w_up, w_down)


def _combine_kernel(final_norm, x_ref, meta_ref, s_ref, ys_ref, gfin_ref, y_ref, buf1, buf2, sem):
    tb = x_ref.shape[0]

    def row_copy(buf, r, slot):
        return pltpu.make_async_copy(ys_ref.at[pl.ds(slot, 1)], buf.at[pl.ds(r, 1)], sem)

    def start(r, c):
        row_copy(buf1, r, s_ref[0, 0, r]).start(priority=0)
        row_copy(buf2, r, s_ref[0, 1, r]).start(priority=1)
        return c

    def wait(r, c):
        row_copy(buf1, r, s_ref[0, 0, r]).wait()
        row_copy(buf2, r, s_ref[0, 1, r]).wait()
        return c

    lax.fori_loop(0, tb, start, 0)
    lax.fori_loop(0, tb, wait, 0)
    meta = meta_ref[...]
    x2 = x_ref[...] + (meta[:, 2:3] * buf1[...] + meta[:, 3:4] * buf2[...])
    if final_norm:
        ms = jnp.mean(x2 * x2, axis=-1, keepdims=True)
        x2 = x2 * lax.rsqrt(ms + EPS) * gfin_ref[...]
    y_ref[...] = x2


def _combine(x1, meta, slots, ys, g_final, final_norm):
    n, d = x1.shape
    tb = slots.shape[2]
    return pl.pallas_call(
        functools.partial(_combine_kernel, final_norm),
        grid=(n // tb,),
        in_specs=[pl.BlockSpec((tb, d), lambda i: (i, 0)),
                  pl.BlockSpec((tb, LANES), lambda i: (i, 0)),
                  pl.BlockSpec((1, SLOT_ROWS, tb), lambda i: (i, 0, 0), memory_space=pltpu.SMEM),
                  pl.BlockSpec(memory_space=pl.ANY),
                  pl.BlockSpec((1, d), lambda i: (0, 0))],
        out_specs=pl.BlockSpec((tb, d), lambda i: (i, 0)),
        out_shape=jax.ShapeDtypeStruct((n, d), F32),
        scratch_shapes=[pltpu.VMEM((tb, d), F32), pltpu.VMEM((tb, d), F32), pltpu.SemaphoreType.DMA(())],
        compiler_params=_cparams(("arbitrary",)),
        name="moe_combine_norm",
    )(x1, meta, slots, ys, g_final)


def _routed_moe(x1, g_ffn, wr_hi, wr_lo, b_r, w_gate, w_up, w_down, g_final, final_norm):
    n = x1.shape[0]
    meta, cnt = _router(x1, g_ffn, wr_hi, wr_lo, b_r, _row_tile(n))
    counts = cnt[0, N_GROUPS:N_GROUPS + N_EXPERTS].astype(jnp.int32)
    padded = (counts + EXPERT_TILE - 1) // EXPERT_TILE * EXPERT_TILE
    ends = jnp.cumsum(padded)
    offs = ends - padded
    n_tiles = 2 * n // EXPERT_TILE + N_EXPERTS
    n_slots = n_tiles * EXPERT_TILE
    tile_start = jnp.arange(n_tiles, dtype=jnp.int32) * EXPERT_TILE
    tile_expert = jnp.minimum(jnp.sum((tile_start[:, None] >= ends[None, :]).astype(jnp.int32), axis=1),
                              N_EXPERTS - 1)
    tile_valid = (tile_start < ends[-1]).astype(jnp.int32)
    offs_lanes = jnp.pad(offs.astype(F32), (N_GROUPS, LANES - N_GROUPS - N_EXPERTS))[None, :]
    slots = _slots(meta, offs_lanes, _row_tile(n))
    fill_hi = jnp.concatenate([ends[:-1], jnp.full((1,), n_slots, jnp.int32)])
    xs = _dispatch(x1, slots, offs + counts, fill_hi, n_slots)
    ys = _experts(xs, tile_expert, tile_valid, g_ffn, w_gate, w_up, w_down)
    return _combine(x1, meta, slots, ys, g_final, final_norm)


def _row_tile(n, tm=512):
    return tm if n % tm == 0 else n


def kernel(x_prompt, x_sample, cache_k_cmp, cache_v_cmp, cache_k_slc, cache_v_slc, state_k_win, state_v_win, page_table, g_mix, w_in, a_ln_g, a_ln_b, a_w_s, a_b_s, cmp_pe_k, cmp_w1_k, cmp_w2_k, cmp_pe_v, cmp_w1_v, cmp_w2_v, w_pa, w_pb, w_o, g_ffn, w_rg, b_rg, w_re, b_re, w_gate, w_up, w_down, g_final):
    depth = g_mix.shape[0]
    b, s, d = x_prompt.shape
    db, t, _ = x_sample.shape
    n_pages = page_table.shape[1]
    past = n_pages * PAGE_SIZE
    wl = min(WINDOW, s)
    assert s % Q_BLOCK == 0 and CHUNK % t == 0 and (db * t) % CHUNK == 0 and t < CMP_BLOCK

    tabs_p = _rope_tables(jnp.arange(s))
    tabs_s = _rope_tables(past + (jnp.arange(db * t) % t))
    prompt_pages = jnp.arange(b * s // PAGE_SIZE, dtype=jnp.int32).reshape(b, s // PAGE_SIZE)
    g_final2 = g_final[None, :]

    xp = x_prompt.reshape(b * s, d)
    xs = x_sample.reshape(db * t, d)
    outs_p = [[] for _ in range(6)]
    outs_s = [[] for _ in range(7)]
    for l in range(depth):
        w_a, w_kv = _layout_w_in(w_in[l])
        g_l = g_mix[l][None, :]
        ln_g, ln_b = a_ln_g[l][None, :], a_ln_b[l][None, :]
        cmp_k = (cmp_pe_k[l], cmp_w1_k[l], cmp_w2_k[l])
        cmp_v = (cmp_pe_v[l], cmp_w1_v[l], cmp_w2_v[l])
        wpa, wpb, wo = w_pa[l].astype(MM), w_pb[l].astype(MM), w_o[l].astype(MM)
        wr_hi, wr_lo, b_r = _router_weights(w_rg[l], b_rg[l], w_re[l], b_re[l])
        wg, wu, wd = w_gate[l].astype(MM), w_up[l].astype(MM), w_down[l].astype(MM)
        last = l == depth - 1

        def tail(x2d, a_out, o, gt):
            n = x2d.shape[0]
            m = _mix(a_out, o, wpa, wpb, gt, _row_tile(n, 1024))
            x1 = _out_proj(m, wo, x2d, _row_tile(n, 1024))
            moe_args = (x1, g_ffn[l][None, :], wr_hi, wr_lo, b_r, wg, wu, wd, g_final2)
            if n >= ROUTED_MIN_TOKENS:
                return _routed_moe(*moe_args, last)
            return _moe(*moe_args, _row_tile(n), last)

        a, gt, h = _project_a(xp, g_l, w_a, tabs_p, _row_tile(b * s, 1024))
        kc_r, vc_r, ks_r, vs_r, kw_r, vw_r, ks_b, vs_b, kw_b, vw_b = _project_kv(
            h, w_kv, tabs_p, _row_tile(b * s))
        mix_p, bias_p = _gmlp_mix(a_w_s[l], a_b_s[l], s)
        (a_out,) = _gmlp(a, ln_g, ln_b, mix_p, bias_p, False)
        pages = lambda r: r.reshape(1, b * s // PAGE_SIZE, PAGE_SIZE * KV_HEADS, HEAD_DIM)
        kc, vc = _compress(pages(kc_r), pages(vc_r), 0, prompt_pages, cmp_k, cmp_v)
        o = _prompt_attention(a, kc, vc, ks_b, vs_b, kw_b, vw_b, gt, b, s)
        xp = tail(xp, a_out, o, gt)
        heads = lambda r: r.reshape(b, s, KV_HEADS, HEAD_DIM)
        for lst, r in zip(outs_p, (kc_r, vc_r, ks_r, vs_r)):
            lst.append(heads(r))
        outs_p[4].append(heads(kw_r)[:, -wl:])
        outs_p[5].append(heads(vw_r)[:, -wl:])

        a, gt, h = _project_a(xs, g_l, w_a, tabs_s, _row_tile(db * t))
        kc_r, vc_r, ks_r, vs_r, kw_r, vw_r, _, _, _, _ = _project_kv(h, w_kv, tabs_s, _row_tile(db * t))
        mix_s, bias_s = _gmlp_mix(a_w_s[l], a_b_s[l], t)
        a_out, v_new = _gmlp(a, ln_g, ln_b, mix_s, bias_s, True)
        flat = lambda c: c.reshape(c.shape[0], c.shape[1], c.shape[2] * KV_HEADS, HEAD_DIM)
        kc, vc = _compress(flat(cache_k_cmp), flat(cache_v_cmp), l, page_table, cmp_k, cmp_v)
        r3 = lambda r: r.reshape(db, t, r.shape[-1])
        new = lambda r: r.reshape(db, t * KV_HEADS, HEAD_DIM)
        o, kwin, vwin = _sample_attention(
            r3(a), kc, vc, flat(cache_k_slc), flat(cache_v_slc), l, page_table,
            flat(state_k_win), flat(state_v_win), new(ks_r), new(vs_r), new(kw_r), new(vw_r), r3(gt))
        xs = tail(xs, a_out, o.reshape(db * t, Q_COLS), gt)
        r4 = lambda r: r.reshape(db, -1, KV_HEADS, HEAD_DIM)
        for lst, r in zip(outs_s, (kc_r, vc_r, ks_r, vs_r, kwin, vwin)):
            lst.append(r4(r))
        outs_s[6].append(v_new.reshape(db, t, A_WIDTH))

    y_prompt = xp.reshape(b, s, d)
    y_sample = xs.reshape(db, t, d)
    return (y_prompt, y_sample, *[jnp.stack(o) for o in outs_p], *[jnp.stack(o) for o in outs_s])
```

```python
import functools
import math

import jax
import jax.numpy as jnp
from jax import lax
from jax.experimental import pallas as pl
from jax.experimental.pallas import tpu as pltpu

F32 = jnp.float32
MM = jnp.bfloat16

D_MODEL = 2048
A_WIDTH = 2048
A_GROUPS = 8
A_GROUP_DIM = A_WIDTH // A_GROUPS
CHUNK = 128
N_HEADS = 16
KV_HEADS = 4
HEAD_DIM = 128
Q_PER_KV = N_HEADS // KV_HEADS
ROPE_DIM = HEAD_DIM // 4
ROPE_THETA = 500000.0
CMP_BLOCK = 32
SLC_BLOCK = 64
_SLC_SHIFT = 6
N_SELECT = 8
WINDOW = 512
Q_BLOCK = 128
PAGE_SIZE = 128
SCALE = HEAD_DIM ** -0.5
LOG2E = math.log2(math.e)
N_GROUPS = 4
EXPERTS_PER_GROUP = 8
N_EXPERTS = N_GROUPS * EXPERTS_PER_GROUP
EXPERT_DIM = 256
EPS = 1e-6
NEG = -1e30
FORCE_BONUS = 1e4
Q_COLS = N_HEADS * HEAD_DIM
KV_COLS = KV_HEADS * HEAD_DIM

LANES = 128
TN = 512
SEL_CHUNK = 512
CMP_PITCH = CMP_BLOCK * KV_HEADS + 8
EXPERT_TILE = 256
SLOT_ROWS = 8
ROUTED_MIN_TOKENS = N_EXPERTS * EXPERT_TILE // 8
VMEM_LIMIT = 56 * 1024 * 1024

_A_TILES = 3 * D_MODEL // TN
_G_TILES = 2 * D_MODEL // TN + 1
_KV_TILES = 6


def _cparams(sem):
    return pltpu.CompilerParams(dimension_semantics=sem, vmem_limit_bytes=VMEM_LIMIT)


def _rope_tile(z, cos, s1, s2):
    outs = []
    for h in range(z.shape[1] // HEAD_DIM):
        zh = z[:, h * HEAD_DIM:(h + 1) * HEAD_DIM]
        outs.append(zh * cos + pltpu.roll(zh, ROPE_DIM // 2, 1) * s1
                    + pltpu.roll(zh, HEAD_DIM - ROPE_DIM // 2, 1) * s2)
    return jnp.concatenate(outs, axis=1)


def _proj_a_kernel(x_ref, g_ref, w_ref, cos_ref, s1_ref, s2_ref, a_ref, gt_ref, h_ref):
    j = pl.program_id(1)

    @pl.when(j == 0)
    def _():
        x = x_ref[...]
        ms = jnp.mean(x * x, axis=-1, keepdims=True)
        h_ref[...] = (x * lax.rsqrt(ms + EPS) * g_ref[...]).astype(MM)

    def z():
        return jnp.dot(h_ref[...], w_ref[...], preferred_element_type=F32)

    @pl.when(j < 2 * D_MODEL // TN)
    def _():
        a_ref[...] = z()

    @pl.when((j >= 2 * D_MODEL // TN) & (j < _A_TILES))
    def _():
        a_ref[...] = _rope_tile(z(), cos_ref[...], s1_ref[...], s2_ref[...])

    @pl.when(j >= _A_TILES)
    def _():
        gt_ref[...] = z()


def _project_a(x2d, g, w, tabs, tm):
    n, d = x2d.shape
    cos, s1, s2 = tabs
    tab_blocks = cos.shape[0] // tm
    row = lambda i, j: (i, 0)
    tab = lambda i, j: (i % tab_blocks, 0)
    return pl.pallas_call(
        _proj_a_kernel,
        grid=(n // tm, _A_TILES + _G_TILES),
        in_specs=[pl.BlockSpec((tm, d), row),
                  pl.BlockSpec((1, d), lambda i, j: (0, 0)),
                  pl.BlockSpec((d, TN), lambda i, j: (0, j)),
                  pl.BlockSpec((tm, LANES), tab), pl.BlockSpec((tm, LANES), tab),
                  pl.BlockSpec((tm, LANES), tab)],
        out_specs=[pl.BlockSpec((tm, TN), lambda i, j: (i, jnp.minimum(j, _A_TILES - 1))),
                   pl.BlockSpec((tm, TN), lambda i, j: (i, jnp.maximum(j - _A_TILES, 0))),
                   pl.BlockSpec((tm, d), row)],
        out_shape=[jax.ShapeDtypeStruct((n, _A_TILES * TN), F32),
                   jax.ShapeDtypeStruct((n, _G_TILES * TN), F32),
                   jax.ShapeDtypeStruct((n, d), MM)],
        compiler_params=_cparams(("parallel", "arbitrary")),
        name="norm_project",
    )(x2d, g, w, cos, s1, s2)


def _proj_kv_kernel(h_ref, w_ref, cos_ref, s1_ref, s2_ref,
                    kc_ref, vc_ref, ks_ref, vs_ref, kw_ref, vw_ref, ksb_ref, vsb_ref, kwb_ref, vwb_ref):
    j = pl.program_id(1)
    outs = ((kc_ref, None, True), (vc_ref, None, False), (ks_ref, ksb_ref, True),
            (vs_ref, vsb_ref, False), (kw_ref, kwb_ref, True), (vw_ref, vwb_ref, False))
    for t, (ref, packed_ref, roped) in enumerate(outs):
        @pl.when(j == t)
        def _(ref=ref, packed_ref=packed_ref, roped=roped):
            z = jnp.dot(h_ref[...], w_ref[...], preferred_element_type=F32)
            v = _rope_tile(z, cos_ref[...], s1_ref[...], s2_ref[...]) if roped else z
            for h in range(KV_HEADS):
                ref[pl.ds(h, z.shape[0], stride=KV_HEADS), :] = v[:, h * HEAD_DIM:(h + 1) * HEAD_DIM]
            if packed_ref is not None:
                packed_ref[...] = v.astype(MM)


def _project_kv(h, w, tabs, tm):
    n, d = h.shape
    cos, s1, s2 = tabs
    tab_blocks = cos.shape[0] // tm
    tab = lambda i, j: (i % tab_blocks, 0)
    return pl.pallas_call(
        _proj_kv_kernel,
        grid=(n // tm, _KV_TILES),
        in_specs=[pl.BlockSpec((tm, d), lambda i, j: (i, 0)),
                  pl.BlockSpec((d, TN), lambda i, j: (0, j)),
                  pl.BlockSpec((tm, LANES), tab), pl.BlockSpec((tm, LANES), tab),
                  pl.BlockSpec((tm, LANES), tab)],
        out_specs=[pl.BlockSpec((tm * KV_HEADS, HEAD_DIM), lambda i, j: (i, 0))] * 6
                  + [pl.BlockSpec((tm, KV_COLS), lambda i, j: (i, 0))] * 4,
        out_shape=[jax.ShapeDtypeStruct((n * KV_HEADS, HEAD_DIM), F32)] * 6
                  + [jax.ShapeDtypeStruct((n, KV_COLS), MM)] * 4,
        compiler_params=_cparams(("parallel", "arbitrary")),
        name="kv_project",
    )(h, w, cos, s1, s2)


def _rope_tables(pos):
    half = ROPE_DIM // 2
    inv = jnp.power(jnp.float32(ROPE_THETA), -jnp.arange(0, ROPE_DIM, 2, dtype=F32) / ROPE_DIM)
    ang = pos.astype(F32)[:, None] * inv[None, :]
    cos, sin = jnp.cos(ang), jnp.sin(ang)
    r = pos.shape[0]
    one = jnp.ones((r, HEAD_DIM - ROPE_DIM), F32)
    zero = jnp.zeros((r, HEAD_DIM - ROPE_DIM), F32)
    zh = jnp.zeros((r, half), F32)
    return (jnp.concatenate([cos, cos, one], axis=1),
            jnp.concatenate([zh, sin, zero], axis=1),
            jnp.concatenate([-sin, zh, zero], axis=1))


def _layout_w_in(w_in):
    c_a = 2 * A_WIDTH + Q_COLS
    c_kv = c_a + 6 * KV_COLS
    g_br, g_ab = w_in[:, c_kv:c_kv + 3 * N_HEADS], w_in[:, c_kv + 3 * N_HEADS:]
    per_head = 3 * Q_PER_KV
    g_br = g_br.reshape(-1, KV_HEADS, per_head)
    g_br = jnp.pad(g_br, ((0, 0), (0, 0), (0, LANES - per_head))).reshape(-1, KV_HEADS * LANES)
    w_a = jnp.concatenate([w_in[:, :c_a], g_ab, g_br], axis=1).astype(MM)
    return w_a, w_in[:, c_a:c_kv].astype(MM)


def _gmlp_kernel(u_ref, v_ref, lng_ref, lnb_ref, mix_ref, bias_ref, o_ref, *maybe_vout):
    u = jax.nn.gelu(u_ref[...])
    v = jax.nn.gelu(v_ref[...])
    mu = jnp.mean(v, axis=-1, keepdims=True)
    vc = v - mu
    var = jnp.mean(vc * vc, axis=-1, keepdims=True)
    v = vc * lax.rsqrt(var + EPS) * lng_ref[...] + lnb_ref[...]
    if maybe_vout:
        maybe_vout[0][...] = v
    for g in range(A_GROUPS):
        sl = slice(g * A_GROUP_DIM, (g + 1) * A_GROUP_DIM)
        s = jnp.dot(mix_ref[g], v[:, sl].astype(MM), preferred_element_type=F32)
        b = bias_ref[g]
        s = s + jnp.concatenate([b] * (A_GROUP_DIM // LANES), axis=1)
        o_ref[:, sl] = (u[:, sl] * s).astype(o_ref.dtype)


def _gmlp(a, ln_g, ln_b, mix, bias, emit_v):
    n = a.shape[0]
    out_shape = [jax.ShapeDtypeStruct((n, A_WIDTH), MM)]
    out_specs = [pl.BlockSpec((CHUNK, A_WIDTH), lambda i: (i, 0))]
    if emit_v:
        out_shape.append(jax.ShapeDtypeStruct((n, A_WIDTH), F32))
        out_specs.append(pl.BlockSpec((CHUNK, A_WIDTH), lambda i: (i, 0)))
    const3 = lambda i: (0, 0, 0)
    return pl.pallas_call(
        _gmlp_kernel,
        grid=(n // CHUNK,),
        in_specs=[pl.BlockSpec((CHUNK, A_WIDTH), lambda i: (i, 0)),
                  pl.BlockSpec((CHUNK, A_WIDTH), lambda i: (i, 1)),
                  pl.BlockSpec((1, A_WIDTH), lambda i: (0, 0)),
                  pl.BlockSpec((1, A_WIDTH), lambda i: (0, 0)),
                  pl.BlockSpec((A_GROUPS, CHUNK, CHUNK), const3),
                  pl.BlockSpec((A_GROUPS, CHUNK, LANES), const3)],
        out_specs=out_specs,
        out_shape=out_shape,
        compiler_params=_cparams(("parallel",)),
        name="gmlp",
    )(a, a, ln_g, ln_b, mix, bias)


def _gmlp_mix(w_s, b_s, t):
    ln = min(t, CHUNK)
    causal = jnp.tril(jnp.ones((ln, ln), dtype=bool))
    w = jnp.where(causal, w_s[:, :ln, :ln], 0.0)
    reps = CHUNK // ln
    eye = jnp.eye(reps, dtype=w.dtype)
    mix = jnp.einsum('ab,gij->gaibj', eye, w).reshape(A_GROUPS, CHUNK, CHUNK)
    bias = jnp.tile(b_s[:, :ln], (1, reps))
    return mix.astype(MM), jnp.broadcast_to(bias[:, :, None], (A_GROUPS, CHUNK, LANES)).astype(F32)


def _compress_kernel(n_pages, pt_ref, *refs):
    k_pages, v_pages = refs[:n_pages], refs[n_pages:2 * n_pages]
    pek_ref, w1k_ref, w2k_ref, pev_ref, w1v_ref, w2v_ref, kc_ref, vc_ref, scr = refs[2 * n_pages:]
    blocks = n_pages * PAGE_SIZE // CMP_BLOCK
    half = blocks // 2

    page_rows = PAGE_SIZE * KV_HEADS
    blk_rows = CMP_BLOCK * KV_HEADS

    region = half * CMP_PITCH

    def one(pages, pe_ref, w1_ref, w2_ref, out_ref):
        for p in range(n_pages):
            page = pages[p][0, 0] + pe_ref[...]
            for bl in range(PAGE_SIZE // CMP_BLOCK):
                m = p * (PAGE_SIZE // CMP_BLOCK) + bl
                off = (m % 2) * region + (m // 2) * CMP_PITCH
                scr[off:off + blk_rows, :] = page[bl * blk_rows:(bl + 1) * blk_rows, :]
        acc = jnp.zeros((KV_HEADS * blocks, HEAD_DIM), F32)
        for jp in range(CMP_BLOCK // 2):
            parts = []
            for j in (2 * jp, 2 * jp + 1):
                rows = [scr[pl.ds(parity * region + j * KV_HEADS + h, half, stride=CMP_PITCH), :]
                        for h in range(KV_HEADS) for parity in (0, 1)]
                parts.append(jnp.concatenate(rows, axis=0).astype(MM))
            lhs = jnp.concatenate(parts, axis=1)
            acc = acc + jnp.dot(lhs, w1_ref[jp], preferred_element_type=F32)
        hid = jax.nn.gelu(acc)
        out = jnp.dot(hid.astype(MM), w2_ref[...], preferred_element_type=F32)
        out_ref[0] = out.reshape(KV_HEADS, blocks, HEAD_DIM)

    one(k_pages, pek_ref, w1k_ref, w2k_ref, kc_ref)
    one(v_pages, pev_ref, w1v_ref, w2v_ref, vc_ref)


def _compress(pool_k, pool_v, layer, page_table, cmp_k, cmp_v):
    n_seq, n_pages = page_table.shape
    blocks = n_pages * PAGE_SIZE // CMP_BLOCK
    page_rows = PAGE_SIZE * KV_HEADS

    def page_spec(p):
        return pl.BlockSpec((1, 1, page_rows, HEAD_DIM), lambda s, pt, p=p: (layer, pt[s, p], 0, 0))

    def prep(c):
        pe, w1, w2 = c
        pe_t = jnp.tile(jnp.repeat(pe, KV_HEADS, axis=0), (PAGE_SIZE // CMP_BLOCK, 1)).astype(F32)
        return pe_t, w1.reshape(CMP_BLOCK // 2, 2 * HEAD_DIM, HEAD_DIM).astype(MM), w2.astype(MM)

    const2 = lambda s, pt: (0, 0)
    const3 = lambda s, pt: (0, 0, 0)
    w_specs = [pl.BlockSpec((page_rows, HEAD_DIM), const2),
               pl.BlockSpec((CMP_BLOCK // 2, 2 * HEAD_DIM, HEAD_DIM), const3),
               pl.BlockSpec((HEAD_DIM, HEAD_DIM), const2)]
    out_spec = pl.BlockSpec((1, KV_HEADS, blocks, HEAD_DIM), lambda s, pt: (s, 0, 0, 0))
    out_shape = jax.ShapeDtypeStruct((n_seq, KV_HEADS, blocks, HEAD_DIM), F32)
    return pl.pallas_call(
        functools.partial(_compress_kernel, n_pages),
        grid_spec=pltpu.PrefetchScalarGridSpec(
            num_scalar_prefetch=1,
            grid=(n_seq,),
            in_specs=[page_spec(p) for p in range(n_pages)] * 2 + w_specs * 2,
            out_specs=[out_spec, out_spec],
            scratch_shapes=[pltpu.VMEM((blocks * CMP_PITCH, HEAD_DIM), F32)]),
        out_shape=[out_shape, out_shape],
        compiler_params=_cparams(("arbitrary",)),
        name="compress",
    )(page_table, *([pool_k] * n_pages), *([pool_v] * n_pages), *prep(cmp_k), *prep(cmp_v))


def _select_rows(imp, qpos, n_blocks):
    j = lax.broadcasted_iota(jnp.int32, (n_blocks, 1), 0)
    forced = (j == 0) | (j == (qpos >> _SLC_SHIFT))
    future = j * SLC_BLOCK > qpos
    score = jnp.where(future, -1.0, imp + jnp.where(forced, FORCE_BONUS, 0.0))
    rank = jnp.zeros(score.shape, jnp.int32)
    for i in range(n_blocks):
        ri = score[i:i + 1, :]
        beats = (ri > score) | ((ri == score) & (i < j))
        rank = rank + beats.astype(jnp.int32)
    return rank < min(N_SELECT, n_blocks)


def _prompt_attn_kernel(q_ref, kc_ref, vc_ref, ks_ref, vs_ref, kw_ref, vw_ref, gbr_ref, o_ref,
                        vst_scr, vwt_scr, bias_scr, acc_scr, ow_scr):
    n = pl.program_id(2)
    seq = ks_ref.shape[0]
    n_tiles = seq // Q_BLOCK
    n_sel_blocks = seq // SLC_BLOCK
    n_cmp = kc_ref.shape[2]
    lane_groups = [slice(g * Q_BLOCK, (g + 1) * Q_BLOCK) for g in range(Q_PER_KV)]

    tiles_per_chunk = SEL_CHUNK // Q_BLOCK

    @pl.when(n == 0)
    def _():
        for kt in range(n_tiles):
            rows = slice(kt * Q_BLOCK, (kt + 1) * Q_BLOCK)
            c, j = divmod(kt, tiles_per_chunk)
            vst_scr[c, :, j * Q_BLOCK:(j + 1) * Q_BLOCK] = vs_ref[rows, :].astype(F32).T.astype(MM)
            vwt_scr[kt] = vw_ref[rows, :].astype(F32).T.astype(MM)

    q = q_ref[...] * (SCALE * LOG2E)
    q_t = jnp.concatenate([q[:, g * HEAD_DIM:(g + 1) * HEAD_DIM].T for g in range(Q_PER_KV)],
                          axis=1).astype(MM)
    qpos = n * Q_BLOCK + lax.broadcasted_iota(jnp.int32, (1, Q_BLOCK), 1)

    win_keys = WINDOW + Q_BLOCK
    w_tile0 = jnp.maximum(n - WINDOW // Q_BLOCK, 0)
    w_start = pl.multiple_of(w_tile0 * Q_BLOCK, Q_BLOCK)
    s_w = jnp.dot(kw_ref[pl.ds(w_start, win_keys), :], q_t, preferred_element_type=F32)
    dist = qpos - (w_start + lax.broadcasted_iota(jnp.int32, (win_keys, 1), 0))
    w_bias = jnp.where((dist >= 0) & (dist < WINDOW), 0.0, NEG)
    ls_w, ps = [], []
    for sl in lane_groups:
        s = s_w[:, sl] + w_bias
        p = jnp.exp2(s - jnp.max(s, axis=0, keepdims=True))
        ls_w.append(jnp.sum(p, axis=0, keepdims=True))
        ps.append(p.astype(MM))
    p_w = jnp.concatenate(ps, axis=1)
    ow_t = functools.reduce(
        lambda a, b: a + b,
        [jnp.dot(vwt_scr[w_tile0 + j], p_w[j * Q_BLOCK:(j + 1) * Q_BLOCK, :], preferred_element_type=F32)
         for j in range(win_keys // Q_BLOCK)])
    ow_scr[...] = ow_t

    s_c = jnp.dot(kc_ref[0, 0].astype(MM), q_t, preferred_element_type=F32)
    r = lax.broadcasted_iota(jnp.int32, (n_cmp, 1), 0)
    blk = jnp.where(r < n_cmp // 2, 2 * r, 2 * (r - n_cmp // 2) + 1)
    ok = ((blk + 1) * CMP_BLOCK - 1) <= qpos
    imp = jnp.zeros((n_cmp, Q_BLOCK), F32)
    probs = []
    for sl in lane_groups:
        s = jnp.where(ok, s_c[:, sl], NEG)
        e = jnp.exp2(s - jnp.max(s, axis=0, keepdims=True))
        p = jnp.where(ok, e / jnp.sum(e, axis=0, keepdims=True), 0.0)
        imp = imp + p
        probs.append(p.astype(MM))
    oc_t = jnp.dot(vc_ref[0, 0].T.astype(MM), jnp.concatenate(probs, axis=1), preferred_element_type=F32)

    sel = _select_rows(imp[:n_cmp // 2] + imp[n_cmp // 2:], qpos, n_sel_blocks)
    sel_bias = jnp.where(sel, 0.0, NEG)
    blocks_per_chunk = SEL_CHUNK // SLC_BLOCK
    key_row = lax.broadcasted_iota(jnp.int32, (SEL_CHUNK, 1), 0)
    n_chunks = lax.div(n, tiles_per_chunk) + 1
    for c in range(seq // SEL_CHUNK):
        @pl.when(c < n_chunks)
        def _(c=c):
            tile = jnp.concatenate(
                [jnp.broadcast_to(sel_bias[c * blocks_per_chunk + i:c * blocks_per_chunk + i + 1, :],
                                  (SLC_BLOCK, Q_BLOCK)) for i in range(blocks_per_chunk)], axis=0)
            bias_scr[c] = jnp.where(c * SEL_CHUNK + key_row <= qpos, tile, NEG)

    acc_scr[...] = jnp.zeros_like(acc_scr)

    def body(c, carry):
        ms, ls = carry
        start = pl.multiple_of(c * SEL_CHUNK, SEL_CHUNK)
        s_t = jnp.dot(ks_ref[pl.ds(start, SEL_CHUNK), :], q_t, preferred_element_type=F32)
        bias = bias_scr[c]
        new_ms, new_ls, ps, alphas = [], [], [], []
        for g, sl in enumerate(lane_groups):
            s = s_t[:, sl] + bias
            m_new = jnp.maximum(ms[g], jnp.max(s, axis=0, keepdims=True))
            alpha = jnp.exp2(ms[g] - m_new)
            p = jnp.exp2(s - m_new)
            new_ls.append(alpha * ls[g] + jnp.sum(p, axis=0, keepdims=True))
            new_ms.append(m_new)
            alphas.append(alpha)
            ps.append(p.astype(MM))
        pv = jnp.dot(vst_scr[c], jnp.concatenate(ps, axis=1), preferred_element_type=F32)
        for g, sl in enumerate(lane_groups):
            acc_scr[:, sl] = alphas[g] * acc_scr[:, sl] + pv[:, sl]
        return tuple(new_ms), tuple(new_ls)

    init = (tuple(jnp.full((1, Q_BLOCK), NEG, F32) for _ in lane_groups),
            tuple(jnp.zeros((1, Q_BLOCK), F32) for _ in lane_groups))
    _, ls_s = lax.fori_loop(0, n_chunks, body, init)

    gate_t = jax.nn.sigmoid(gbr_ref[...]).T
    for g, sl in enumerate(lane_groups):
        o_t = (oc_t[:, sl] * gate_t[3 * g:3 * g + 1, :]
               + (acc_scr[:, sl] / ls_s[g]) * gate_t[3 * g + 1:3 * g + 2, :]
               + (ow_scr[:, sl] / ls_w[g]) * gate_t[3 * g + 2:3 * g + 3, :])
        o_ref[:, g * HEAD_DIM:(g + 1) * HEAD_DIM] = o_t.T.astype(o_ref.dtype)


def _prompt_attention(a, kc, vc, ks, vs, kw, vw, gt, batch, seq):
    n_qb = seq // Q_BLOCK
    q_col0 = 2 * A_WIDTH // (Q_PER_KV * HEAD_DIM)
    gbr_col0 = 2 * D_MODEL // LANES
    cmp_spec = pl.BlockSpec((1, 1, kc.shape[2], HEAD_DIM), lambda b, h, n: (b, h, 0, 0))
    kv_spec = pl.BlockSpec((seq, HEAD_DIM), lambda b, h, n: (b, h))
    assert seq % SEL_CHUNK == 0 and seq >= WINDOW + Q_BLOCK
    scratch = [pltpu.VMEM((seq // SEL_CHUNK, HEAD_DIM, SEL_CHUNK), MM),
               pltpu.VMEM((n_qb, HEAD_DIM, Q_BLOCK), MM),
               pltpu.VMEM((seq // SEL_CHUNK, SEL_CHUNK, Q_BLOCK), F32),
               pltpu.VMEM((HEAD_DIM, Q_PER_KV * Q_BLOCK), F32),
               pltpu.VMEM((HEAD_DIM, Q_PER_KV * Q_BLOCK), F32)]
    return pl.pallas_call(
        _prompt_attn_kernel,
        grid=(batch, KV_HEADS, n_qb),
        in_specs=[pl.BlockSpec((Q_BLOCK, Q_PER_KV * HEAD_DIM), lambda b, h, n: (b * n_qb + n, q_col0 + h)),
                  cmp_spec, cmp_spec, kv_spec, kv_spec, kv_spec, kv_spec,
                  pl.BlockSpec((Q_BLOCK, LANES), lambda b, h, n: (b * n_qb + n, gbr_col0 + h))],
        out_specs=pl.BlockSpec((Q_BLOCK, Q_PER_KV * HEAD_DIM), lambda b, h, n: (b * n_qb + n, h)),
        out_shape=jax.ShapeDtypeStruct((batch * seq, Q_COLS), MM),
        scratch_shapes=scratch,
        compiler_params=_cparams(("parallel", "parallel", "arbitrary")),
        name="prompt_attention",
    )(a, kc, vc, ks, vs, kw, vw, gt)


def _dot_nt(a, b):
    return lax.dot_general(a, b, (((1,), (1,)), ((), ())), preferred_element_type=F32)


def _select_cols(imp, qpos, n_blocks):
    j = lax.broadcasted_iota(jnp.int32, (1, n_blocks), 1)
    forced = (j == 0) | (j == (qpos >> _SLC_SHIFT))
    future = j * SLC_BLOCK > qpos
    score = jnp.where(future, -1.0, imp + jnp.where(forced, FORCE_BONUS, 0.0))
    rank = jnp.zeros(score.shape, jnp.int32)
    for i in range(n_blocks):
        ci = score[:, i:i + 1]
        beats = (ci > score) | ((ci == score) & (i < j))
        rank = rank + beats.astype(jnp.int32)
    return (rank < min(N_SELECT, n_blocks)).astype(F32)


def _softmax_parts(parts):
    m = functools.reduce(jnp.maximum, [jnp.max(s, axis=1, keepdims=True) for s, _ in parts])
    es = [jnp.exp2(s - m) for s, _ in parts]
    den = functools.reduce(lambda a, b: a + b, [jnp.sum(e, axis=1, keepdims=True) for e in es])
    acc = None
    for e, (_, v) in zip(es, parts):
        o = jnp.dot((e / den).astype(MM), v, preferred_element_type=F32)
        acc = o if acc is None else acc + o
    return acc


def _sample_attn_kernel(n_pages, pt_ref, *refs):
    ks_pages, vs_pages = refs[:n_pages], refs[n_pages:2 * n_pages]
    (q_ref, kc_ref, vc_ref, kbuf_ref, vbuf_ref, ksn_ref, vsn_ref, kwn_ref, vwn_ref, gbr_ref,
     o_ref, kwo_ref, vwo_ref) = refs[2 * n_pages:]
    t = q_ref.shape[1]
    past = n_pages * PAGE_SIZE
    nb_past = past // SLC_BLOCK
    buf_rows = kbuf_ref.shape[2]
    wb = buf_rows // KV_HEADS
    new_rows = t * KV_HEADS
    hq = KV_HEADS * t
    n_cmp = kc_ref.shape[2]
    head_shift, t_shift, cmp_shift = (v.bit_length() - 1 for v in (KV_HEADS, t, n_cmp))
    assert (KV_HEADS, t, n_cmp) == (1 << head_shift, 1 << t_shift, 1 << cmp_shift) and LANES % KV_HEADS == 0

    q = q_ref[0] * (SCALE * LOG2E)
    qb = jnp.concatenate([q[:, (h * Q_PER_KV + g) * HEAD_DIM:(h * Q_PER_KV + g + 1) * HEAD_DIM]
                          for g in range(Q_PER_KV) for h in range(KV_HEADS)], axis=0).astype(MM)
    ri = lax.broadcasted_iota(jnp.int32, (Q_PER_KV * hq, 1), 0)
    row_h = (ri >> t_shift) & (KV_HEADS - 1)
    row_t = ri & (t - 1)
    qpos = past + row_t

    def key_cols(n):
        c = lax.broadcasted_iota(jnp.int32, (1, n), 1)
        return c & (KV_HEADS - 1), c >> head_shift

    kwo_ref[0, 0, :buf_rows - new_rows] = kbuf_ref[0, 0, new_rows:]
    kwo_ref[0, 0, buf_rows - new_rows:] = kwn_ref[0]
    vwo_ref[0, 0, :buf_rows - new_rows] = vbuf_ref[0, 0, new_rows:]
    vwo_ref[0, 0, buf_rows - new_rows:] = vwn_ref[0]

    kc = kc_ref[0].reshape(KV_HEADS * n_cmp, HEAD_DIM).astype(MM)
    vc = vc_ref[0].reshape(KV_HEADS * n_cmp, HEAD_DIM).astype(MM)
    col = lax.broadcasted_iota(jnp.int32, (1, KV_HEADS * n_cmp), 1)
    slot = col & (n_cmp - 1)
    blk = jnp.where(slot < n_cmp // 2, 2 * slot, 2 * (slot - n_cmp // 2) + 1)
    ok = ((col >> cmp_shift) == row_h) & (((blk + 1) * CMP_BLOCK - 1) <= qpos)
    s = jnp.where(ok, _dot_nt(qb, kc), NEG)
    e = jnp.exp2(s - jnp.max(s, axis=1, keepdims=True))
    p = jnp.where(ok, e / jnp.sum(e, axis=1, keepdims=True), 0.0)
    o_c = jnp.dot(p.astype(MM), vc, preferred_element_type=F32)
    p = functools.reduce(lambda a, b: a + b, [p[:, h * n_cmp:(h + 1) * n_cmp] for h in range(KV_HEADS)])
    p = functools.reduce(lambda a, b: a + b, [p[g * hq:(g + 1) * hq] for g in range(Q_PER_KV)])
    imp = jnp.concatenate([p[:, :n_cmp // 2] + p[:, n_cmp // 2:], jnp.zeros((hq, 1), F32)], axis=1)
    sel = _select_cols(imp, qpos[:hq], nb_past + 1)
    sel = jnp.concatenate([sel] * Q_PER_KV, axis=0)

    page_h, page_r = key_cols(PAGE_SIZE * KV_HEADS)
    own_head = page_h == row_h
    page_blk = page_r >> _SLC_SHIFT
    per_page = PAGE_SIZE // SLC_BLOCK
    parts = []
    for pg in range(n_pages):
        ok = own_head & functools.reduce(
            lambda a, b: a | b,
            [(page_blk == c) & (sel[:, pg * per_page + c:pg * per_page + c + 1] > 0.5) for c in range(per_page)])
        parts.append((jnp.where(ok, _dot_nt(qb, ks_pages[pg][0, 0].astype(MM)), NEG),
                      vs_pages[pg][0, 0].astype(MM)))
    new_h, new_t = key_cols(new_rows)
    new_ok = (new_h == row_h) & (new_t <= row_t)
    parts.append((jnp.where(new_ok & (sel[:, nb_past:nb_past + 1] > 0.5),
                            _dot_nt(qb, ksn_ref[0].astype(MM)), NEG), vsn_ref[0].astype(MM)))
    o_s = _softmax_parts(parts)

    buf_h, buf_r = key_cols(buf_rows)
    buf_pos = past - wb + buf_r
    dist = qpos - buf_pos
    buf_ok = (buf_h == row_h) & (dist >= 0) & (dist < WINDOW) & (buf_pos >= 0)
    o_w = _softmax_parts([
        (jnp.where(buf_ok, _dot_nt(qb, kbuf_ref[0, 0].astype(MM)), NEG), vbuf_ref[0, 0].astype(MM)),
        (jnp.where(new_ok, _dot_nt(qb, kwn_ref[0].astype(MM)), NEG), vwn_ref[0].astype(MM))])

    gate = jax.nn.sigmoid(gbr_ref[0])
    for g in range(Q_PER_KV):
        for h in range(KV_HEADS):
            rs = slice(g * hq + h * t, g * hq + (h + 1) * t)
            c0 = h * LANES + 3 * g
            o = (o_c[rs] * gate[:, c0:c0 + 1] + o_s[rs] * gate[:, c0 + 1:c0 + 2]
                 + o_w[rs] * gate[:, c0 + 2:c0 + 3])
            col0 = (h * Q_PER_KV + g) * HEAD_DIM
            o_ref[0, :, col0:col0 + HEAD_DIM] = o.astype(o_ref.dtype)


def _sample_attention(a3, kc, vc, pool_ks, pool_vs, layer, page_table, kbuf, vbuf, ksn, vsn, kwn, vwn, gt3):
    db, t, _ = a3.shape
    n_pages = page_table.shape[1]
    buf_rows = kbuf.shape[2]
    page_rows = PAGE_SIZE * KV_HEADS

    def page_spec(p):
        return pl.BlockSpec((1, 1, page_rows, HEAD_DIM), lambda s, pt, p=p: (layer, pt[s, p], 0, 0))

    row3 = lambda s, pt: (s, 0, 0)
    cmp_spec = pl.BlockSpec((1, KV_HEADS, kc.shape[2], HEAD_DIM), lambda s, pt: (s, 0, 0, 0))
    buf_spec = pl.BlockSpec((1, 1, buf_rows, HEAD_DIM), lambda s, pt: (layer, s, 0, 0))
    out_buf_spec = pl.BlockSpec((1, 1, buf_rows, HEAD_DIM), lambda s, pt: (0, s, 0, 0))
    new_spec = pl.BlockSpec((1, t * KV_HEADS, HEAD_DIM), row3)
    buf_shape = jax.ShapeDtypeStruct((1, db, buf_rows, HEAD_DIM), F32)
    return pl.pallas_call(
        functools.partial(_sample_attn_kernel, n_pages),
        grid_spec=pltpu.PrefetchScalarGridSpec(
            num_scalar_prefetch=1,
            grid=(db,),
            in_specs=[page_spec(p) for p in range(n_pages)] * 2
                     + [pl.BlockSpec((1, t, Q_COLS), lambda s, pt: (s, 0, 2 * A_WIDTH // Q_COLS)),
                        cmp_spec, cmp_spec, buf_spec, buf_spec, new_spec, new_spec, new_spec, new_spec,
                        pl.BlockSpec((1, t, KV_HEADS * LANES),
                                     lambda s, pt: (s, 0, 2 * D_MODEL // (KV_HEADS * LANES)))],
            out_specs=[pl.BlockSpec((1, t, Q_COLS), row3), out_buf_spec, out_buf_spec]),
        out_shape=[jax.ShapeDtypeStruct((db, t, Q_COLS), MM), buf_shape, buf_shape],
        compiler_params=_cparams(("arbitrary",)),
        name="sample_attention",
    )(page_table, *([pool_ks] * n_pages), *([pool_vs] * n_pages),
      a3, kc, vc, kbuf, vbuf, ksn, vsn, kwn, vwn, gt3)


def _mix_kernel(a_ref, o_ref, wpa_ref, wpb_ref, ga_ref, gb_ref, m_ref):
    y_a = jnp.dot(a_ref[...], wpa_ref[...], preferred_element_type=F32)
    y_b = jnp.dot(o_ref[...], wpb_ref[...], preferred_element_type=F32)
    m = jax.nn.sigmoid(ga_ref[...]) * y_a + jax.nn.sigmoid(gb_ref[...]) * y_b
    m_ref[...] = m.astype(m_ref.dtype)


def _mix(a_out, o, w_pa, w_pb, gt, tm):
    n = a_out.shape[0]
    gb0 = D_MODEL // TN
    return pl.pallas_call(
        _mix_kernel,
        grid=(n // tm, D_MODEL // TN),
        in_specs=[pl.BlockSpec((tm, A_WIDTH), lambda i, j: (i, 0)),
                  pl.BlockSpec((tm, Q_COLS), lambda i, j: (i, 0)),
                  pl.BlockSpec((A_WIDTH, TN), lambda i, j: (0, j)),
                  pl.BlockSpec((Q_COLS, TN), lambda i, j: (0, j)),
                  pl.BlockSpec((tm, TN), lambda i, j: (i, j)),
                  pl.BlockSpec((tm, TN), lambda i, j: (i, gb0 + j))],
        out_specs=pl.BlockSpec((tm, TN), lambda i, j: (i, j)),
        out_shape=jax.ShapeDtypeStruct((n, D_MODEL), MM),
        compiler_params=_cparams(("parallel", "arbitrary")),
        name="merge_gate",
    )(a_out, o, w_pa, w_pb, gt, gt)


def _out_proj_kernel(m_ref, w_ref, x_ref, o_ref):
    o_ref[...] = x_ref[...] + jnp.dot(m_ref[...], w_ref[...], preferred_element_type=F32)


def _out_proj(m, w_o, x2d, tm):
    n = m.shape[0]
    return pl.pallas_call(
        _out_proj_kernel,
        grid=(n // tm, D_MODEL // TN),
        in_specs=[pl.BlockSpec((tm, D_MODEL), lambda i, j: (i, 0)),
                  pl.BlockSpec((D_MODEL, TN), lambda i, j: (0, j)),
                  pl.BlockSpec((tm, TN), lambda i, j: (i, j))],
        out_specs=pl.BlockSpec((tm, TN), lambda i, j: (i, j)),
        out_shape=jax.ShapeDtypeStruct((n, D_MODEL), F32),
        compiler_params=_cparams(("parallel", "arbitrary")),
        name="out_proj",
    )(m, w_o, x2d)


def _normed_logits(x, gf_ref, wr_hi_ref, wr_lo_ref, br_ref):
    ms = jnp.mean(x * x, axis=-1, keepdims=True)
    h = x * lax.rsqrt(ms + EPS) * gf_ref[...]
    hi = h.astype(MM)
    lo = (h - hi.astype(F32)).astype(MM)
    logits = (jnp.dot(hi, wr_hi_ref[...], preferred_element_type=F32)
              + (jnp.dot(hi, wr_lo_ref[...], preferred_element_type=F32)
                 + jnp.dot(lo, wr_hi_ref[...], preferred_element_type=F32))) + br_ref[...]
    return hi, logits


def _route(logits):
    lane = lax.broadcasted_iota(jnp.int32, logits.shape, 1)
    big = jnp.int32(LANES)
    is_g = lane < N_GROUPS
    gl = jnp.where(is_g, logits, -jnp.inf)
    gm = jnp.max(gl, axis=1, keepdims=True)
    ge = jnp.exp(gl - gm)
    pg = ge / jnp.sum(ge, axis=1, keepdims=True)
    pg_top = jnp.max(pg, axis=1, keepdims=True)
    g_idx = jnp.min(jnp.where(is_g & (pg == pg_top), lane, big), axis=1, keepdims=True)
    in_grp = (lane >= N_GROUPS) & (((lane - N_GROUPS) >> 3) == g_idx)
    el = jnp.where(in_grp, logits, -jnp.inf)
    em = jnp.max(el, axis=1, keepdims=True)
    ee = jnp.exp(el - em)
    pe = ee / jnp.sum(ee, axis=1, keepdims=True)
    p1 = jnp.max(pe, axis=1, keepdims=True)
    i1 = jnp.min(jnp.where(in_grp & (pe == p1), lane, big), axis=1, keepdims=True)
    rest = in_grp & (lane != i1)
    p2 = jnp.max(jnp.where(rest, pe, -1.0), axis=1, keepdims=True)
    i2 = jnp.min(jnp.where(rest & (pe == p2), lane, big), axis=1, keepdims=True)
    tot = p1 + p2
    return i1, i2, pg_top * p1 / tot, pg_top * p2 / tot


def _moe_kernel(final_norm, x_ref, gf_ref, wr_hi_ref, wr_lo_ref, br_ref, wg_ref, wu_ref, wd_ref, gfin_ref,
                y_ref, h_scr, gate_scr, acc_scr):
    e = pl.program_id(1)

    @pl.when(e == 0)
    def _():
        hi, logits = _normed_logits(x_ref[...], gf_ref, wr_hi_ref, wr_lo_ref, br_ref)
        i1, i2, w1, w2 = _route(logits)
        lane = lax.broadcasted_iota(jnp.int32, logits.shape, 1)
        h_scr[...] = hi
        gate_scr[...] = jnp.where(lane == i1, w1, 0.0) + jnp.where(lane == i2, w2, 0.0)
        acc_scr[...] = jnp.zeros_like(acc_scr)

    h = h_scr[...]
    lane = lax.broadcasted_iota(jnp.int32, gate_scr.shape, 1)
    gcol = jnp.sum(jnp.where(lane == e + N_GROUPS, gate_scr[...], 0.0), axis=1, keepdims=True)
    hid = (jax.nn.silu(jnp.dot(h, wg_ref[0], preferred_element_type=F32))
           * jnp.dot(h, wu_ref[0], preferred_element_type=F32))
    acc_scr[...] += jnp.dot((hid * gcol).astype(MM), wd_ref[0], preferred_element_type=F32)

    @pl.when(e == pl.num_programs(1) - 1)
    def _():
        x2 = x_ref[...] + acc_scr[...]
        if final_norm:
            ms = jnp.mean(x2 * x2, axis=-1, keepdims=True)
            x2 = x2 * lax.rsqrt(ms + EPS) * gfin_ref[...]
        y_ref[...] = x2


def _moe(x1, g_ffn, wr_hi, wr_lo, b_r, w_gate, w_up, w_down, g_final, tm, final_norm):
    n = x1.shape[0]
    row = lambda i, e: (i, 0)
    const = lambda i, e: (0, 0)
    return pl.pallas_call(
        functools.partial(_moe_kernel, final_norm),
        grid=(n // tm, N_EXPERTS),
        in_specs=[pl.BlockSpec((tm, D_MODEL), row),
                  pl.BlockSpec((1, D_MODEL), const),
                  pl.BlockSpec((D_MODEL, LANES), const),
                  pl.BlockSpec((D_MODEL, LANES), const),
                  pl.BlockSpec((1, LANES), const),
                  pl.BlockSpec((1, D_MODEL, EXPERT_DIM), lambda i, e: (e, 0, 0)),
                  pl.BlockSpec((1, D_MODEL, EXPERT_DIM), lambda i, e: (e, 0, 0)),
                  pl.BlockSpec((1, EXPERT_DIM, D_MODEL), lambda i, e: (e, 0, 0)),
                  pl.BlockSpec((1, D_MODEL), const)],
        out_specs=pl.BlockSpec((tm, D_MODEL), row),
        out_shape=jax.ShapeDtypeStruct((n, D_MODEL), F32),
        scratch_shapes=[pltpu.VMEM((tm, D_MODEL), MM), pltpu.VMEM((tm, LANES), F32),
                        pltpu.VMEM((tm, D_MODEL), F32)],
        compiler_params=_cparams(("parallel", "arbitrary")),
        name="moe_final_norm",
    )(x1, g_ffn, wr_hi, wr_lo, b_r, w_gate, w_up, w_down, g_final)


def _router_weights(w_rg, b_rg, w_re, b_re):
    w = jnp.concatenate([w_rg, w_re], axis=1)
    w = jnp.pad(w, ((0, 0), (0, LANES - w.shape[1])))
    b = jnp.pad(jnp.concatenate([b_rg, b_re]), (0, LANES - N_GROUPS - N_EXPERTS))[None, :]
    hi = w.astype(MM)
    lo = (w - hi.astype(F32)).astype(MM)
    return hi, lo, b.astype(F32)


def _router_kernel(x_ref, gf_ref, wr_hi_ref, wr_lo_ref, br_ref, tril_ref, meta_ref, cnt_ref, base_scr):
    @pl.when(pl.program_id(0) == 0)
    def _():
        base_scr[...] = jnp.zeros_like(base_scr)

    _, logits = _normed_logits(x_ref[...], gf_ref, wr_hi_ref, wr_lo_ref, br_ref)
    i1, i2, w1, w2 = _route(logits)
    lane = lax.broadcasted_iota(jnp.int32, logits.shape, 1)
    hit1, hit2 = lane == i1, lane == i2
    chosen = jnp.where(hit1 | hit2, 1.0, 0.0)
    before = jnp.dot(tril_ref[...], chosen.astype(MM), preferred_element_type=F32) + base_scr[...]
    r1 = jnp.sum(jnp.where(hit1, before, 0.0), axis=1, keepdims=True)
    r2 = jnp.sum(jnp.where(hit2, before, 0.0), axis=1, keepdims=True)
    base_scr[...] += jnp.sum(chosen, axis=0, keepdims=True)
    cols = ((i1 - N_GROUPS).astype(F32), (i2 - N_GROUPS).astype(F32), w1, w2, r1, r2)
    meta_ref[...] = functools.reduce(lambda a, b: a + b,
                                     [jnp.where(lane == k, c, 0.0) for k, c in enumerate(cols)])
    cnt_ref[...] = base_scr[...]


def _router(x1, g_ffn, wr_hi, wr_lo, b_r, tm):
    n = x1.shape[0]
    const = lambda i: (0, 0)
    tril = jnp.tril(jnp.ones((tm, tm), MM), -1)
    return pl.pallas_call(
        _router_kernel,
        grid=(n // tm,),
        in_specs=[pl.BlockSpec((tm, D_MODEL), lambda i: (i, 0)),
                  pl.BlockSpec((1, D_MODEL), const),
                  pl.BlockSpec((D_MODEL, LANES), const),
                  pl.BlockSpec((D_MODEL, LANES), const),
                  pl.BlockSpec((1, LANES), const),
                  pl.BlockSpec((tm, tm), const)],
        out_specs=[pl.BlockSpec((tm, LANES), lambda i: (i, 0)), pl.BlockSpec((1, LANES), const)],
        out_shape=[jax.ShapeDtypeStruct((n, LANES), F32), jax.ShapeDtypeStruct((1, LANES), F32)],
        scratch_shapes=[pltpu.VMEM((1, LANES), F32)],
        compiler_params=_cparams(("arbitrary",)),
        name="moe_router",
    )(x1, g_ffn, wr_hi, wr_lo, b_r, tril)


def _slots_kernel(meta_ref, offs_ref, s_ref):
    meta = meta_ref[...]
    lane = lax.broadcasted_iota(jnp.int32, meta.shape, 1)

    def slot(e_col, r_col):
        e_lane = meta[:, e_col:e_col + 1].astype(jnp.int32) + N_GROUPS
        first = jnp.sum(jnp.where(lane == e_lane, offs_ref[...], 0.0), axis=1, keepdims=True)
        return first + meta[:, r_col:r_col + 1]

    both = jnp.where(lane == 0, slot(0, 4), 0.0) + jnp.where(lane == 1, slot(1, 5), 0.0)
    s_ref[0] = both.T[:SLOT_ROWS, :].astype(jnp.int32)


def _slots(meta, offs_lanes, tb):
    n = meta.shape[0]
    return pl.pallas_call(
        _slots_kernel,
        grid=(n // tb,),
        in_specs=[pl.BlockSpec((tb, LANES), lambda i: (i, 0)), pl.BlockSpec((1, LANES), lambda i: (0, 0))],
        out_specs=pl.BlockSpec((1, SLOT_ROWS, tb), lambda i: (i, 0, 0)),
        out_shape=jax.ShapeDtypeStruct((n // tb, SLOT_ROWS, tb), jnp.int32),
        compiler_params=_cparams(("parallel",)),
        name="moe_slots",
    )(meta, offs_lanes)


def _dispatch_kernel(fill_lo_ref, fill_hi_ref, x_ref, s_ref, xs_ref, sem):
    tb = x_ref.shape[0]

    def row_copy(r, slot):
        return pltpu.make_async_copy(x_ref.at[pl.ds(r, 1)], xs_ref.at[pl.ds(slot, 1)], sem)

    def start(r, c):
        row_copy(r, s_ref[0, 0, r]).start(priority=0)
        row_copy(r, s_ref[0, 1, r]).start(priority=1)
        return c

    lax.fori_loop(0, tb, start, 0, unroll=8)

    @pl.when(pl.program_id(0) == 0)
    def _():
        def fill_start(s, c):
            row_copy(0, s).start()
            return c

        def fill_wait(s, c):
            row_copy(0, 0).wait()
            return c

        for e in range(N_EXPERTS):
            lax.fori_loop(fill_lo_ref[e], fill_hi_ref[e], fill_start, 0)
        for e in range(N_EXPERTS):
            lax.fori_loop(fill_lo_ref[e], fill_hi_ref[e], fill_wait, 0)

    whole = pltpu.make_async_copy(x_ref, xs_ref.at[pl.ds(0, tb)], sem)
    whole.wait()
    whole.wait()


def _dispatch(x1, slots, fill_lo, fill_hi, n_slots):
    n, d = x1.shape
    tb = slots.shape[2]
    return pl.pallas_call(
        _dispatch_kernel,
        grid_spec=pltpu.PrefetchScalarGridSpec(
            num_scalar_prefetch=2,
            grid=(n // tb,),
            in_specs=[pl.BlockSpec((tb, d), lambda i, lo, hi: (i, 0)),
                      pl.BlockSpec((1, SLOT_ROWS, tb), lambda i, lo, hi: (i, 0, 0), memory_space=pltpu.SMEM)],
            out_specs=pl.BlockSpec(memory_space=pl.ANY),
            scratch_shapes=[pltpu.SemaphoreType.DMA(())]),
        out_shape=jax.ShapeDtypeStruct((n_slots, d), F32),
        compiler_params=_cparams(("arbitrary",)),
        name="moe_dispatch",
    )(fill_lo, fill_hi, x1, slots)


def _expert_kernel(te_ref, tv_ref, xs_ref, gf_ref, wg_ref, wu_ref, wd_ref, ys_ref):
    used = tv_ref[pl.program_id(0)] == 1

    @pl.when(jnp.logical_not(used))
    def _():
        ys_ref[...] = jnp.zeros_like(ys_ref)

    @pl.when(used)
    def _():
        x = xs_ref[...]
        ms = jnp.mean(x * x, axis=-1, keepdims=True)
        h = (x * lax.rsqrt(ms + EPS) * gf_ref[...]).astype(MM)
        hid = (jax.nn.silu(jnp.dot(h, wg_ref[0], preferred_element_type=F32))
               * jnp.dot(h, wu_ref[0], preferred_element_type=F32))
        ys_ref[...] = jnp.dot(hid.astype(MM), wd_ref[0], preferred_element_type=F32)


def _experts(xs, tile_expert, tile_valid, g_ffn, w_gate, w_up, w_down):
    n_slots, d = xs.shape
    tile = lambda i, te, tv: (i, 0)
    weight = lambda i, te, tv: (te[i], 0, 0)
    return pl.pallas_call(
        _expert_kernel,
        grid_spec=pltpu.PrefetchScalarGridSpec(
            num_scalar_prefetch=2,
            grid=(tile_expert.shape[0],),
            in_specs=[pl.BlockSpec((EXPERT_TILE, d), tile),
                      pl.BlockSpec((1, d), lambda i, te, tv: (0, 0)),
                      pl.BlockSpec((1, d, EXPERT_DIM), weight),
                      pl.BlockSpec((1, d, EXPERT_DIM), weight),
                      pl.BlockSpec((1, EXPERT_DIM, d), weight)],
            out_specs=pl.BlockSpec((EXPERT_TILE, d), tile)),
        out_shape=jax.ShapeDtypeStruct((n_slots, d), F32),
        compiler_params=_cparams(("arbitrary",)),
        name="moe_experts",
    )(tile_expert, tile_valid, xs, g_ffn, w_gate, w_up, w_down)


def _combine_kernel(final_norm, x_ref, meta_ref, s_ref, ys_ref, gfin_ref, y_ref, buf1, buf2, sem):
    tb = x_ref.shape[0]

    def row_copy(buf, r, slot):
        return pltpu.make_async_copy(ys_ref.at[pl.ds(slot, 1)], buf.at[pl.ds(r, 1)], sem)

    def start(r, c):
        row_copy(buf1, r, s_ref[0, 0, r]).start(priority=0)
        row_copy(buf2, r, s_ref[0, 1, r]).start(priority=1)
        return c

    lax.fori_loop(0, tb, start, 0, unroll=8)
    pltpu.make_async_copy(ys_ref.at[pl.ds(0, tb)], buf1, sem).wait()
    pltpu.make_async_copy(ys_ref.at[pl.ds(0, tb)], buf2, sem).wait()
    meta = meta_ref[...]
    x2 = x_ref[...] + (meta[:, 2:3] * buf1[...] + meta[:, 3:4] * buf2[...])
    if final_norm:
        ms = jnp.mean(x2 * x2, axis=-1, keepdims=True)
        x2 = x2 * lax.rsqrt(ms + EPS) * gfin_ref[...]
    y_ref[...] = x2


def _combine(x1, meta, slots, ys, g_final, final_norm):
    n, d = x1.shape
    tb = slots.shape[2]
    return pl.pallas_call(
        functools.partial(_combine_kernel, final_norm),
        grid=(n // tb,),
        in_specs=[pl.BlockSpec((tb, d), lambda i: (i, 0)),
                  pl.BlockSpec((tb, LANES), lambda i: (i, 0)),
                  pl.BlockSpec((1, SLOT_ROWS, tb), lambda i: (i, 0, 0), memory_space=pltpu.SMEM),
                  pl.BlockSpec(memory_space=pl.ANY),
                  pl.BlockSpec((1, d), lambda i: (0, 0))],
        out_specs=pl.BlockSpec((tb, d), lambda i: (i, 0)),
        out_shape=jax.ShapeDtypeStruct((n, d), F32),
        scratch_shapes=[pltpu.VMEM((tb, d), F32), pltpu.VMEM((tb, d), F32), pltpu.SemaphoreType.DMA(())],
        compiler_params=_cparams(("arbitrary",)),
        name="moe_combine_norm",
    )(x1, meta, slots, ys, g_final)


def _routed_moe(x1, g_ffn, wr_hi, wr_lo, b_r, w_gate, w_up, w_down, g_final, final_norm):
    n = x1.shape[0]
    meta, cnt = _router(x1, g_ffn, wr_hi, wr_lo, b_r, _row_tile(n))
    counts = cnt[0, N_GROUPS:N_GROUPS + N_EXPERTS].astype(jnp.int32)
    padded = (counts + EXPERT_TILE - 1) // EXPERT_TILE * EXPERT_TILE
    ends = jnp.cumsum(padded)
    offs = ends - padded
    n_tiles = 2 * n // EXPERT_TILE + N_EXPERTS
    n_slots = n_tiles * EXPERT_TILE
    tile_start = jnp.arange(n_tiles, dtype=jnp.int32) * EXPERT_TILE
    tile_expert = jnp.minimum(jnp.sum((tile_start[:, None] >= ends[None, :]).astype(jnp.int32), axis=1),
                              N_EXPERTS - 1)
    tile_valid = (tile_start < ends[-1]).astype(jnp.int32)
    offs_lanes = jnp.pad(offs.astype(F32), (N_GROUPS, LANES - N_GROUPS - N_EXPERTS))[None, :]
    slots = _slots(meta, offs_lanes, _row_tile(n))
    fill_hi = jnp.concatenate([ends[:-1], jnp.full((1,), n_slots, jnp.int32)])
    xs = _dispatch(x1, slots, offs + counts, fill_hi, n_slots)
    ys = _experts(xs, tile_expert, tile_valid, g_ffn, w_gate, w_up, w_down)
    return _combine(x1, meta, slots, ys, g_final, final_norm)


def _row_tile(n, tm=512):
    return tm if n % tm == 0 else n


def kernel(x_prompt, x_sample, cache_k_cmp, cache_v_cmp, cache_k_slc, cache_v_slc, state_k_win, state_v_win, page_table, g_mix, w_in, a_ln_g, a_ln_b, a_w_s, a_b_s, cmp_pe_k, cmp_w1_k, cmp_w2_k, cmp_pe_v, cmp_w1_v, cmp_w2_v, w_pa, w_pb, w_o, g_ffn, w_rg, b_rg, w_re, b_re, w_gate, w_up, w_down, g_final):
    depth = g_mix.shape[0]
    b, s, d = x_prompt.shape
    db, t, _ = x_sample.shape
    n_pages = page_table.shape[1]
    past = n_pages * PAGE_SIZE
    wl = min(WINDOW, s)
    assert s % Q_BLOCK == 0 and CHUNK % t == 0 and (db * t) % CHUNK == 0 and t < CMP_BLOCK

    tabs_p = _rope_tables(jnp.arange(s))
    tabs_s = _rope_tables(past + (jnp.arange(db * t) % t))
    prompt_pages = jnp.arange(b * s // PAGE_SIZE, dtype=jnp.int32).reshape(b, s // PAGE_SIZE)
    g_final2 = g_final[None, :]

    xp = x_prompt.reshape(b * s, d)
    xs = x_sample.reshape(db * t, d)
    outs_p = [[] for _ in range(6)]
    outs_s = [[] for _ in range(7)]
    for l in range(depth):
        w_a, w_kv = _layout_w_in(w_in[l])
        g_l = g_mix[l][None, :]
        ln_g, ln_b = a_ln_g[l][None, :], a_ln_b[l][None, :]
        cmp_k = (cmp_pe_k[l], cmp_w1_k[l], cmp_w2_k[l])
        cmp_v = (cmp_pe_v[l], cmp_w1_v[l], cmp_w2_v[l])
        wpa, wpb, wo = w_pa[l].astype(MM), w_pb[l].astype(MM), w_o[l].astype(MM)
        wr_hi, wr_lo, b_r = _router_weights(w_rg[l], b_rg[l], w_re[l], b_re[l])
        wg, wu, wd = w_gate[l].astype(MM), w_up[l].astype(MM), w_down[l].astype(MM)
        last = l == depth - 1

        def tail(x2d, a_out, o, gt):
            n = x2d.shape[0]
            m = _mix(a_out, o, wpa, wpb, gt, _row_tile(n, 1024))
            x1 = _out_proj(m, wo, x2d, _row_tile(n, 1024))
            moe_args = (x1, g_ffn[l][None, :], wr_hi, wr_lo, b_r, wg, wu, wd, g_final2)
            if n >= ROUTED_MIN_TOKENS:
                return _routed_moe(*moe_args, last)
            return _moe(*moe_args, _row_tile(n), last)

        a, gt, h = _project_a(xp, g_l, w_a, tabs_p, _row_tile(b * s, 1024))
        kc_r, vc_r, ks_r, vs_r, kw_r, vw_r, ks_b, vs_b, kw_b, vw_b = _project_kv(
            h, w_kv, tabs_p, _row_tile(b * s))
        mix_p, bias_p = _gmlp_mix(a_w_s[l], a_b_s[l], s)
        (a_out,) = _gmlp(a, ln_g, ln_b, mix_p, bias_p, False)
        pages = lambda r: r.reshape(1, b * s // PAGE_SIZE, PAGE_SIZE * KV_HEADS, HEAD_DIM)
        kc, vc = _compress(pages(kc_r), pages(vc_r), 0, prompt_pages, cmp_k, cmp_v)
        o = _prompt_attention(a, kc, vc, ks_b, vs_b, kw_b, vw_b, gt, b, s)
        xp = tail(xp, a_out, o, gt)
        heads = lambda r: r.reshape(b, s, KV_HEADS, HEAD_DIM)
        for lst, r in zip(outs_p, (kc_r, vc_r, ks_r, vs_r)):
            lst.append(heads(r))
        outs_p[4].append(heads(kw_r)[:, -wl:])
        outs_p[5].append(heads(vw_r)[:, -wl:])

        a, gt, h = _project_a(xs, g_l, w_a, tabs_s, _row_tile(db * t))
        kc_r, vc_r, ks_r, vs_r, kw_r, vw_r, _, _, _, _ = _project_kv(h, w_kv, tabs_s, _row_tile(db * t))
        mix_s, bias_s = _gmlp_mix(a_w_s[l], a_b_s[l], t)
        a_out, v_new = _gmlp(a, ln_g, ln_b, mix_s, bias_s, True)
        flat = lambda c: c.reshape(c.shape[0], c.shape[1], c.shape[2] * KV_HEADS, HEAD_DIM)
        kc, vc = _compress(flat(cache_k_cmp), flat(cache_v_cmp), l, page_table, cmp_k, cmp_v)
        r3 = lambda r: r.reshape(db, t, r.shape[-1])
        new = lambda r: r.reshape(db, t * KV_HEADS, HEAD_DIM)
        o, kwin, vwin = _sample_attention(
            r3(a), kc, vc, flat(cache_k_slc), flat(cache_v_slc), l, page_table,
            flat(state_k_win), flat(state_v_win), new(ks_r), new(vs_r), new(kw_r), new(vw_r), r3(gt))
        xs = tail(xs, a_out, o.reshape(db * t, Q_COLS), gt)
        r4 = lambda r: r.reshape(db, -1, KV_HEADS, HEAD_DIM)
        for lst, r in zip(outs_s, (kc_r, vc_r, ks_r, vs_r, kwin, vwin)):
            lst.append(r4(r))
        outs_s[6].append(v_new.reshape(db, t, A_WIDTH))

    y_prompt = xp.reshape(b, s, d)
    y_sample = xs.reshape(db, t, d)
    return (y_prompt, y_sample, *[jnp.stack(o) for o in outs_p], *[jnp.stack(o) for o in outs_s])
```
